```python
import jax, jax.numpy as jnp
from jax import lax
import numpy as np

D_MODEL = 1024
BATCH = 4
SEQ = 8192
DEPTH = 2

CTX_LEN = 256
GRID_W = 64
CHUNK = GRID_W
D_MIX = D_MODEL
M_W = D_MIX // 2
M_H = 4
M_DH = M_W // M_H
F_W = D_MIX // 4
F_G = 4
C_W = D_MIX // 4
C_G = 4
CONV_K = 31
FFN_HIDDEN = 2816
N_EXPERTS = 8
TOP_K = 2
EPS = 1e-6
N_DENSE = (DEPTH + 1) // 2
N_MOE = DEPTH // 2

O_Q = 0
O_K = O_Q + M_W
O_V = O_K + M_W
O_O = O_V + M_W
O_G = O_O + M_W
O_F = O_G + 4 * M_H
O_CU = O_F + F_W
O_CG = O_CU + C_W
P_IN = O_CG + C_W

kernel_name = "hybrid_mlstm_fourier_conformer_moe_dit"


def rmsnorm(x, g):
    xf = x.astype(jnp.float32)
    y = xf * lax.rsqrt(jnp.mean(xf * xf, axis=-1, keepdims=True) + EPS) * g.astype(jnp.float32)
    return y.astype(x.dtype)


def ada_modulation(cond, w_mod, b_mod):
    return jnp.split(jax.nn.silu(cond) @ w_mod + b_mod, 6, axis=-1)


def modulate(h, shift, scale):
    return h * (1 + scale) + shift


def mlstm_inputs(p, b_gate):
    bsz, n, _ = p.shape
    def heads(a):
        return jnp.transpose(a.reshape(bsz, n, M_H, M_DH), (0, 2, 1, 3)).astype(jnp.float32)
    q = heads(p[..., O_Q:O_K])
    k = heads(p[..., O_K:O_V]) * (M_DH ** -0.5)
    v = heads(p[..., O_V:O_O])
    o = p[..., O_O:O_G]
    g = p[..., O_G:O_F].reshape(bsz, n, 4, M_H) + b_gate.reshape(4, M_H)
    g = jnp.transpose(g, (2, 0, 3, 1)).astype(jnp.float32)
    return q, k, v, o, g


def mlstm_init_state(bsz):
    return (jnp.zeros((bsz, M_H, M_DH, M_DH), jnp.float32),
            jnp.zeros((bsz, M_H, M_DH), jnp.float32),
            jnp.full((bsz, M_H), -1e30, jnp.float32))


def mlstm_chunk_scan(q, k, v, ig, lf, state):
    bsz, nh, n, d = q.shape
    rows = n // CHUNK
    def to_chunks(a):
        return jnp.moveaxis(a.reshape(bsz, nh, rows, CHUNK, *a.shape[3:]), 2, 0)
    tril = jnp.tril(jnp.ones((CHUNK, CHUNK), dtype=bool))

    def step(carry, xs):
        c0, n0, m0 = carry
        qc, kc, vc, igc, lfc = xs
        bcum = jnp.cumsum(lfc, axis=-1)
        dmat = jnp.where(tril, bcum[..., :, None] - bcum[..., None, :] + igc[..., None, :], -jnp.inf)
        m_inter = bcum + m0[..., None]
        m_t = jnp.maximum(m_inter, jnp.max(dmat, axis=-1))
        s = jnp.einsum('bhtd,bhsd->bhts', qc, kc) * jnp.exp(dmat - m_t[..., None])
        inter = jnp.exp(m_inter - m_t)
        num = jnp.einsum('bhts,bhse->bhte', s, vc) + inter[..., None] * jnp.einsum('bhtd,bhde->bhte', qc, c0)
        den = jnp.sum(s, axis=-1) + inter * jnp.einsum('bhtd,bhd->bht', qc, n0)
        h = num / jnp.maximum(jnp.abs(den), jnp.exp(-m_t))[..., None]
        b_last = bcum[..., -1]
        gs = b_last[..., None] - bcum + igc
        m_new = jnp.maximum(b_last + m0, jnp.max(gs, axis=-1))
        wk = jnp.exp(gs - m_new[..., None])
        decay = jnp.exp(b_last + m0 - m_new)
        c_new = decay[..., None, None] * c0 + jnp.einsum('bhs,bhsd,bhse->bhde', wk, kc, vc)
        n_new = decay[..., None] * n0 + jnp.einsum('bhs,bhsd->bhd', wk, kc)
        return (c_new, n_new, m_new), h

    state, hs = lax.scan(step, state, (to_chunks(q), to_chunks(k), to_chunks(v), to_chunks(ig), to_chunks(lf)))
    return jnp.moveaxis(hs, 0, 2).reshape(bsz, nh, n, d), state


def flip_seq(a):
    return jnp.flip(a, axis=2)


def mlstm_output(h_sum, o, norm_g):
    bsz, _, n, _ = h_sum.shape
    h = jnp.swapaxes(h_sum, 1, 2)
    h = h * lax.rsqrt(jnp.mean(h * h, axis=-1, keepdims=True) + EPS) * norm_g.reshape(M_H, M_DH).astype(jnp.float32)
    return (jax.nn.sigmoid(o.astype(jnp.float32)) * h.reshape(bsz, n, M_W)).astype(o.dtype)


def bidir_scan(q, k, v, g, init_f, init_b):
    h_f, s_f = mlstm_chunk_scan(q, k, v, g[0], jax.nn.log_sigmoid(g[1]), init_f)
    h_b, s_b = mlstm_chunk_scan(flip_seq(q), flip_seq(k), flip_seq(v), flip_seq(g[2]),
                                flip_seq(jax.nn.log_sigmoid(g[3])), init_b)
    return h_f + flip_seq(h_b), s_f, s_b


def mlstm_mixer(p_lat, p_ctx, b_gate, norm_g, need_ctx):
    ql, kl, vl, ol, gl = mlstm_inputs(p_lat, b_gate)
    qc, kc, vc, oc, gc = mlstm_inputs(p_ctx, b_gate)
    s0 = mlstm_init_state(p_lat.shape[0])
    h_ctx, s_f, s_b = bidir_scan(qc, kc, vc, gc, s0, s0)
    h_lat, _, _ = bidir_scan(ql, kl, vl, gl, s_f, s_b)
    y_lat = mlstm_output(h_lat, ol, norm_g)
    y_ctx = mlstm_output(h_ctx, oc, norm_g) if need_ctx else None
    return y_lat, y_ctx


def fourier_mix(u):
    bsz, n, _ = u.shape
    z = u.astype(jnp.float32).reshape(bsz, n, F_G, F_W // F_G)
    y = jnp.real(jnp.fft.fftn(z, axes=(1, 3), norm="ortho"))
    return y.reshape(bsz, n, F_W).astype(u.dtype)


def conv_module(u, g, dw, db, ln_g, ln_b):
    bsz, n, _ = u.shape
    a = u * jax.nn.sigmoid(g)
    y = lax.conv_general_dilated(a, dw[:, None, :], window_strides=(1,),
                                 padding=[(CONV_K // 2, CONV_K // 2)],
                                 dimension_numbers=('NWC', 'WIO', 'NWC'),
                                 feature_group_count=C_W) + db
    yf = y.astype(jnp.float32).reshape(bsz, n, C_G, C_W // C_G)
    mu = jnp.mean(yf, axis=-1, keepdims=True)
    var = jnp.mean(jnp.square(yf - mu), axis=-1, keepdims=True)
    yf = ((yf - mu) * lax.rsqrt(var + EPS)).reshape(bsz, n, C_W) * ln_g.astype(jnp.float32) + ln_b.astype(jnp.float32)
    return jax.nn.silu(yf).astype(u.dtype)


def merge_heads(p, y_mlstm, dw, db, ln_g, ln_b, w_out):
    y_f = fourier_mix(p[..., O_F:O_CU])
    y_c = conv_module(p[..., O_CU:O_CG], p[..., O_CG:P_IN], dw, db, ln_g, ln_b)
    return jnp.concatenate([y_mlstm, y_f, y_c], axis=-1) @ w_out


def swiglu(h, w_gate_up, w_down):
    gt, up = jnp.split(h @ w_gate_up, 2, axis=-1)
    return (jax.nn.silu(gt) * up) @ w_down


def moe_swiglu(h, w_router, b_router, w_gate_up, w_down):
    logits = (h @ w_router).astype(jnp.float32) + b_router.astype(jnp.float32)
    top_v, top_i = lax.top_k(logits, TOP_K)
    top_w = jax.nn.softmax(top_v, axis=-1)
    gates = jnp.sum(jax.nn.one_hot(top_i, N_EXPERTS, dtype=jnp.float32) * top_w[..., None], axis=-2).astype(h.dtype)
    y = jnp.zeros_like(h)
    for e in range(N_EXPERTS):
        y = y + gates[..., e:e + 1] * swiglu(h, w_gate_up[e], w_down[e])
    return y


def channel_mixer(h, layer, ffn_w_gate_up, ffn_w_down, moe_w_router, moe_b_router, moe_w_gate_up, moe_w_down):
    idx = layer // 2
    if layer % 2 == 0:
        return swiglu(h, ffn_w_gate_up[idx], ffn_w_down[idx])
    return moe_swiglu(h, moe_w_router[idx], moe_b_router[idx], moe_w_gate_up[idx], moe_w_down[idx])


def setup_inputs(seed: int = 0) -> dict:
    key = jax.random.key(seed)
    ks = jax.random.split(key, 26)
    D = D_MODEL
    def nrm(k, shape, s):
        return jax.random.normal(k, shape, jnp.float32) * s
    f_bias = jnp.linspace(3.0, 6.0, M_H, dtype=jnp.float32)
    b_gate = jnp.concatenate([
        nrm(ks[9], (DEPTH, M_H), 0.1),
        f_bias + nrm(ks[10], (DEPTH, M_H), 0.1),
        nrm(ks[11], (DEPTH, M_H), 0.1),
        f_bias + nrm(ks[12], (DEPTH, M_H), 0.1)], axis=-1)
    return {
        "x": nrm(ks[0], (BATCH, SEQ, D), 1.0),
        "c": nrm(ks[1], (BATCH, D), 1.0),
        "ctx": nrm(ks[2], (BATCH, CTX_LEN, D), 1.0),
        "c_ctx": nrm(ks[3], (D,), 1.0),
        "norm1_g": 1.0 + nrm(ks[4], (DEPTH, D), 0.02),
        "norm2_g": 1.0 + nrm(ks[5], (DEPTH, D), 0.02),
        "w_mod": nrm(ks[6], (DEPTH, D, 6 * D), 0.5 * D ** -0.5),
        "b_mod": nrm(ks[7], (DEPTH, 6 * D), 0.02),
        "w_in": nrm(ks[8], (DEPTH, D, P_IN), D ** -0.5),
        "b_gate": b_gate,
        "mlstm_norm_g": 1.0 + nrm(ks[13], (DEPTH, M_W), 0.02),
        "conv_dw": nrm(ks[14], (DEPTH, CONV_K, C_W), CONV_K ** -0.5),
        "conv_db": nrm(ks[15], (DEPTH, C_W), 0.02),
        "conv_norm_g": 1.0 + nrm(ks[16], (DEPTH, C_W), 0.02),
        "conv_norm_b": nrm(ks[17], (DEPTH, C_W), 0.02),
        "w_out": nrm(ks[18], (DEPTH, D_MIX, D), D_MIX ** -0.5),
        "ffn_w_gate_up": nrm(ks[19], (N_DENSE, D, 2 * FFN_HIDDEN), D ** -0.5),
        "ffn_w_down": nrm(ks[20], (N_DENSE, FFN_HIDDEN, D), FFN_HIDDEN ** -0.5),
        "moe_w_router": nrm(ks[21], (N_MOE, D, N_EXPERTS), D ** -0.5),
        "moe_b_router": nrm(ks[22], (N_MOE, N_EXPERTS), 0.01),
        "moe_w_gate_up": nrm(ks[23], (N_MOE, N_EXPERTS, D, 2 * FFN_HIDDEN), D ** -0.5),
        "moe_w_down": nrm(ks[24], (N_MOE, N_EXPERTS, FFN_HIDDEN, D), FFN_HIDDEN ** -0.5),
        "final_norm_g": 1.0 + nrm(ks[25], (D,), 0.02),
    }


def reference(x, c, ctx, c_ctx, norm1_g, norm2_g, w_mod, b_mod, w_in, b_gate, mlstm_norm_g,
              conv_dw, conv_db, conv_norm_g, conv_norm_b, w_out, ffn_w_gate_up, ffn_w_down,
              moe_w_router, moe_b_router, moe_w_gate_up, moe_w_down, final_norm_g):
    xc = ctx
    for layer in range(DEPTH):
        need_ctx = layer < DEPTH - 1
        sh1, sc1, g1, sh2, sc2, g2 = [m[:, None, :] for m in ada_modulation(c, w_mod[layer], b_mod[layer])]
        csh1, csc1, cg1, csh2, csc2, cg2 = ada_modulation(c_ctx, w_mod[layer], b_mod[layer])

        px = modulate(rmsnorm(x, norm1_g[layer]), sh1, sc1) @ w_in[layer]
        pc = modulate(rmsnorm(xc, norm1_g[layer]), csh1, csc1) @ w_in[layer]
        m_lat, m_ctx = mlstm_mixer(px, pc, b_gate[layer], mlstm_norm_g[layer], need_ctx)
        x = x + g1 * merge_heads(px, m_lat, conv_dw[layer], conv_db[layer], conv_norm_g[layer],
                                 conv_norm_b[layer], w_out[layer])
        if need_ctx:
            xc = xc + cg1 * merge_heads(pc, m_ctx, conv_dw[layer], conv_db[layer], conv_norm_g[layer],
                                        conv_norm_b[layer], w_out[layer])

        hx = modulate(rmsnorm(x, norm2_g[layer]), sh2, sc2)
        x = x + g2 * channel_mixer(hx, layer, ffn_w_gate_up, ffn_w_down, moe_w_router, moe_b_router,
                                   moe_w_gate_up, moe_w_down)
        if need_ctx:
            hc = modulate(rmsnorm(xc, norm2_g[layer]), csh2, csc2)
            xc = xc + cg2 * channel_mixer(hc, layer, ffn_w_gate_up, ffn_w_down, moe_w_router, moe_b_router,
                                          moe_w_gate_up, moe_w_down)
    return rmsnorm(x, final_norm_g)
```

```python
import functools
import math

import numpy as np
import jax
import jax.numpy as jnp
from jax import lax
from jax.experimental import pallas as pl
from jax.experimental.pallas import tpu as pltpu

F32 = jnp.float32
BF16 = jnp.bfloat16

D_MODEL = 1024
M_W = 512
M_H = 4
M_DH = 128
F_W = 256
F_G = 4
C_W = 256
C_G = 4
CONV_K = 31
FFN_HIDDEN = 2816
N_EXPERTS = 8
EPS = 1e-6

O_Q = 0
O_K = O_Q + M_W
O_V = O_K + M_W
O_O = O_V + M_W
O_G = O_O + M_W
O_F = O_G + 4 * M_H
O_CU = O_F + F_W
O_CG = O_CU + C_W
P_IN = O_CG + C_W

LC = 256
DFT_N1 = 128
ROW_TILE = 512
HID_TILE = 256
CONV_TILE = 256
CONV_PAD = 16
VMEM_LIMIT = 56 * 1024 * 1024


def _params(*sem):
    return pltpu.CompilerParams(dimension_semantics=sem, vmem_limit_bytes=VMEM_LIMIT)


def _const_spec(shape):
    zeros = (0,) * len(shape)
    return pl.BlockSpec(shape, lambda *_: zeros)


def _dot(a, b):
    return jnp.dot(a, b, preferred_element_type=F32)


def _dot_nt(a, b):
    return lax.dot_general(a, b, (((1,), (1,)), ((), ())), preferred_element_type=F32)


def _split2(a):
    hi = a.astype(BF16)
    lo = (a - hi.astype(F32)).astype(BF16)
    return hi, lo


def _split3(a):
    x1 = a.astype(BF16)
    r1 = a - x1.astype(F32)
    x2 = r1.astype(BF16)
    x3 = (r1 - x2.astype(F32)).astype(BF16)
    return x1, x2, x3


def _dot_precise(a, b):
    ah, al = _split2(a)
    bh, bl = _split2(b)
    return _dot(ah, bh) + _dot(al, bh) + _dot(ah, bl)


def _silu(x):
    return x * jax.nn.sigmoid(x)


def _log_sigmoid(x):
    return -(jnp.maximum(-x, 0.0) + jnp.log1p(jnp.exp(-jnp.abs(x))))


def _rms(x):
    return x * lax.rsqrt(jnp.mean(x * x, axis=-1, keepdims=True) + EPS)


def _mod_kernel(c_ref, w_ref, b_ref, o_ref):
    o_ref[...] = _dot_precise(_silu(c_ref[...]), w_ref[...]) + b_ref[...]


def _modulation(cond8, w_mod, b_mod):
    d = cond8.shape[1]
    n_out = w_mod.shape[1]
    tn = 1536
    return pl.pallas_call(
        _mod_kernel,
        grid=(n_out // tn,),
        in_specs=[_const_spec((8, d)),
                  pl.BlockSpec((d, tn), lambda j: (0, j)),
                  pl.BlockSpec((1, tn), lambda j: (0, j))],
        out_specs=pl.BlockSpec((8, tn), lambda j: (0, j)),
        out_shape=jax.ShapeDtypeStruct((8, n_out), F32),
        compiler_params=_params("arbitrary"),
        name="modulation",
    )(cond8, w_mod, b_mod.reshape(1, n_out))


def _in_kernel(x_ref, mod_ref, g_ref, wm_ref, wkt_ref, wgt_ref, bg_ref,
               qvo_ref, kt_ref, zf_ref, cc_ref, gt_ref):
    tm = x_ref.shape[1]
    h = _rms(x_ref[0]) * g_ref[...]
    h = h * (1.0 + mod_ref[0, 1:2, :]) + mod_ref[0, 0:1, :]
    hb = h.astype(BF16)
    p = _dot(hb, wm_ref[...])
    qvo_ref[0] = p[:, :3 * M_W].astype(BF16)
    zf_ref[0] = p[:, 3 * M_W:3 * M_W + F_W].astype(BF16)
    cc_ref[0] = p[:, 3 * M_W + F_W:].astype(BF16)
    kt_ref[0] = (_dot_nt(wkt_ref[...], hb) * (M_DH ** -0.5)).astype(BF16)
    gt = _dot_nt(wgt_ref[...], hb) + bg_ref[...]
    for j in range(tm // LC):
        gt_ref[0, j] = gt[:, j * LC:(j + 1) * LC]


def _in_proj(x, modv, norm_g, wm, wkt, wgt, bg):
    b, n, d = x.shape
    tm = min(ROW_TILE, n)
    nm = wm.shape[1]
    out_shape = (
        jax.ShapeDtypeStruct((b, n, 3 * M_W), BF16),
        jax.ShapeDtypeStruct((b, M_W, n), BF16),
        jax.ShapeDtypeStruct((b, n, F_W), BF16),
        jax.ShapeDtypeStruct((b, n, 2 * C_W), BF16),
        jax.ShapeDtypeStruct((b, n // LC, 16, LC), F32),
    )
    return pl.pallas_call(
        _in_kernel,
        grid=(b, n // tm),
        in_specs=[pl.BlockSpec((1, tm, d), lambda i, j: (i, j, 0)),
                  pl.BlockSpec((1, 8, d), lambda i, j: (i, 0, 0)),
                  _const_spec((1, d)),
                  _const_spec((d, nm)),
                  _const_spec((M_W, d)),
                  _const_spec((16, d)),
                  _const_spec((16, 1))],
        out_specs=(pl.BlockSpec((1, tm, 3 * M_W), lambda i, j: (i, j, 0)),
                   pl.BlockSpec((1, M_W, tm), lambda i, j: (i, 0, j)),
                   pl.BlockSpec((1, tm, F_W), lambda i, j: (i, j, 0)),
                   pl.BlockSpec((1, tm, 2 * C_W), lambda i, j: (i, j, 0)),
                   pl.BlockSpec((1, tm // LC, 16, LC), lambda i, j: (i, j, 0, 0))),
        out_shape=out_shape,
        compiler_params=_params("parallel", "arbitrary"),
        name="in_proj",
    )(x, modv, norm_g.reshape(1, d), wm, wkt, wgt, bg)


def _mlstm_kernel(qf_ref, vf_ref, ktf_ref, gtf_ref, qb_ref, vb_ref, ktb_ref, gtb_ref,
                  cf0_ref, cb0_ref, mf0_ref, mb0_ref, u3_ref, lo3_ref, ut3_ref, lot3_ref,
                  hf_ref, hb_ref, cf_ref, cb_ref, mf_ref, mb_ref):
    @pl.when(pl.program_id(1) == 0)
    def _():
        cf_ref[...] = cf0_ref[...]
        cb_ref[...] = cb0_ref[...]
        mf_ref[...] = mf0_ref[...]
        mb_ref[...] = mb0_ref[...]

    gf = gtf_ref[0, 0]
    gb = gtb_ref[0, 0]
    r16 = lax.broadcasted_iota(jnp.int32, (16, LC), 0)
    lf = jnp.where((r16 >= 4) & (r16 < 8), _log_sigmoid(gf),
                   jnp.where(r16 >= 12, _log_sigmoid(gb), 0.0))
    lf = jnp.concatenate([lf, jnp.zeros((128 - 16, LC), F32)], axis=0)
    xs = jnp.concatenate(_split3(lf), axis=1)
    rows_f = _dot(xs[0:16], u3_ref[...])
    rows_b = _dot(xs[0:16], lo3_ref[...])
    cols_f = _dot_nt(ut3_ref[...], xs)
    cols_b = _dot_nt(lot3_ref[...], xs)

    t_idx = lax.broadcasted_iota(jnp.int32, (LC, LC), 0)
    s_idx = lax.broadcasted_iota(jnp.int32, (LC, LC), 1)
    ones_col = (lax.broadcasted_iota(jnp.int32, (LC, M_DH), 1) == 0).astype(BF16)

    dirs = (
        (qf_ref, vf_ref, ktf_ref, gf, rows_f, cols_f, s_idx <= t_idx, cf_ref, mf_ref, hf_ref, 0, 4, LC - 1),
        (qb_ref, vb_ref, ktb_ref, gb, rows_b, cols_b, s_idx >= t_idx, cb_ref, mb_ref, hb_ref, 8, 12, 0),
    )
    for q_ref, v_ref, kt_ref, g, rows, cols, mask, c_ref, m_ref, h_ref, i_off, f_off, last in dirs:
        for h in range(M_H):
            hs = slice(h * M_DH, (h + 1) * M_DH)
            q = q_ref[0, :, hs]
            kt = kt_ref[0, hs, :]
            vaug = jnp.concatenate([v_ref[0, :, hs], ones_col], axis=1)
            i_row = g[i_off + h:i_off + h + 1, :]
            b_row = rows[f_off + h:f_off + h + 1, :]
            b_col = cols[:, f_off + h:f_off + h + 1]
            b_last = b_row[:, last:last + 1]
            c0 = c_ref[0, h]
            m0 = m_ref[0, h, 0:1, 0:1]

            dmat = jnp.where(mask, b_col + (i_row - b_row), -jnp.inf)
            m_inter = b_col + m0
            m_t = jnp.maximum(m_inter, jnp.max(dmat, axis=-1, keepdims=True))
            s = (_dot(q, kt) * jnp.exp(dmat - m_t)).astype(BF16)
            inter = jnp.exp(m_inter - m_t)
            na = _dot(s, vaug) + inter * _dot(q, c0.astype(BF16))
            den = na[:, M_DH:M_DH + 1]
            h_ref[0, :, hs] = na[:, :M_DH] / jnp.maximum(jnp.abs(den), jnp.exp(-m_t))

            gs = b_last - b_row + i_row
            m_new = jnp.maximum(b_last + m0, jnp.max(gs, axis=-1, keepdims=True))
            kw = (kt.astype(F32) * jnp.exp(gs - m_new)).astype(BF16)
            c_ref[0, h] = jnp.exp(b_last + m0 - m_new) * c0 + _dot(kw, vaug)
            m_ref[0, h] = jnp.broadcast_to(m_new, (8, 128))


def _tri_constants():
    s = np.arange(LC)[:, None]
    t = np.arange(LC)[None, :]
    u = (s <= t).astype(np.float32)
    lo = (s >= t).astype(np.float32)
    u3 = np.concatenate([u, u, u], axis=0)
    lo3 = np.concatenate([lo, lo, lo], axis=0)
    return (jnp.asarray(u3, BF16), jnp.asarray(lo3, BF16),
            jnp.asarray(u3.T.copy(), BF16), jnp.asarray(lo3.T.copy(), BF16))


def _mlstm(qvo, kt, gt, state_f, state_b):
    b, n, _ = qvo.shape
    nc = n // LC
    cf0, mf0 = state_f
    cb0, mb0 = state_b
    consts = _tri_constants()
    fwd = lambda i, c: (i, c, 0)
    bwd = lambda i, c: (i, nc - 1 - c, 0)
    st_c = pl.BlockSpec((1, M_H, M_DH, 2 * M_DH), lambda i, c: (i, 0, 0, 0))
    st_m = pl.BlockSpec((1, M_H, 8, 128), lambda i, c: (i, 0, 0, 0))
    h_sds = jax.ShapeDtypeStruct((b, n, M_W), F32)
    return pl.pallas_call(
        _mlstm_kernel,
        grid=(b, nc),
        in_specs=[pl.BlockSpec((1, LC, M_W), fwd),
                  pl.BlockSpec((1, LC, M_W), lambda i, c: (i, c, 1)),
                  pl.BlockSpec((1, M_W, LC), lambda i, c: (i, 0, c)),
                  pl.BlockSpec((1, 1, 16, LC), lambda i, c: (i, c, 0, 0)),
                  pl.BlockSpec((1, LC, M_W), bwd),
                  pl.BlockSpec((1, LC, M_W), lambda i, c: (i, nc - 1 - c, 1)),
                  pl.BlockSpec((1, M_W, LC), lambda i, c: (i, 0, nc - 1 - c)),
                  pl.BlockSpec((1, 1, 16, LC), lambda i, c: (i, nc - 1 - c, 0, 0)),
                  st_c, st_c, st_m, st_m,
                  _const_spec((3 * LC, LC)), _const_spec((3 * LC, LC)),
                  _const_spec((LC, 3 * LC)), _const_spec((LC, 3 * LC))],
        out_specs=(pl.BlockSpec((1, LC, M_W), fwd), pl.BlockSpec((1, LC, M_W), bwd),
                   st_c, st_c, st_m, st_m),
        out_shape=(h_sds, h_sds,
                   jax.ShapeDtypeStruct(cf0.shape, F32), jax.ShapeDtypeStruct(cb0.shape, F32),
                   jax.ShapeDtypeStruct(mf0.shape, F32), jax.ShapeDtypeStruct(mb0.shape, F32)),
        compiler_params=_params("parallel", "arbitrary"),
        name="mlstm",
    )(qvo, qvo, kt, gt, qvo, qvo, kt, gt, cf0, cb0, mf0, mb0, *consts)


def _channel_dft_matrix():
    gw = F_W // F_G
    j = np.arange(gw)[:, None]
    l = np.arange(gw)[None, :]
    ang = 2.0 * np.pi * j * l / gw
    eye = np.eye(F_G)
    bc = np.kron(eye, np.cos(ang)) / math.sqrt(gw)
    bs = np.kron(eye, np.sin(ang)) / math.sqrt(gw)
    return jnp.asarray(np.concatenate([bc, -bs], axis=1), BF16)


def _dft_cos_sin(n):
    k = np.arange(n)[:, None]
    m = np.arange(n)[None, :]
    ang = 2.0 * np.pi * ((k * m) % n) / n
    return np.cos(ang) / math.sqrt(n), np.sin(ang) / math.sqrt(n)


def _fourier1_kernel(z_ref, bcs_ref, f1_ref, ct_ref, st_ref, o_ref):
    n2 = o_ref.shape[2]

    def body(j, carry):
        off = pl.multiple_of(j * F_W, F_W)
        ab = _dot(z_ref[0, :, pl.ds(off, F_W)], bcs_ref[...])
        u = jnp.concatenate([ab[:, :F_W], ab[:, F_W:]], axis=0).astype(BF16)
        t = _dot(f1_ref[...], u)
        tre, tim = t[:DFT_N1], t[DFT_N1:]
        ct = jnp.concatenate([ct_ref[j], ct_ref[j]], axis=1)
        st = jnp.concatenate([st_ref[j], st_ref[j]], axis=1)
        o_ref[0, 0, j] = (tre * ct + tim * st).astype(BF16)
        o_ref[0, 1, j] = (tim * ct - tre * st).astype(BF16)
        return carry

    lax.fori_loop(0, n2, body, 0)


def _fourier2_kernel(t_ref, f2_ref, o_ref):
    o_ref[0] = _dot(f2_ref[...], t_ref[0]).astype(o_ref.dtype)


def _fourier_direct_kernel(z_ref, bcs_ref, f_ref, o_ref):
    ab = _dot(z_ref[0], bcs_ref[...])
    u = jnp.concatenate([ab[:, :F_W], ab[:, F_W:]], axis=0).astype(BF16)
    o_ref[0] = _dot(f_ref[...], u).astype(o_ref.dtype)


def _fourier(zf):
    b, n, _ = zf.shape
    bcs = _channel_dft_matrix()
    if n <= 512:
        c, s = _dft_cos_sin(n)
        fmat = jnp.asarray(np.concatenate([c, s], axis=1), BF16)
        return pl.pallas_call(
            _fourier_direct_kernel,
            grid=(b,),
            in_specs=[pl.BlockSpec((1, n, F_W), lambda i: (i, 0, 0)),
                      _const_spec((F_W, 2 * F_W)), _const_spec((n, 2 * n))],
            out_specs=pl.BlockSpec((1, n, F_W), lambda i: (i, 0, 0)),
            out_shape=jax.ShapeDtypeStruct((b, n, F_W), BF16),
            compiler_params=_params("parallel"),
            name="fourier_direct",
        )(zf, bcs, fmat)

    n1 = DFT_N1
    n2 = n // n1
    c1, s1 = _dft_cos_sin(n1)
    f1 = jnp.asarray(np.block([[c1, s1], [-s1, c1]]), BF16)
    k1 = np.arange(n1)[None, :]
    m2 = np.arange(n2)[:, None]
    ang = 2.0 * np.pi * ((m2 * k1) % n) / n
    ct = jnp.asarray(np.broadcast_to(np.cos(ang)[:, :, None], (n2, n1, 128)), F32)
    st = jnp.asarray(np.broadcast_to(np.sin(ang)[:, :, None], (n2, n1, 128)), F32)
    c2, s2 = _dft_cos_sin(n2)
    f2 = jnp.asarray(np.concatenate([c2, s2], axis=1), BF16)

    t = pl.pallas_call(
        _fourier1_kernel,
        grid=(b,),
        in_specs=[pl.BlockSpec((1, n1, n2 * F_W), lambda i: (i, 0, 0)),
                  _const_spec((F_W, 2 * F_W)), _const_spec((2 * n1, 2 * n1)),
                  _const_spec((n2, n1, 128)), _const_spec((n2, n1, 128))],
        out_specs=pl.BlockSpec((1, 2, n2, n1, F_W), lambda i: (i, 0, 0, 0, 0)),
        out_shape=jax.ShapeDtypeStruct((b, 2, n2, n1, F_W), BF16),
        compiler_params=_params("parallel"),
        name="fourier_stage1",
    )(zf.reshape(b, n1, n2 * F_W), bcs, f1, ct, st)
    width = n1 * F_W
    tw = min(4096, width)
    y = pl.pallas_call(
        _fourier2_kernel,
        grid=(b, width // tw),
        in_specs=[pl.BlockSpec((1, 2 * n2, tw), lambda i, j: (i, 0, j)),
                  _const_spec((n2, 2 * n2))],
        out_specs=pl.BlockSpec((1, n2, tw), lambda i, j: (i, 0, j)),
        out_shape=jax.ShapeDtypeStruct((b, n2, width), BF16),
        compiler_params=_params("parallel", "arbitrary"),
        name="fourier_stage2",
    )(t.reshape(b, 2 * n2, width), f2)
    return y.reshape(b, n, F_W)


def _group_mean(y, gm_ref):
    hi, lo = _split2(y)
    return _dot(hi, gm_ref[...]) + _dot(lo, gm_ref[...])


def _conv_kernel(cc_ref, dw_ref, db_ref, lg_ref, lb_ref, gm_ref, o_ref, a_ref):
    n = cc_ref.shape[1]
    rt = min(CONV_TILE, n)
    a_ref[0:CONV_PAD, :] = jnp.zeros((CONV_PAD, C_W), F32)
    a_ref[n + CONV_PAD:n + 2 * CONV_PAD, :] = jnp.zeros((CONV_PAD, C_W), F32)

    def glu(i, carry):
        r = pl.multiple_of(i * rt, rt)
        blk = cc_ref[0, pl.ds(r, rt), :].astype(F32)
        a_ref[pl.ds(r + CONV_PAD, rt), :] = blk[:, :C_W] * jax.nn.sigmoid(blk[:, C_W:])
        return carry

    lax.fori_loop(0, n // rt, glu, 0)

    def conv(i, carry):
        r = pl.multiple_of(i * rt, rt)
        win = a_ref[pl.ds(r, rt + 2 * CONV_PAD), :]
        span = rt + 2 * CONV_PAD - 8
        phases = [win[s:s + span] for s in range(8)]
        acc = jnp.broadcast_to(db_ref[...], (rt, C_W))
        for j in range(CONV_K):
            off = CONV_PAD - CONV_K // 2 + j
            base = 8 * (off // 8)
            acc = acc + phases[off % 8][base:base + rt] * dw_ref[j:j + 1, :]
        d = acc - _group_mean(acc, gm_ref)
        var = _group_mean(d * d, gm_ref)
        yn = d * lax.rsqrt(var + EPS) * lg_ref[...] + lb_ref[...]
        o_ref[0, pl.ds(r, rt), :] = _silu(yn).astype(o_ref.dtype)
        return carry

    lax.fori_loop(0, n // rt, conv, 0)


def _conv(cc, dw, db, ln_g, ln_b):
    b, n, _ = cc.shape
    gw = C_W // C_G
    gm = jnp.asarray(np.kron(np.eye(C_G), np.full((gw, gw), 1.0 / gw)), BF16)
    dw32 = jnp.concatenate([dw, jnp.zeros((32 - CONV_K, C_W), F32)], axis=0)
    return pl.pallas_call(
        _conv_kernel,
        grid=(b,),
        in_specs=[pl.BlockSpec((1, n, 2 * C_W), lambda i: (i, 0, 0)),
                  _const_spec((32, C_W)), _const_spec((1, C_W)), _const_spec((1, C_W)),
                  _const_spec((1, C_W)), _const_spec((C_W, C_W))],
        out_specs=pl.BlockSpec((1, n, C_W), lambda i: (i, 0, 0)),
        out_shape=jax.ShapeDtypeStruct((b, n, C_W), BF16),
        scratch_shapes=[pltpu.VMEM((n + 2 * CONV_PAD, C_W), F32)],
        compiler_params=_params("parallel"),
        name="conv_module",
    )(cc, dw32, db.reshape(1, C_W), ln_g.reshape(1, C_W), ln_b.reshape(1, C_W), gm)


def _top2_gates(logits):
    lane = lax.broadcasted_iota(jnp.int32, logits.shape, 1)
    lg = jnp.where(lane < N_EXPERTS, logits, -jnp.inf)
    v1 = jnp.max(lg, axis=-1, keepdims=True)
    i1 = jnp.min(jnp.where(lg == v1, lane, 128), axis=-1, keepdims=True)
    lg2 = jnp.where(lane == i1, -jnp.inf, lg)
    v2 = jnp.max(lg2, axis=-1, keepdims=True)
    i2 = jnp.min(jnp.where(lg2 == v2, lane, 128), axis=-1, keepdims=True)
    e2 = jnp.exp(v2 - v1)
    w1 = 1.0 / (1.0 + e2)
    w2 = e2 / (1.0 + e2)
    return jnp.where(lane == i1, w1, jnp.where(lane == i2, w2, 0.0))


def _out_kernel(hf_ref, hb_ref, o_ref, yf_ref, yc_ref, x_ref, mod_ref, hg_ref, g2_ref, wo_ref,
                *rest, with_router):
    if with_router:
        wr_ref, br_ref, xo_ref, hx_ref, gates_ref = rest
    else:
        xo_ref, hx_ref = rest
    hsum = hf_ref[0] + hb_ref[0]
    heads = [_rms(hsum[:, h * M_DH:(h + 1) * M_DH]) for h in range(M_H)]
    ym = jax.nn.sigmoid(o_ref[0].astype(F32)) * (jnp.concatenate(heads, axis=1) * hg_ref[...])
    proj = (_dot(ym.astype(BF16), wo_ref[0:M_W, :])
            + _dot(yf_ref[0], wo_ref[M_W:M_W + F_W, :])
            + _dot(yc_ref[0], wo_ref[M_W + F_W:, :]))
    xn = x_ref[0] + mod_ref[0, 2:3, :] * proj
    xo_ref[0] = xn
    hx = (_rms(xn) * g2_ref[...]) * (1.0 + mod_ref[0, 4:5, :]) + mod_ref[0, 3:4, :]
    hx_ref[0] = hx.astype(BF16)
    if with_router:
        gates_ref[0] = _top2_gates(_dot_precise(hx, wr_ref[...]) + br_ref[...])


def _out_proj(hf, hb, qvo, yf, yc, x, modv, head_g, norm2_g, w_out, router=None):
    b, n, d = x.shape
    tm = min(ROW_TILE, n)
    row = lambda w: pl.BlockSpec((1, tm, w), lambda i, j: (i, j, 0))
    in_specs = [row(M_W), row(M_W),
                pl.BlockSpec((1, tm, M_W), lambda i, j: (i, j, 2)),
                row(F_W), row(C_W), row(d),
                pl.BlockSpec((1, 8, d), lambda i, j: (i, 0, 0)),
                _const_spec((1, M_W)), _const_spec((1, d)), _const_spec((d, d))]
    args = [hf, hb, qvo, yf, yc, x, modv, head_g.reshape(1, M_W), norm2_g.reshape(1, d), w_out]
    out_specs = [row(d), row(d)]
    out_shape = [jax.ShapeDtypeStruct((b, n, d), F32), jax.ShapeDtypeStruct((b, n, d), BF16)]
    if router is not None:
        in_specs += [_const_spec((d, 128)), _const_spec((1, 128))]
        args += list(router)
        out_specs.append(row(128))
        out_shape.append(jax.ShapeDtypeStruct((b, n, 128), F32))
    return pl.pallas_call(
        functools.partial(_out_kernel, with_router=router is not None),
        grid=(b, n // tm),
        in_specs=in_specs,
        out_specs=tuple(out_specs),
        out_shape=tuple(out_shape),
        compiler_params=_params("parallel", "arbitrary"),
        name="out_proj",
    )(*args)


def _swiglu_tile(hx, wgu_ref, wd_ref):
    acc = jnp.zeros((hx.shape[0], D_MODEL), F32)
    for j in range(FFN_HIDDEN // HID_TILE):
        cs = slice(j * HID_TILE, (j + 1) * HID_TILE)
        us = slice(FFN_HIDDEN + j * HID_TILE, FFN_HIDDEN + (j + 1) * HID_TILE)
        a = _silu(_dot(hx, wgu_ref[:, cs])) * _dot(hx, wgu_ref[:, us])
        acc = acc + _dot(a.astype(BF16), wd_ref[cs, :])
    return acc


def _ffn_kernel(hx_ref, x_ref, mod_ref, wgu_ref, wd_ref, o_ref):
    o_ref[0] = x_ref[0] + mod_ref[0, 5:6, :] * _swiglu_tile(hx_ref[0], wgu_ref, wd_ref)


def _ffn(hx, x, modv, wgu, wd):
    b, n, d = x.shape
    tm = min(ROW_TILE, n)
    row = lambda: pl.BlockSpec((1, tm, d), lambda i, j: (i, j, 0))
    return pl.pallas_call(
        _ffn_kernel,
        grid=(b, n // tm),
        in_specs=[row(), row(), pl.BlockSpec((1, 8, d), lambda i, j: (i, 0, 0)),
                  _const_spec(wgu.shape), _const_spec(wd.shape)],
        out_specs=row(),
        out_shape=jax.ShapeDtypeStruct((b, n, d), F32),
        compiler_params=_params("parallel", "arbitrary"),
        name="ffn",
    )(hx, x, modv, wgu, wd)


def _moe_kernel(hx_ref, x_ref, gates_ref, mod_ref, fg_ref, wgu_ref, wd_ref, o_ref, acc_ref):
    e = pl.program_id(2)

    @pl.when(e == 0)
    def _():
        acc_ref[...] = jnp.zeros_like(acc_ref)

    lane = lax.broadcasted_iota(jnp.int32, gates_ref.shape[1:], 1)
    gate = jnp.sum(jnp.where(lane == e, gates_ref[0], 0.0), axis=-1, keepdims=True)
    acc_ref[...] += gate * _swiglu_tile(hx_ref[0], wgu_ref.at[0], wd_ref.at[0])

    @pl.when(e == N_EXPERTS - 1)
    def _():
        xn = x_ref[0] + mod_ref[0, 5:6, :] * acc_ref[...]
        o_ref[0] = _rms(xn) * fg_ref[...]


def _moe_final(hx, x, gates, modv, final_g, wgu, wd):
    b, n, d = x.shape
    tm = min(ROW_TILE, n)
    row = lambda w: pl.BlockSpec((1, tm, w), lambda i, j, e: (i, j, 0))
    return pl.pallas_call(
        _moe_kernel,
        grid=(b, n // tm, N_EXPERTS),
        in_specs=[row(d), row(d), row(128),
                  pl.BlockSpec((1, 8, d), lambda i, j, e: (i, 0, 0)),
                  _const_spec((1, d)),
                  pl.BlockSpec((1,) + wgu.shape[1:], lambda i, j, e: (e, 0, 0)),
                  pl.BlockSpec((1,) + wd.shape[1:], lambda i, j, e: (e, 0, 0))],
        out_specs=row(d),
        out_shape=jax.ShapeDtypeStruct((b, n, d), F32),
        scratch_shapes=[pltpu.VMEM((tm, d), F32)],
        compiler_params=_params("parallel", "arbitrary", "arbitrary"),
        name="moe",
    )(hx, x, gates, modv, final_g.reshape(1, d), wgu, wd)


def _in_weights(w):
    wm = jnp.concatenate([w[:, O_Q:O_K], w[:, O_V:O_G], w[:, O_F:P_IN]], axis=1).astype(BF16)
    return wm, w[:, O_K:O_V].T.astype(BF16), w[:, O_G:O_F].T.astype(BF16)


def _zero_state(b):
    return (jnp.zeros((b, M_H, M_DH, 2 * M_DH), F32), jnp.full((b, M_H, 8, 128), -1e30, F32))


def _mod_rows(m):
    r = m.shape[0]
    m = m.reshape(r, 6, D_MODEL)
    return jnp.concatenate([m, jnp.zeros((r, 2, D_MODEL), F32)], axis=1)


def kernel(x, c, ctx, c_ctx, norm1_g, norm2_g, w_mod, b_mod, w_in, b_gate, mlstm_norm_g, conv_dw, conv_db, conv_norm_g, conv_norm_b, w_out, ffn_w_gate_up, ffn_w_down, moe_w_router, moe_b_router, moe_w_gate_up, moe_w_down, final_norm_g):
    b = x.shape[0]
    depth = w_in.shape[0]
    assert depth == 2 and b <= 7
    cond8 = jnp.concatenate([c, c_ctx[None, :], jnp.zeros((7 - b, D_MODEL), F32)], axis=0)
    xc = ctx
    for layer in range(depth):
        last = layer == depth - 1
        mods = _modulation(cond8, w_mod[layer], b_mod[layer])
        mod_lat = _mod_rows(mods[:b])
        mod_ctx = jnp.broadcast_to(_mod_rows(mods[b:b + 1]), (b, 8, D_MODEL))
        wm, wkt, wgt = _in_weights(w_in[layer])
        bg = b_gate[layer].reshape(16, 1)
        w_o = w_out[layer].astype(BF16)

        qvo_c, kt_c, zf_c, cc_c, gt_c = _in_proj(xc, mod_ctx, norm1_g[layer], wm, wkt, wgt, bg)
        qvo_l, kt_l, zf_l, cc_l, gt_l = _in_proj(x, mod_lat, norm1_g[layer], wm, wkt, wgt, bg)

        hf_c, hb_c, cf, cb, mf, mb = _mlstm(qvo_c, kt_c, gt_c, _zero_state(b), _zero_state(b))
        hf_l, hb_l, _, _, _, _ = _mlstm(qvo_l, kt_l, gt_l, (cf, mf), (cb, mb))

        conv_args = (conv_dw[layer], conv_db[layer], conv_norm_g[layer], conv_norm_b[layer])
        yf_l = _fourier(zf_l)
        yc_l = _conv(cc_l, *conv_args)
        if not last:
            x, hx = _out_proj(hf_l, hb_l, qvo_l, yf_l, yc_l, x, mod_lat, mlstm_norm_g[layer],
                              norm2_g[layer], w_o)
            yf_c = _fourier(zf_c)
            yc_c = _conv(cc_c, *conv_args)
            xc, hc = _out_proj(hf_c, hb_c, qvo_c, yf_c, yc_c, xc, mod_ctx, mlstm_norm_g[layer],
                               norm2_g[layer], w_o)
            wgu = ffn_w_gate_up[layer // 2].astype(BF16)
            wd = ffn_w_down[layer // 2].astype(BF16)
            x = _ffn(hx, x, mod_lat, wgu, wd)
            xc = _ffn(hc, xc, mod_ctx, wgu, wd)
        else:
            idx = layer // 2
            wr = jnp.concatenate([moe_w_router[idx], jnp.zeros((D_MODEL, 128 - N_EXPERTS), F32)], axis=1)
            br = jnp.concatenate([moe_b_router[idx], jnp.zeros((128 - N_EXPERTS,), F32)]).reshape(1, 128)
            x, hx, gates = _out_proj(hf_l, hb_l, qvo_l, yf_l, yc_l, x, mod_lat, mlstm_norm_g[layer],
                                     norm2_g[layer], w_o, router=(wr, br))
            x = _moe_final(hx, x, gates, mod_lat, final_norm_g,
                           moe_w_gate_up[idx].astype(BF16), moe_w_down[idx].astype(BF16))
    return x
```

```python
import functools
import math

import numpy as np
import jax
import jax.numpy as jnp
from jax import lax
from jax.experimental import pallas as pl
from jax.experimental.pallas import tpu as pltpu
from jax.experimental.pallas import tpu_sc as plsc

F32 = jnp.float32
BF16 = jnp.bfloat16

D_MODEL = 1024
M_W = 512
M_H = 4
M_DH = 128
F_W = 256
F_G = 4
C_W = 256
C_G = 4
CONV_K = 31
FFN_HIDDEN = 2816
N_EXPERTS = 8
EPS = 1e-6

O_Q = 0
O_K = O_Q + M_W
O_V = O_K + M_W
O_O = O_V + M_W
O_G = O_O + M_W
O_F = O_G + 4 * M_H
O_CU = O_F + F_W
O_CG = O_CU + C_W
P_IN = O_CG + C_W

LC = 256
DFT_N1 = 128
ROW_TILE = 512
HID_TILE = 256
CONV_TILE = 256
CONV_PAD = 16
GROUP_TILE = 512
SC_CORES = 2
SC_SUBCORES = 16
SC_WORKERS = SC_CORES * SC_SUBCORES
SC_CHUNK = 128
VMEM_LIMIT = 56 * 1024 * 1024


def _params(*sem):
    return pltpu.CompilerParams(dimension_semantics=sem, vmem_limit_bytes=VMEM_LIMIT)


def _const_spec(shape):
    zeros = (0,) * len(shape)
    return pl.BlockSpec(shape, lambda *_: zeros)


def _dot(a, b):
    return jnp.dot(a, b, preferred_element_type=F32)


def _dot_nt(a, b):
    return lax.dot_general(a, b, (((1,), (1,)), ((), ())), preferred_element_type=F32)


def _split2(a):
    hi = a.astype(BF16)
    lo = (a - hi.astype(F32)).astype(BF16)
    return hi, lo


def _split3(a):
    x1 = a.astype(BF16)
    r1 = a - x1.astype(F32)
    x2 = r1.astype(BF16)
    x3 = (r1 - x2.astype(F32)).astype(BF16)
    return x1, x2, x3


def _dot_precise(a, b):
    ah, al = _split2(a)
    bh, bl = _split2(b)
    return _dot(ah, bh) + _dot(al, bh) + _dot(ah, bl)


def _silu(x):
    return x * jax.nn.sigmoid(x)


def _log_sigmoid(x):
    return -(jnp.maximum(-x, 0.0) + jnp.log1p(jnp.exp(-jnp.abs(x))))


def _rms(x):
    return x * lax.rsqrt(jnp.mean(x * x, axis=-1, keepdims=True) + EPS)


def _mod_kernel(c_ref, w_ref, b_ref, o_ref):
    o_ref[...] = _dot_precise(_silu(c_ref[...]), w_ref[...]) + b_ref[...]


def _modulation(cond8, w_mod, b_mod):
    d = cond8.shape[1]
    n_out = w_mod.shape[1]
    tn = 1536
    return pl.pallas_call(
        _mod_kernel,
        grid=(n_out // tn,),
        in_specs=[_const_spec((8, d)),
                  pl.BlockSpec((d, tn), lambda j: (0, j)),
                  pl.BlockSpec((1, tn), lambda j: (0, j))],
        out_specs=pl.BlockSpec((8, tn), lambda j: (0, j)),
        out_shape=jax.ShapeDtypeStruct((8, n_out), F32),
        compiler_params=_params("arbitrary"),
        name="modulation",
    )(cond8, w_mod, b_mod.reshape(1, n_out))


def _in_kernel(x_ref, mod_ref, g_ref, wm_ref, wkt_ref, wgt_ref, bg_ref,
               qvo_ref, kt_ref, zf_ref, cc_ref, gt_ref):
    tm = x_ref.shape[1]
    h = _rms(x_ref[0]) * g_ref[...]
    h = h * (1.0 + mod_ref[0, 1:2, :]) + mod_ref[0, 0:1, :]
    hb = h.astype(BF16)
    p = _dot(hb, wm_ref[...])
    qvo_ref[0] = p[:, :3 * M_W].astype(BF16)
    zf_ref[0] = p[:, 3 * M_W:3 * M_W + F_W].astype(BF16)
    cc_ref[0] = p[:, 3 * M_W + F_W:].astype(BF16)
    kt_ref[0] = (_dot_nt(wkt_ref[...], hb) * (M_DH ** -0.5)).astype(BF16)
    gt = _dot_nt(wgt_ref[...], hb) + bg_ref[...]
    for j in range(tm // LC):
        gt_ref[0, j] = gt[:, j * LC:(j + 1) * LC]


def _in_proj(x, modv, norm_g, wm, wkt, wgt, bg):
    b, n, d = x.shape
    tm = min(ROW_TILE, n)
    nm = wm.shape[1]
    out_shape = (
        jax.ShapeDtypeStruct((b, n, 3 * M_W), BF16),
        jax.ShapeDtypeStruct((b, M_W, n), BF16),
        jax.ShapeDtypeStruct((b, n, F_W), BF16),
        jax.ShapeDtypeStruct((b, n, 2 * C_W), BF16),
        jax.ShapeDtypeStruct((b, n // LC, 16, LC), F32),
    )
    return pl.pallas_call(
        _in_kernel,
        grid=(b, n // tm),
        in_specs=[pl.BlockSpec((1, tm, d), lambda i, j: (i, j, 0)),
                  pl.BlockSpec((1, 8, d), lambda i, j: (i, 0, 0)),
                  _const_spec((1, d)),
                  _const_spec((d, nm)),
                  _const_spec((M_W, d)),
                  _const_spec((16, d)),
                  _const_spec((16, 1))],
        out_specs=(pl.BlockSpec((1, tm, 3 * M_W), lambda i, j: (i, j, 0)),
                   pl.BlockSpec((1, M_W, tm), lambda i, j: (i, 0, j)),
                   pl.BlockSpec((1, tm, F_W), lambda i, j: (i, j, 0)),
                   pl.BlockSpec((1, tm, 2 * C_W), lambda i, j: (i, j, 0)),
                   pl.BlockSpec((1, tm // LC, 16, LC), lambda i, j: (i, j, 0, 0))),
        out_shape=out_shape,
        compiler_params=_params("parallel", "arbitrary"),
        name="in_proj",
    )(x, modv, norm_g.reshape(1, d), wm, wkt, wgt, bg)


def _mlstm_kernel(qf_ref, vf_ref, ktf_ref, gtf_ref, qb_ref, vb_ref, ktb_ref, gtb_ref,
                  cf0_ref, cb0_ref, mf0_ref, mb0_ref, u3_ref, lo3_ref, ut3_ref, lot3_ref,
                  hf_ref, hb_ref, cf_ref, cb_ref, mf_ref, mb_ref):
    @pl.when(pl.program_id(1) == 0)
    def _():
        cf_ref[...] = cf0_ref[...]
        cb_ref[...] = cb0_ref[...]
        mf_ref[...] = mf0_ref[...]
        mb_ref[...] = mb0_ref[...]

    gf = gtf_ref[0, 0]
    gb = gtb_ref[0, 0]
    r16 = lax.broadcasted_iota(jnp.int32, (16, LC), 0)
    lf = jnp.where((r16 >= 4) & (r16 < 8), _log_sigmoid(gf),
                   jnp.where(r16 >= 12, _log_sigmoid(gb), 0.0))
    lf = jnp.concatenate([lf, jnp.zeros((128 - 16, LC), F32)], axis=0)
    xs = jnp.concatenate(_split3(lf), axis=1)
    rows_f = _dot(xs[0:16], u3_ref[...])
    rows_b = _dot(xs[0:16], lo3_ref[...])
    cols_f = _dot_nt(ut3_ref[...], xs)
    cols_b = _dot_nt(lot3_ref[...], xs)

    t_idx = lax.broadcasted_iota(jnp.int32, (LC, LC), 0)
    s_idx = lax.broadcasted_iota(jnp.int32, (LC, LC), 1)
    ones_col = (lax.broadcasted_iota(jnp.int32, (LC, M_DH), 1) == 0).astype(BF16)

    dirs = (
        (qf_ref, vf_ref, ktf_ref, gf, rows_f, cols_f, s_idx <= t_idx, cf_ref, mf_ref, hf_ref, 0, 4, LC - 1),
        (qb_ref, vb_ref, ktb_ref, gb, rows_b, cols_b, s_idx >= t_idx, cb_ref, mb_ref, hb_ref, 8, 12, 0),
    )
    for q_ref, v_ref, kt_ref, g, rows, cols, mask, c_ref, m_ref, h_ref, i_off, f_off, last in dirs:
        for h in range(M_H):
            hs = slice(h * M_DH, (h + 1) * M_DH)
            q = q_ref[0, :, hs]
            kt = kt_ref[0, hs, :]
            vaug = jnp.concatenate([v_ref[0, :, hs], ones_col], axis=1)
            i_row = g[i_off + h:i_off + h + 1, :]
            b_row = rows[f_off + h:f_off + h + 1, :]
            b_col = cols[:, f_off + h:f_off + h + 1]
            b_last = b_row[:, last:last + 1]
            c0 = c_ref[0, h]
            m0 = m_ref[0, h, 0:1, 0:1]

            dmat = jnp.where(mask, b_col + (i_row - b_row), -jnp.inf)
            m_inter = b_col + m0
            m_t = jnp.maximum(m_inter, jnp.max(dmat, axis=-1, keepdims=True))
            s = (_dot(q, kt) * jnp.exp(dmat - m_t)).astype(BF16)
            inter = jnp.exp(m_inter - m_t)
            na = _dot(s, vaug) + inter * _dot(q, c0.astype(BF16))
            den = na[:, M_DH:M_DH + 1]
            h_ref[0, :, hs] = na[:, :M_DH] / jnp.maximum(jnp.abs(den), jnp.exp(-m_t))

            gs = b_last - b_row + i_row
            m_new = jnp.maximum(b_last + m0, jnp.max(gs, axis=-1, keepdims=True))
            kw = (kt.astype(F32) * jnp.exp(gs - m_new)).astype(BF16)
            c_ref[0, h] = jnp.exp(b_last + m0 - m_new) * c0 + _dot(kw, vaug)
            m_ref[0, h] = jnp.broadcast_to(m_new, (8, 128))


def _tri_constants():
    s = np.arange(LC)[:, None]
    t = np.arange(LC)[None, :]
    u = (s <= t).astype(np.float32)
    lo = (s >= t).astype(np.float32)
    u3 = np.concatenate([u, u, u], axis=0)
    lo3 = np.concatenate([lo, lo, lo], axis=0)
    return (jnp.asarray(u3, BF16), jnp.asarray(lo3, BF16),
            jnp.asarray(u3.T.copy(), BF16), jnp.asarray(lo3.T.copy(), BF16))


def _mlstm(qvo, kt, gt, state_f, state_b):
    b, n, _ = qvo.shape
    nc = n // LC
    cf0, mf0 = state_f
    cb0, mb0 = state_b
    consts = _tri_constants()
    fwd = lambda i, c: (i, c, 0)
    bwd = lambda i, c: (i, nc - 1 - c, 0)
    st_c = pl.BlockSpec((1, M_H, M_DH, 2 * M_DH), lambda i, c: (i, 0, 0, 0))
    st_m = pl.BlockSpec((1, M_H, 8, 128), lambda i, c: (i, 0, 0, 0))
    h_sds = jax.ShapeDtypeStruct((b, n, M_W), F32)
    return pl.pallas_call(
        _mlstm_kernel,
        grid=(b, nc),
        in_specs=[pl.BlockSpec((1, LC, M_W), fwd),
                  pl.BlockSpec((1, LC, M_W), lambda i, c: (i, c, 1)),
                  pl.BlockSpec((1, M_W, LC), lambda i, c: (i, 0, c)),
                  pl.BlockSpec((1, 1, 16, LC), lambda i, c: (i, c, 0, 0)),
                  pl.BlockSpec((1, LC, M_W), bwd),
                  pl.BlockSpec((1, LC, M_W), lambda i, c: (i, nc - 1 - c, 1)),
                  pl.BlockSpec((1, M_W, LC), lambda i, c: (i, 0, nc - 1 - c)),
                  pl.BlockSpec((1, 1, 16, LC), lambda i, c: (i, nc - 1 - c, 0, 0)),
                  st_c, st_c, st_m, st_m,
                  _const_spec((3 * LC, LC)), _const_spec((3 * LC, LC)),
                  _const_spec((LC, 3 * LC)), _const_spec((LC, 3 * LC))],
        out_specs=(pl.BlockSpec((1, LC, M_W), fwd), pl.BlockSpec((1, LC, M_W), bwd),
                   st_c, st_c, st_m, st_m),
        out_shape=(h_sds, h_sds,
                   jax.ShapeDtypeStruct(cf0.shape, F32), jax.ShapeDtypeStruct(cb0.shape, F32),
                   jax.ShapeDtypeStruct(mf0.shape, F32), jax.ShapeDtypeStruct(mb0.shape, F32)),
        compiler_params=_params("parallel", "arbitrary"),
        name="mlstm",
    )(qvo, qvo, kt, gt, qvo, qvo, kt, gt, cf0, cb0, mf0, mb0, *consts)


def _channel_dft_matrix():
    gw = F_W // F_G
    j = np.arange(gw)[:, None]
    l = np.arange(gw)[None, :]
    ang = 2.0 * np.pi * j * l / gw
    eye = np.eye(F_G)
    bc = np.kron(eye, np.cos(ang)) / math.sqrt(gw)
    bs = np.kron(eye, np.sin(ang)) / math.sqrt(gw)
    return jnp.asarray(np.concatenate([bc, -bs], axis=1), BF16)


def _dft_cos_sin(n):
    k = np.arange(n)[:, None]
    m = np.arange(n)[None, :]
    ang = 2.0 * np.pi * ((k * m) % n) / n
    return np.cos(ang) / math.sqrt(n), np.sin(ang) / math.sqrt(n)


def _fourier1_kernel(z_ref, bcs_ref, f1_ref, ct_ref, st_ref, o_ref):
    n2 = o_ref.shape[2]

    def body(j, carry):
        off = pl.multiple_of(j * F_W, F_W)
        ab = _dot(z_ref[0, :, pl.ds(off, F_W)], bcs_ref[...])
        u = jnp.concatenate([ab[:, :F_W], ab[:, F_W:]], axis=0).astype(BF16)
        t = _dot(f1_ref[...], u)
        tre, tim = t[:DFT_N1], t[DFT_N1:]
        ct = jnp.concatenate([ct_ref[j], ct_ref[j]], axis=1)
        st = jnp.concatenate([st_ref[j], st_ref[j]], axis=1)
        o_ref[0, 0, j] = (tre * ct + tim * st).astype(BF16)
        o_ref[0, 1, j] = (tim * ct - tre * st).astype(BF16)
        return carry

    lax.fori_loop(0, n2, body, 0)


def _fourier2_kernel(t_ref, f2_ref, o_ref):
    o_ref[0] = _dot(f2_ref[...], t_ref[0]).astype(o_ref.dtype)


def _fourier_direct_kernel(z_ref, bcs_ref, f_ref, o_ref):
    ab = _dot(z_ref[0], bcs_ref[...])
    u = jnp.concatenate([ab[:, :F_W], ab[:, F_W:]], axis=0).astype(BF16)
    o_ref[0] = _dot(f_ref[...], u).astype(o_ref.dtype)


def _fourier(zf):
    b, n, _ = zf.shape
    bcs = _channel_dft_matrix()
    if n <= 512:
        c, s = _dft_cos_sin(n)
        fmat = jnp.asarray(np.concatenate([c, s], axis=1), BF16)
        return pl.pallas_call(
            _fourier_direct_kernel,
            grid=(b,),
            in_specs=[pl.BlockSpec((1, n, F_W), lambda i: (i, 0, 0)),
                      _const_spec((F_W, 2 * F_W)), _const_spec((n, 2 * n))],
            out_specs=pl.BlockSpec((1, n, F_W), lambda i: (i, 0, 0)),
            out_shape=jax.ShapeDtypeStruct((b, n, F_W), BF16),
            compiler_params=_params("parallel"),
            name="fourier_direct",
        )(zf, bcs, fmat)

    n1 = DFT_N1
    n2 = n // n1
    c1, s1 = _dft_cos_sin(n1)
    f1 = jnp.asarray(np.block([[c1, s1], [-s1, c1]]), BF16)
    k1 = np.arange(n1)[None, :]
    m2 = np.arange(n2)[:, None]
    ang = 2.0 * np.pi * ((m2 * k1) % n) / n
    ct = jnp.asarray(np.broadcast_to(np.cos(ang)[:, :, None], (n2, n1, 128)), F32)
    st = jnp.asarray(np.broadcast_to(np.sin(ang)[:, :, None], (n2, n1, 128)), F32)
    c2, s2 = _dft_cos_sin(n2)
    f2 = jnp.asarray(np.concatenate([c2, s2], axis=1), BF16)

    t = pl.pallas_call(
        _fourier1_kernel,
        grid=(b,),
        in_specs=[pl.BlockSpec((1, n1, n2 * F_W), lambda i: (i, 0, 0)),
                  _const_spec((F_W, 2 * F_W)), _const_spec((2 * n1, 2 * n1)),
                  _const_spec((n2, n1, 128)), _const_spec((n2, n1, 128))],
        out_specs=pl.BlockSpec((1, 2, n2, n1, F_W), lambda i: (i, 0, 0, 0, 0)),
        out_shape=jax.ShapeDtypeStruct((b, 2, n2, n1, F_W), BF16),
        compiler_params=_params("parallel"),
        name="fourier_stage1",
    )(zf.reshape(b, n1, n2 * F_W), bcs, f1, ct, st)
    width = n1 * F_W
    tw = min(4096, width)
    y = pl.pallas_call(
        _fourier2_kernel,
        grid=(b, width // tw),
        in_specs=[pl.BlockSpec((1, 2 * n2, tw), lambda i, j: (i, 0, j)),
                  _const_spec((n2, 2 * n2))],
        out_specs=pl.BlockSpec((1, n2, tw), lambda i, j: (i, 0, j)),
        out_shape=jax.ShapeDtypeStruct((b, n2, width), BF16),
        compiler_params=_params("parallel", "arbitrary"),
        name="fourier_stage2",
    )(t.reshape(b, 2 * n2, width), f2)
    return y.reshape(b, n, F_W)


def _group_mean(y, gm_ref):
    hi, lo = _split2(y)
    return _dot(hi, gm_ref[...]) + _dot(lo, gm_ref[...])


def _conv_kernel(cc_ref, dw_ref, db_ref, lg_ref, lb_ref, gm_ref, o_ref, a_ref):
    n = cc_ref.shape[1]
    rt = min(CONV_TILE, n)
    a_ref[0:CONV_PAD, :] = jnp.zeros((CONV_PAD, C_W), F32)
    a_ref[n + CONV_PAD:n + 2 * CONV_PAD, :] = jnp.zeros((CONV_PAD, C_W), F32)

    def glu(i, carry):
        r = pl.multiple_of(i * rt, rt)
        blk = cc_ref[0, pl.ds(r, rt), :].astype(F32)
        a_ref[pl.ds(r + CONV_PAD, rt), :] = blk[:, :C_W] * jax.nn.sigmoid(blk[:, C_W:])
        return carry

    lax.fori_loop(0, n // rt, glu, 0)

    def conv(i, carry):
        r = pl.multiple_of(i * rt, rt)
        win = a_ref[pl.ds(r, rt + 2 * CONV_PAD), :]
        span = rt + 2 * CONV_PAD - 8
        phases = [win[s:s + span] for s in range(8)]
        acc = jnp.broadcast_to(db_ref[...], (rt, C_W))
        for j in range(CONV_K):
            off = CONV_PAD - CONV_K // 2 + j
            base = 8 * (off // 8)
            acc = acc + phases[off % 8][base:base + rt] * dw_ref[j:j + 1, :]
        d = acc - _group_mean(acc, gm_ref)
        var = _group_mean(d * d, gm_ref)
        yn = d * lax.rsqrt(var + EPS) * lg_ref[...] + lb_ref[...]
        o_ref[0, pl.ds(r, rt), :] = _silu(yn).astype(o_ref.dtype)
        return carry

    lax.fori_loop(0, n // rt, conv, 0)


def _conv(cc, dw, db, ln_g, ln_b):
    b, n, _ = cc.shape
    gw = C_W // C_G
    gm = jnp.asarray(np.kron(np.eye(C_G), np.full((gw, gw), 1.0 / gw)), BF16)
    dw32 = jnp.concatenate([dw, jnp.zeros((32 - CONV_K, C_W), F32)], axis=0)
    return pl.pallas_call(
        _conv_kernel,
        grid=(b,),
        in_specs=[pl.BlockSpec((1, n, 2 * C_W), lambda i: (i, 0, 0)),
                  _const_spec((32, C_W)), _const_spec((1, C_W)), _const_spec((1, C_W)),
                  _const_spec((1, C_W)), _const_spec((C_W, C_W))],
        out_specs=pl.BlockSpec((1, n, C_W), lambda i: (i, 0, 0)),
        out_shape=jax.ShapeDtypeStruct((b, n, C_W), BF16),
        scratch_shapes=[pltpu.VMEM((n + 2 * CONV_PAD, C_W), F32)],
        compiler_params=_params("parallel"),
        name="conv_module",
    )(cc, dw32, db.reshape(1, C_W), ln_g.reshape(1, C_W), ln_b.reshape(1, C_W), gm)


def _top2_route(logits):
    lane = lax.broadcasted_iota(jnp.int32, logits.shape, 1)
    lg = jnp.where(lane < N_EXPERTS, logits, -jnp.inf)
    v1 = jnp.max(lg, axis=-1, keepdims=True)
    i1 = jnp.min(jnp.where(lg == v1, lane, 128), axis=-1, keepdims=True)
    lg2 = jnp.where(lane == i1, -jnp.inf, lg)
    v2 = jnp.max(lg2, axis=-1, keepdims=True)
    i2 = jnp.min(jnp.where(lg2 == v2, lane, 128), axis=-1, keepdims=True)
    e2 = jnp.exp(v2 - v1)
    w1 = 1.0 / (1.0 + e2)
    w2 = e2 / (1.0 + e2)
    return jnp.where(lane == 0, i1.astype(F32),
                     jnp.where(lane == 1, i2.astype(F32),
                               jnp.where(lane == 2, w1, jnp.where(lane == 3, w2, 0.0))))


def _pack_halves(y):
    w = y.shape[1] // 2
    lo = pltpu.bitcast(y[:, :w].astype(BF16).astype(F32), jnp.uint32)
    hi = pltpu.bitcast(y[:, w:].astype(BF16).astype(F32), jnp.uint32)
    return (hi & jnp.uint32(0xFFFF0000)) | (lo >> 16)


def _unpack_halves(p):
    lo = pltpu.bitcast(p << 16, F32)
    hi = pltpu.bitcast(p & jnp.uint32(0xFFFF0000), F32)
    return jnp.concatenate([lo, hi], axis=1)


def _out_kernel(hf_ref, hb_ref, o_ref, yf_ref, yc_ref, x_ref, mod_ref, hg_ref, g2_ref, wo_ref,
                *rest, with_router):
    if with_router:
        wr_ref, br_ref, xo_ref, hx_ref, route_ref = rest
    else:
        xo_ref, hx_ref = rest
    hsum = hf_ref[0] + hb_ref[0]
    heads = [_rms(hsum[:, h * M_DH:(h + 1) * M_DH]) for h in range(M_H)]
    ym = jax.nn.sigmoid(o_ref[0].astype(F32)) * (jnp.concatenate(heads, axis=1) * hg_ref[...])
    proj = (_dot(ym.astype(BF16), wo_ref[0:M_W, :])
            + _dot(yf_ref[0], wo_ref[M_W:M_W + F_W, :])
            + _dot(yc_ref[0], wo_ref[M_W + F_W:, :]))
    xn = x_ref[0] + mod_ref[0, 2:3, :] * proj
    xo_ref[0] = xn
    hx = (_rms(xn) * g2_ref[...]) * (1.0 + mod_ref[0, 4:5, :]) + mod_ref[0, 3:4, :]
    if with_router:
        hx_ref[0] = _pack_halves(hx)
        route_ref[0] = _top2_route(_dot_precise(hx, wr_ref[...]) + br_ref[...])
    else:
        hx_ref[0] = hx.astype(BF16)


def _out_proj(hf, hb, qvo, yf, yc, x, modv, head_g, norm2_g, w_out, router=None):
    b, n, d = x.shape
    tm = min(ROW_TILE, n)
    row = lambda w: pl.BlockSpec((1, tm, w), lambda i, j: (i, j, 0))
    in_specs = [row(M_W), row(M_W),
                pl.BlockSpec((1, tm, M_W), lambda i, j: (i, j, 2)),
                row(F_W), row(C_W), row(d),
                pl.BlockSpec((1, 8, d), lambda i, j: (i, 0, 0)),
                _const_spec((1, M_W)), _const_spec((1, d)), _const_spec((d, d))]
    args = [hf, hb, qvo, yf, yc, x, modv, head_g.reshape(1, M_W), norm2_g.reshape(1, d), w_out]
    out_specs = [row(d), row(d)]
    out_shape = [jax.ShapeDtypeStruct((b, n, d), F32), jax.ShapeDtypeStruct((b, n, d), BF16)]
    if router is not None:
        out_specs[1] = row(d // 2)
        out_shape[1] = jax.ShapeDtypeStruct((b, n, d // 2), jnp.uint32)
        in_specs += [_const_spec((d, 128)), _const_spec((1, 128))]
        args += list(router)
        out_specs.append(row(128))
        out_shape.append(jax.ShapeDtypeStruct((b, n, 128), F32))
    return pl.pallas_call(
        functools.partial(_out_kernel, with_router=router is not None),
        grid=(b, n // tm),
        in_specs=in_specs,
        out_specs=tuple(out_specs),
        out_shape=tuple(out_shape),
        compiler_params=_params("parallel", "arbitrary"),
        name="out_proj",
    )(*args)


def _swiglu_tile(hx, wgu_ref, wd_ref):
    acc = jnp.zeros((hx.shape[0], D_MODEL), F32)
    for j in range(FFN_HIDDEN // HID_TILE):
        cs = slice(j * HID_TILE, (j + 1) * HID_TILE)
        us = slice(FFN_HIDDEN + j * HID_TILE, FFN_HIDDEN + (j + 1) * HID_TILE)
        a = _silu(_dot(hx, wgu_ref[:, cs])) * _dot(hx, wgu_ref[:, us])
        acc = acc + _dot(a.astype(BF16), wd_ref[cs, :])
    return acc


def _ffn_kernel(hx_ref, x_ref, mod_ref, wgu_ref, wd_ref, o_ref):
    o_ref[0] = x_ref[0] + mod_ref[0, 5:6, :] * _swiglu_tile(hx_ref[0], wgu_ref, wd_ref)


def _ffn(hx, x, modv, wgu, wd):
    b, n, d = x.shape
    tm = min(ROW_TILE, n)
    row = lambda: pl.BlockSpec((1, tm, d), lambda i, j: (i, j, 0))
    return pl.pallas_call(
        _ffn_kernel,
        grid=(b, n // tm),
        in_specs=[row(), row(), pl.BlockSpec((1, 8, d), lambda i, j: (i, 0, 0)),
                  _const_spec(wgu.shape), _const_spec(wd.shape)],
        out_specs=row(),
        out_shape=jax.ShapeDtypeStruct((b, n, d), F32),
        compiler_params=_params("parallel", "arbitrary"),
        name="ffn",
    )(hx, x, modv, wgu, wd)


def _rank_kernel(route_ref, tri_ref, pos_ref, cnt_ref, carry_ref):
    @pl.when(pl.program_id(0) == 0)
    def _():
        carry_ref[...] = jnp.zeros_like(carry_ref)

    r = route_ref[...]
    lane = lax.broadcasted_iota(jnp.int32, r.shape, 1).astype(F32)
    e1, e2 = r[:, 0:1], r[:, 1:2]
    hit = jnp.where((lane == e1) | (lane == e2), 1.0, 0.0)
    before = _dot(tri_ref[...], hit.astype(BF16)) + carry_ref[0:1, :]
    rank1 = jnp.sum(jnp.where(lane == e1, before, 0.0), axis=-1, keepdims=True)
    rank2 = jnp.sum(jnp.where(lane == e2, before, 0.0), axis=-1, keepdims=True)
    pos_ref[...] = jnp.where(lane == 0.0, rank1, jnp.where(lane == 1.0, rank2, 0.0)).astype(jnp.int32)
    total = carry_ref[...] + jnp.sum(hit, axis=0, keepdims=True)
    carry_ref[...] = total
    cnt_ref[...] = total.astype(jnp.int32)


def _route_ranks(route):
    t = route.shape[0]
    tm = min(ROW_TILE, t)
    tri = jnp.asarray(np.tril(np.ones((tm, tm), np.float32), -1), BF16)
    return pl.pallas_call(
        _rank_kernel,
        grid=(t // tm,),
        in_specs=[pl.BlockSpec((tm, 128), lambda i: (i, 0)), _const_spec((tm, tm))],
        out_specs=(pl.BlockSpec((tm, 128), lambda i: (i, 0)), _const_spec((8, 128))),
        out_shape=(jax.ShapeDtypeStruct((t, 128), jnp.int32), jax.ShapeDtypeStruct((8, 128), jnp.int32)),
        scratch_shapes=[pltpu.VMEM((8, 128), F32)],
        compiler_params=_params("arbitrary"),
        name="route_ranks",
    )(route, tri)


def _sc_worker_rows(n_rows):
    assert n_rows % (SC_WORKERS * SC_CHUNK) == 0
    return n_rows // SC_WORKERS


def _sc_scatter_rows(x, dest, n_out):
    t, w = x.shape
    per_w = _sc_worker_rows(t)
    mesh = plsc.VectorSubcoreMesh(core_axis_name="c", subcore_axis_name="s")

    @functools.partial(
        pl.kernel, mesh=mesh,
        out_type=jax.ShapeDtypeStruct((n_out, w), x.dtype),
        scratch_types=[pltpu.VMEM((1, SC_CHUNK), jnp.int32),
                       pltpu.VMEM((SC_CHUNK, w), x.dtype),
                       pltpu.SemaphoreType.DMA],
    )
    def scatter(x_hbm, dest_hbm, out_hbm, idx_v, rows_v, sem):
        base = (lax.axis_index("s") * SC_CORES + lax.axis_index("c")) * per_w

        @pl.loop(0, per_w // SC_CHUNK)
        def _(g):
            off = base + g * SC_CHUNK
            pltpu.sync_copy(x_hbm.at[pl.ds(off, SC_CHUNK)], rows_v)
            for k in range(2):
                pltpu.sync_copy(dest_hbm.at[pl.ds(k, 1), pl.ds(off, SC_CHUNK)], idx_v)
                pltpu.async_copy(rows_v, out_hbm.at[idx_v.at[0]], sem).wait()

    return scatter(x, dest)


def _sc_gather_rows(table, idx):
    r = idx.shape[0]
    w = table.shape[1]
    per_w = _sc_worker_rows(r)
    mesh = plsc.VectorSubcoreMesh(core_axis_name="c", subcore_axis_name="s")

    @functools.partial(
        pl.kernel, mesh=mesh,
        out_type=jax.ShapeDtypeStruct((r, w), table.dtype),
        scratch_types=[pltpu.VMEM((1, SC_CHUNK), jnp.int32),
                       pltpu.VMEM((SC_CHUNK, w), table.dtype),
                       pltpu.SemaphoreType.DMA],
    )
    def gather(table_hbm, idx_hbm, out_hbm, idx_v, rows_v, sem):
        base = (lax.axis_index("s") * SC_CORES + lax.axis_index("c")) * per_w

        @pl.loop(0, per_w // SC_CHUNK)
        def _(g):
            off = base + g * SC_CHUNK
            pltpu.sync_copy(idx_hbm.at[pl.ds(0, 1), pl.ds(off, SC_CHUNK)], idx_v)
            pltpu.async_copy(table_hbm.at[idx_v.at[0]], rows_v, sem).wait()
            pltpu.sync_copy(rows_v, out_hbm.at[pl.ds(off, SC_CHUNK)])

    return gather(table, idx.reshape(1, r))


def _group_kernel(te_ref, nv_ref, xs_ref, wgu_ref, wd_ref, ys_ref):
    @pl.when(pl.program_id(0) < nv_ref[0])
    def _():
        hx = _unpack_halves(xs_ref[...]).astype(BF16)
        ys_ref[...] = _pack_halves(_swiglu_tile(hx, wgu_ref.at[0], wd_ref.at[0]))


def _grouped_swiglu(xs, tile_expert, n_valid, wgu, wd):
    r, wp = xs.shape
    grid_spec = pltpu.PrefetchScalarGridSpec(
        num_scalar_prefetch=2,
        grid=(r // GROUP_TILE,),
        in_specs=[pl.BlockSpec((GROUP_TILE, wp), lambda i, te, nv: (i, 0)),
                  pl.BlockSpec((1,) + wgu.shape[1:], lambda i, te, nv: (te[i], 0, 0)),
                  pl.BlockSpec((1,) + wd.shape[1:], lambda i, te, nv: (te[i], 0, 0))],
        out_specs=pl.BlockSpec((GROUP_TILE, wp), lambda i, te, nv: (i, 0)),
    )
    return pl.pallas_call(
        _group_kernel,
        grid_spec=grid_spec,
        out_shape=jax.ShapeDtypeStruct((r, wp), jnp.uint32),
        compiler_params=_params("arbitrary"),
        name="grouped_swiglu",
    )(tile_expert, n_valid, xs, wgu, wd)


def _combine_kernel(x_ref, y_ref, route_ref, mod_ref, fg_ref, o_ref):
    r = route_ref[0]
    moe = r[:, 2:3] * _unpack_halves(y_ref[0, 0]) + r[:, 3:4] * _unpack_halves(y_ref[1, 0])
    xn = x_ref[0] + mod_ref[0, 5:6, :] * moe
    o_ref[0] = _rms(xn) * fg_ref[...]


def _combine_final(x, y2, route, modv, final_g):
    b, n, d = x.shape
    tm = min(ROW_TILE, n)
    row = lambda w: pl.BlockSpec((1, tm, w), lambda i, j: (i, j, 0))
    return pl.pallas_call(
        _combine_kernel,
        grid=(b, n // tm),
        in_specs=[row(d), pl.BlockSpec((2, 1, tm, d // 2), lambda i, j: (0, i, j, 0)), row(128),
                  pl.BlockSpec((1, 8, d), lambda i, j: (i, 0, 0)), _const_spec((1, d))],
        out_specs=row(d),
        out_shape=jax.ShapeDtypeStruct((b, n, d), F32),
        compiler_params=_params("parallel", "arbitrary"),
        name="moe_combine",
    )(x, y2, route, modv, final_g.reshape(1, d))


def _moe_final(hxp, x, route, modv, final_g, wgu, wd):
    b, n, d = x.shape
    t = b * n
    pos, counts = _route_ranks(route.reshape(t, 128))
    counts = counts[0, :N_EXPERTS]
    padded = ((counts + GROUP_TILE - 1) // GROUP_TILE) * GROUP_TILE
    ends = jnp.cumsum(padded)
    starts = ends - padded
    experts = route.reshape(t, 128)[:, :2].astype(jnp.int32)
    dest = (starts[experts] + pos[:, :2]).T
    n_rows = 2 * t + N_EXPERTS * GROUP_TILE
    tile_start = jnp.arange(n_rows // GROUP_TILE, dtype=jnp.int32) * GROUP_TILE
    tile_expert = jnp.minimum(jnp.searchsorted(ends, tile_start, side="right"), N_EXPERTS - 1).astype(jnp.int32)
    n_valid = (ends[-1:] // GROUP_TILE).astype(jnp.int32)

    xs = _sc_scatter_rows(hxp.reshape(t, d // 2), dest, n_rows)
    ys = _grouped_swiglu(xs, tile_expert, n_valid, wgu, wd)
    y2 = _sc_gather_rows(ys, dest.reshape(2 * t))
    return _combine_final(x, y2.reshape(2, b, n, d // 2), route, modv, final_g)


def _in_weights(w):
    wm = jnp.concatenate([w[:, O_Q:O_K], w[:, O_V:O_G], w[:, O_F:P_IN]], axis=1).astype(BF16)
    return wm, w[:, O_K:O_V].T.astype(BF16), w[:, O_G:O_F].T.astype(BF16)


def _zero_state(b):
    return (jnp.zeros((b, M_H, M_DH, 2 * M_DH), F32), jnp.full((b, M_H, 8, 128), -1e30, F32))


def _mod_rows(m):
    r = m.shape[0]
    m = m.reshape(r, 6, D_MODEL)
    return jnp.concatenate([m, jnp.zeros((r, 2, D_MODEL), F32)], axis=1)


def kernel(x, c, ctx, c_ctx, norm1_g, norm2_g, w_mod, b_mod, w_in, b_gate, mlstm_norm_g, conv_dw, conv_db, conv_norm_g, conv_norm_b, w_out, ffn_w_gate_up, ffn_w_down, moe_w_router, moe_b_router, moe_w_gate_up, moe_w_down, final_norm_g):
    b = x.shape[0]
    depth = w_in.shape[0]
    assert depth == 2 and b <= 7
    cond8 = jnp.concatenate([c, c_ctx[None, :], jnp.zeros((7 - b, D_MODEL), F32)], axis=0)
    xc = ctx
    for layer in range(depth):
        last = layer == depth - 1
        mods = _modulation(cond8, w_mod[layer], b_mod[layer])
        mod_lat = _mod_rows(mods[:b])
        mod_ctx = jnp.broadcast_to(_mod_rows(mods[b:b + 1]), (b, 8, D_MODEL))
        wm, wkt, wgt = _in_weights(w_in[layer])
        bg = b_gate[layer].reshape(16, 1)
        w_o = w_out[layer].astype(BF16)

        qvo_c, kt_c, zf_c, cc_c, gt_c = _in_proj(xc, mod_ctx, norm1_g[layer], wm, wkt, wgt, bg)
        qvo_l, kt_l, zf_l, cc_l, gt_l = _in_proj(x, mod_lat, norm1_g[layer], wm, wkt, wgt, bg)

        hf_c, hb_c, cf, cb, mf, mb = _mlstm(qvo_c, kt_c, gt_c, _zero_state(b), _zero_state(b))
        hf_l, hb_l, _, _, _, _ = _mlstm(qvo_l, kt_l, gt_l, (cf, mf), (cb, mb))

        conv_args = (conv_dw[layer], conv_db[layer], conv_norm_g[layer], conv_norm_b[layer])
        yf_l = _fourier(zf_l)
        yc_l = _conv(cc_l, *conv_args)
        if not last:
            x, hx = _out_proj(hf_l, hb_l, qvo_l, yf_l, yc_l, x, mod_lat, mlstm_norm_g[layer],
                              norm2_g[layer], w_o)
            yf_c = _fourier(zf_c)
            yc_c = _conv(cc_c, *conv_args)
            xc, hc = _out_proj(hf_c, hb_c, qvo_c, yf_c, yc_c, xc, mod_ctx, mlstm_norm_g[layer],
                               norm2_g[layer], w_o)
            wgu = ffn_w_gate_up[layer // 2].astype(BF16)
            wd = ffn_w_down[layer // 2].astype(BF16)
            x = _ffn(hx, x, mod_lat, wgu, wd)
            xc = _ffn(hc, xc, mod_ctx, wgu, wd)
        else:
            idx = layer // 2
            wr = jnp.concatenate([moe_w_router[idx], jnp.zeros((D_MODEL, 128 - N_EXPERTS), F32)], axis=1)
            br = jnp.concatenate([moe_b_router[idx], jnp.zeros((128 - N_EXPERTS,), F32)]).reshape(1, 128)
            x, hxp, route = _out_proj(hf_l, hb_l, qvo_l, yf_l, yc_l, x, mod_lat, mlstm_norm_g[layer],
                                      norm2_g[layer], w_o, router=(wr, br))
            x = _moe_final(hxp, x, route, mod_lat, final_norm_g,
                           moe_w_gate_up[idx].astype(BF16), moe_w_down[idx].astype(BF16))
    return x
```

```python
import functools
import math

import numpy as np
import jax
import jax.numpy as jnp
from jax import lax
from jax.experimental import pallas as pl
from jax.experimental.pallas import tpu as pltpu
from jax.experimental.pallas import tpu_sc as plsc

F32 = jnp.float32
BF16 = jnp.bfloat16

D_MODEL = 1024
M_W = 512
M_H = 4
M_DH = 128
F_W = 256
F_G = 4
C_W = 256
C_G = 4
CONV_K = 31
FFN_HIDDEN = 2816
N_EXPERTS = 8
EPS = 1e-6
LOG2E = 1.4426950408889634

O_Q = 0
O_K = O_Q + M_W
O_V = O_K + M_W
O_O = O_V + M_W
O_G = O_O + M_W
O_F = O_G + 4 * M_H
O_CU = O_F + F_W
O_CG = O_CU + C_W
P_IN = O_CG + C_W

LC = 256
DFT_N1 = 128
ROW_TILE = 512
HID_TILE = 256
CONV_TILE = 256
CONV_PAD = 16
GROUP_TILE = 512
SC_CORES = 2
SC_SUBCORES = 16
SC_WORKERS = SC_CORES * SC_SUBCORES
SC_CHUNK = 128
VMEM_LIMIT = 56 * 1024 * 1024


def _params(*sem):
    return pltpu.CompilerParams(dimension_semantics=sem, vmem_limit_bytes=VMEM_LIMIT)


def _const_spec(shape):
    zeros = (0,) * len(shape)
    return pl.BlockSpec(shape, lambda *_: zeros)


def _dot(a, b):
    return jnp.dot(a, b, preferred_element_type=F32)


def _dot_nt(a, b):
    return lax.dot_general(a, b, (((1,), (1,)), ((), ())), preferred_element_type=F32)


def _split2(a):
    hi = a.astype(BF16)
    lo = (a - hi.astype(F32)).astype(BF16)
    return hi, lo


def _split3(a):
    x1 = a.astype(BF16)
    r1 = a - x1.astype(F32)
    x2 = r1.astype(BF16)
    x3 = (r1 - x2.astype(F32)).astype(BF16)
    return x1, x2, x3


def _dot_precise(a, b):
    ah, al = _split2(a)
    bh, bl = _split2(b)
    return _dot(ah, bh) + _dot(al, bh) + _dot(ah, bl)


def _silu(x):
    return x * jax.nn.sigmoid(x)


def _log_sigmoid(x):
    return -(jnp.maximum(-x, 0.0) + jnp.log1p(jnp.exp(-jnp.abs(x))))


def _rms(x):
    return x * lax.rsqrt(jnp.mean(x * x, axis=-1, keepdims=True) + EPS)


def _mod_kernel(c_ref, w_ref, b_ref, o_ref):
    o_ref[...] = _dot_precise(_silu(c_ref[...]), w_ref[...]) + b_ref[...]


def _modulation(cond8, w_mod, b_mod):
    d = cond8.shape[1]
    n_out = w_mod.shape[1]
    tn = 1536
    return pl.pallas_call(
        _mod_kernel,
        grid=(n_out // tn,),
        in_specs=[_const_spec((8, d)),
                  pl.BlockSpec((d, tn), lambda j: (0, j)),
                  pl.BlockSpec((1, tn), lambda j: (0, j))],
        out_specs=pl.BlockSpec((8, tn), lambda j: (0, j)),
        out_shape=jax.ShapeDtypeStruct((8, n_out), F32),
        compiler_params=_params("arbitrary"),
        name="modulation",
    )(cond8, w_mod, b_mod.reshape(1, n_out))


def _in_kernel(x_ref, mod_ref, g_ref, wm_ref, wkt_ref, wgt_ref, bg_ref, bgr_ref,
               qvo_ref, kt_ref, zf_ref, cc_ref, gt_ref, gc_ref):
    tm = x_ref.shape[1]
    h = _rms(x_ref[0]) * g_ref[...]
    h = h * (1.0 + mod_ref[0, 1:2, :]) + mod_ref[0, 0:1, :]
    hb = h.astype(BF16)
    p = _dot(hb, wm_ref[...])
    o_f, o_c, o_g = 3 * M_W, 3 * M_W + F_W, 3 * M_W + F_W + 2 * C_W
    qvo_ref[0] = p[:, :o_f].astype(BF16)
    zf_ref[0] = p[:, o_f:o_c].astype(BF16)
    cc_ref[0] = p[:, o_c:o_g].astype(BF16)
    kt_ref[0] = (_dot_nt(wkt_ref[...], hb) * (M_DH ** -0.5)).astype(BF16)
    gc_ref[0] = p[:, o_g:] + bgr_ref[...]
    gt = _dot_nt(wgt_ref[...], hb) + bg_ref[...]
    for j in range(tm // LC):
        gt_ref[0, j] = gt[:, j * LC:(j + 1) * LC]


def _in_proj(x, modv, norm_g, wm, wkt, wgt, bg):
    b, n, d = x.shape
    tm = min(ROW_TILE, n)
    nm = wm.shape[1]
    bg_row = jnp.concatenate([bg.reshape(1, 16), jnp.zeros((1, 128 - 16), F32)], axis=1)
    out_shape = (
        jax.ShapeDtypeStruct((b, n, 3 * M_W), BF16),
        jax.ShapeDtypeStruct((b, M_W, n), BF16),
        jax.ShapeDtypeStruct((b, n, F_W), BF16),
        jax.ShapeDtypeStruct((b, n, 2 * C_W), BF16),
        jax.ShapeDtypeStruct((b, n // LC, 16, LC), F32),
        jax.ShapeDtypeStruct((b, n, 128), F32),
    )
    return pl.pallas_call(
        _in_kernel,
        grid=(b, n // tm),
        in_specs=[pl.BlockSpec((1, tm, d), lambda i, j: (i, j, 0)),
                  pl.BlockSpec((1, 8, d), lambda i, j: (i, 0, 0)),
                  _const_spec((1, d)),
                  _const_spec((d, nm)),
                  _const_spec((M_W, d)),
                  _const_spec((16, d)),
                  _const_spec((16, 1)),
                  _const_spec((1, 128))],
        out_specs=(pl.BlockSpec((1, tm, 3 * M_W), lambda i, j: (i, j, 0)),
                   pl.BlockSpec((1, M_W, tm), lambda i, j: (i, 0, j)),
                   pl.BlockSpec((1, tm, F_W), lambda i, j: (i, j, 0)),
                   pl.BlockSpec((1, tm, 2 * C_W), lambda i, j: (i, j, 0)),
                   pl.BlockSpec((1, tm // LC, 16, LC), lambda i, j: (i, j, 0, 0)),
                   pl.BlockSpec((1, tm, 128), lambda i, j: (i, j, 0))),
        out_shape=out_shape,
        compiler_params=_params("parallel", "arbitrary"),
        name="in_proj",
    )(x, modv, norm_g.reshape(1, d), wm, wkt, wgt, bg, bg_row)


def _mlstm_kernel(qf_ref, vf_ref, ktf_ref, gtf_ref, gcf_ref, qb_ref, vb_ref, ktb_ref, gtb_ref, gcb_ref,
                  cf0_ref, cb0_ref, mf0_ref, mb0_ref, u3_ref, lo3_ref, ut3_ref, lot3_ref,
                  hf_ref, hb_ref, cf_ref, cb_ref, mf_ref, mb_ref):
    @pl.when(pl.program_id(1) == 0)
    def _():
        cf_ref[...] = cf0_ref[...]
        cb_ref[...] = cb0_ref[...]
        mf_ref[...] = mf0_ref[...]
        mb_ref[...] = mb0_ref[...]

    gf = gtf_ref[0, 0]
    gb = gtb_ref[0, 0]
    rows_f = _dot(jnp.concatenate(_split3(_log_sigmoid(gf)), axis=1), u3_ref[...])
    rows_b = _dot(jnp.concatenate(_split3(_log_sigmoid(gb)), axis=1), lo3_ref[...])
    cols_f = _dot(ut3_ref[...], jnp.concatenate(_split3(_log_sigmoid(gcf_ref[0])), axis=0))
    cols_b = _dot(lot3_ref[...], jnp.concatenate(_split3(_log_sigmoid(gcb_ref[0])), axis=0))

    t_idx = lax.broadcasted_iota(jnp.int32, (LC, LC), 0)
    s_idx = lax.broadcasted_iota(jnp.int32, (LC, LC), 1)
    ones = jnp.ones((LC, M_DH), BF16)

    dirs = (
        (qf_ref, vf_ref, ktf_ref, gf, rows_f, cols_f, s_idx <= t_idx, cf_ref, mf_ref, hf_ref, 0, 4, LC - 1),
        (qb_ref, vb_ref, ktb_ref, gb, rows_b, cols_b, s_idx >= t_idx, cb_ref, mb_ref, hb_ref, 8, 12, 0),
    )
    outs = []
    for q_ref, v_ref, kt_ref, g, rows, cols, mask, c_ref, m_ref, h_ref, i_off, f_off, last in dirs:
        for h in range(M_H):
            hs = slice(h * M_DH, (h + 1) * M_DH)
            q = q_ref[0, :, hs]
            kt = kt_ref[0, hs, :]
            vaug = jnp.concatenate([v_ref[0, :, hs], ones], axis=1)
            i_row = g[i_off + h:i_off + h + 1, :]
            b_row = rows[f_off + h:f_off + h + 1, :]
            b_col = jnp.broadcast_to(cols[:, f_off + h:f_off + h + 1], (LC, M_DH))
            b_last = b_row[:, last:last + 1]
            c0 = c_ref[0, h]
            m0 = m_ref[0, h, 0:1, 0:1]

            am = jnp.where(mask, (i_row - b_row) * LOG2E, -jnp.inf)
            g2 = jnp.maximum(m0 * LOG2E, jnp.max(am, axis=-1, keepdims=True))
            g2 = jnp.broadcast_to(g2, (LC, M_DH))
            e = jnp.exp2(am - jnp.concatenate([g2, g2], axis=1))
            s = (_dot(q, kt) * e).astype(BF16)
            qi = (q.astype(F32) * jnp.exp2(m0 * LOG2E - g2)).astype(BF16)
            na = _dot(s, vaug) + _dot(qi, c0.astype(BF16))
            floor = jnp.exp2(-(b_col * LOG2E + g2))
            h_new = (na[:, :M_DH] / jnp.maximum(jnp.abs(na[:, M_DH:]), floor)).astype(h_ref.dtype)

            gs = b_last - b_row + i_row
            m_new = jnp.maximum(b_last + m0, jnp.max(gs, axis=-1, keepdims=True))
            kw = (kt.astype(F32) * jnp.exp(gs - m_new)).astype(BF16)
            c_new = jnp.exp(b_last + m0 - m_new) * c0 + _dot(kw, vaug)
            outs.append((h_ref, c_ref, m_ref, h, hs, h_new, c_new, m_new))

    for h_ref, c_ref, m_ref, h, hs, h_new, c_new, m_new in outs:
        h_ref[0, :, hs] = h_new
        c_ref[0, h] = c_new
        m_ref[0, h] = jnp.broadcast_to(m_new, (8, 128))


def _tri_constants():
    s = np.arange(LC)[:, None]
    t = np.arange(LC)[None, :]
    u = (s <= t).astype(np.float32)
    lo = (s >= t).astype(np.float32)
    u3 = np.concatenate([u, u, u], axis=0)
    lo3 = np.concatenate([lo, lo, lo], axis=0)
    return (jnp.asarray(u3, BF16), jnp.asarray(lo3, BF16),
            jnp.asarray(u3.T.copy(), BF16), jnp.asarray(lo3.T.copy(), BF16))


def _mlstm(qvo, kt, gt, gc, state_f, state_b):
    b, n, _ = qvo.shape
    nc = n // LC
    cf0, mf0 = state_f
    cb0, mb0 = state_b
    fwd = lambda i, c: (i, c, 0)
    bwd = lambda i, c: (i, nc - 1 - c, 0)
    st_c = pl.BlockSpec((1, M_H, M_DH, 2 * M_DH), lambda i, c: (i, 0, 0, 0))
    st_m = pl.BlockSpec((1, M_H, 8, 128), lambda i, c: (i, 0, 0, 0))
    h_sds = jax.ShapeDtypeStruct((b, n, M_W), BF16)
    return pl.pallas_call(
        _mlstm_kernel,
        grid=(b, nc),
        in_specs=[pl.BlockSpec((1, LC, M_W), fwd),
                  pl.BlockSpec((1, LC, M_W), lambda i, c: (i, c, 1)),
                  pl.BlockSpec((1, M_W, LC), lambda i, c: (i, 0, c)),
                  pl.BlockSpec((1, 1, 16, LC), lambda i, c: (i, c, 0, 0)),
                  pl.BlockSpec((1, LC, 128), fwd),
                  pl.BlockSpec((1, LC, M_W), bwd),
                  pl.BlockSpec((1, LC, M_W), lambda i, c: (i, nc - 1 - c, 1)),
                  pl.BlockSpec((1, M_W, LC), lambda i, c: (i, 0, nc - 1 - c)),
                  pl.BlockSpec((1, 1, 16, LC), lambda i, c: (i, nc - 1 - c, 0, 0)),
                  pl.BlockSpec((1, LC, 128), bwd),
                  st_c, st_c, st_m, st_m,
                  _const_spec((3 * LC, LC)), _const_spec((3 * LC, LC)),
                  _const_spec((LC, 3 * LC)), _const_spec((LC, 3 * LC))],
        out_specs=(pl.BlockSpec((1, LC, M_W), fwd), pl.BlockSpec((1, LC, M_W), bwd),
                   st_c, st_c, st_m, st_m),
        out_shape=(h_sds, h_sds,
                   jax.ShapeDtypeStruct(cf0.shape, F32), jax.ShapeDtypeStruct(cb0.shape, F32),
                   jax.ShapeDtypeStruct(mf0.shape, F32), jax.ShapeDtypeStruct(mb0.shape, F32)),
        compiler_params=_params("parallel", "arbitrary"),
        name="mlstm",
    )(qvo, qvo, kt, gt, gc, qvo, qvo, kt, gt, gc, cf0, cb0, mf0, mb0, *_tri_constants())


def _channel_dft_matrix():
    gw = F_W // F_G
    j = np.arange(gw)[:, None]
    l = np.arange(gw)[None, :]
    ang = 2.0 * np.pi * j * l / gw
    eye = np.eye(F_G)
    bc = np.kron(eye, np.cos(ang)) / math.sqrt(gw)
    bs = np.kron(eye, np.sin(ang)) / math.sqrt(gw)
    return jnp.asarray(np.concatenate([bc, -bs], axis=1), BF16)


def _dft_cos_sin(n):
    k = np.arange(n)[:, None]
    m = np.arange(n)[None, :]
    ang = 2.0 * np.pi * ((k * m) % n) / n
    return np.cos(ang) / math.sqrt(n), np.sin(ang) / math.sqrt(n)


def _fourier1_kernel(z_ref, bcs_ref, f1_ref, ct_ref, st_ref, o_ref):
    n2 = o_ref.shape[2]

    def body(j, carry):
        off = pl.multiple_of(j * F_W, F_W)
        ab = _dot(z_ref[0, :, pl.ds(off, F_W)], bcs_ref[...])
        u = jnp.concatenate([ab[:, :F_W], ab[:, F_W:]], axis=0).astype(BF16)
        t = _dot(f1_ref[...], u)
        tre, tim = t[:DFT_N1], t[DFT_N1:]
        ct = jnp.concatenate([ct_ref[j], ct_ref[j]], axis=1)
        st = jnp.concatenate([st_ref[j], st_ref[j]], axis=1)
        o_ref[0, 0, j] = (tre * ct + tim * st).astype(BF16)
        o_ref[0, 1, j] = (tim * ct - tre * st).astype(BF16)
        return carry

    lax.fori_loop(0, n2, body, 0)


def _fourier2_kernel(t_ref, f2_ref, o_ref):
    o_ref[0] = _dot(f2_ref[...], t_ref[0]).astype(o_ref.dtype)


def _fourier_direct_kernel(z_ref, bcs_ref, f_ref, o_ref):
    ab = _dot(z_ref[0], bcs_ref[...])
    u = jnp.concatenate([ab[:, :F_W], ab[:, F_W:]], axis=0).astype(BF16)
    o_ref[0] = _dot(f_ref[...], u).astype(o_ref.dtype)


def _fourier(zf):
    b, n, _ = zf.shape
    bcs = _channel_dft_matrix()
    if n <= 512:
        c, s = _dft_cos_sin(n)
        fmat = jnp.asarray(np.concatenate([c, s], axis=1), BF16)
        return pl.pallas_call(
            _fourier_direct_kernel,
            grid=(b,),
            in_specs=[pl.BlockSpec((1, n, F_W), lambda i: (i, 0, 0)),
                      _const_spec((F_W, 2 * F_W)), _const_spec((n, 2 * n))],
            out_specs=pl.BlockSpec((1, n, F_W), lambda i: (i, 0, 0)),
            out_shape=jax.ShapeDtypeStruct((b, n, F_W), BF16),
            compiler_params=_params("parallel"),
            name="fourier_direct",
        )(zf, bcs, fmat)

    n1 = DFT_N1
    n2 = n // n1
    c1, s1 = _dft_cos_sin(n1)
    f1 = jnp.asarray(np.block([[c1, s1], [-s1, c1]]), BF16)
    k1 = np.arange(n1)[None, :]
    m2 = np.arange(n2)[:, None]
    ang = 2.0 * np.pi * ((m2 * k1) % n) / n
    ct = jnp.asarray(np.broadcast_to(np.cos(ang)[:, :, None], (n2, n1, 128)), F32)
    st = jnp.asarray(np.broadcast_to(np.sin(ang)[:, :, None], (n2, n1, 128)), F32)
    c2, s2 = _dft_cos_sin(n2)
    f2 = jnp.asarray(np.concatenate([c2, s2], axis=1), BF16)

    t = pl.pallas_call(
        _fourier1_kernel,
        grid=(b,),
        in_specs=[pl.BlockSpec((1, n1, n2 * F_W), lambda i: (i, 0, 0)),
                  _const_spec((F_W, 2 * F_W)), _const_spec((2 * n1, 2 * n1)),
                  _const_spec((n2, n1, 128)), _const_spec((n2, n1, 128))],
        out_specs=pl.BlockSpec((1, 2, n2, n1, F_W), lambda i: (i, 0, 0, 0, 0)),
        out_shape=jax.ShapeDtypeStruct((b, 2, n2, n1, F_W), BF16),
        compiler_params=_params("parallel"),
        name="fourier_stage1",
    )(zf.reshape(b, n1, n2 * F_W), bcs, f1, ct, st)
    width = n1 * F_W
    tw = min(4096, width)
    y = pl.pallas_call(
        _fourier2_kernel,
        grid=(b, width // tw),
        in_specs=[pl.BlockSpec((1, 2 * n2, tw), lambda i, j: (i, 0, j)),
                  _const_spec((n2, 2 * n2))],
        out_specs=pl.BlockSpec((1, n2, tw), lambda i, j: (i, 0, j)),
        out_shape=jax.ShapeDtypeStruct((b, n2, width), BF16),
        compiler_params=_params("parallel", "arbitrary"),
        name="fourier_stage2",
    )(t.reshape(b, 2 * n2, width), f2)
    return y.reshape(b, n, F_W)


def _group_mean(y, gm_ref):
    hi, lo = _split2(y)
    return _dot(hi, gm_ref[...]) + _dot(lo, gm_ref[...])


def _conv_kernel(cc_ref, dw_ref, db_ref, lg_ref, lb_ref, gm_ref, o_ref, a_ref):
    n = cc_ref.shape[1]
    rt = min(CONV_TILE, n)
    a_ref[0:CONV_PAD, :] = jnp.zeros((CONV_PAD, C_W), F32)
    a_ref[n + CONV_PAD:n + 2 * CONV_PAD, :] = jnp.zeros((CONV_PAD, C_W), F32)

    def glu(i, carry):
        r = pl.multiple_of(i * rt, rt)
        blk = cc_ref[0, pl.ds(r, rt), :].astype(F32)
        a_ref[pl.ds(r + CONV_PAD, rt), :] = blk[:, :C_W] * jax.nn.sigmoid(blk[:, C_W:])
        return carry

    lax.fori_loop(0, n // rt, glu, 0)

    def conv(i, carry):
        r = pl.multiple_of(i * rt, rt)
        win = a_ref[pl.ds(r, rt + 2 * CONV_PAD), :]
        span = rt + 2 * CONV_PAD - 8
        phases = [win[s:s + span] for s in range(8)]
        acc = jnp.broadcast_to(db_ref[...], (rt, C_W))
        for j in range(CONV_K):
            off = CONV_PAD - CONV_K // 2 + j
            base = 8 * (off // 8)
            acc = acc + phases[off % 8][base:base + rt] * dw_ref[j:j + 1, :]
        d = acc - _group_mean(acc, gm_ref)
        var = _group_mean(d * d, gm_ref)
        yn = d * lax.rsqrt(var + EPS) * lg_ref[...] + lb_ref[...]
        o_ref[0, pl.ds(r, rt), :] = _silu(yn).astype(o_ref.dtype)
        return carry

    lax.fori_loop(0, n // rt, conv, 0)


def _conv(cc, dw, db, ln_g, ln_b):
    b, n, _ = cc.shape
    gw = C_W // C_G
    gm = jnp.asarray(np.kron(np.eye(C_G), np.full((gw, gw), 1.0 / gw)), BF16)
    dw32 = jnp.concatenate([dw, jnp.zeros((32 - CONV_K, C_W), F32)], axis=0)
    return pl.pallas_call(
        _conv_kernel,
        grid=(b,),
        in_specs=[pl.BlockSpec((1, n, 2 * C_W), lambda i: (i, 0, 0)),
                  _const_spec((32, C_W)), _const_spec((1, C_W)), _const_spec((1, C_W)),
                  _const_spec((1, C_W)), _const_spec((C_W, C_W))],
        out_specs=pl.BlockSpec((1, n, C_W), lambda i: (i, 0, 0)),
        out_shape=jax.ShapeDtypeStruct((b, n, C_W), BF16),
        scratch_shapes=[pltpu.VMEM((n + 2 * CONV_PAD, C_W), F32)],
        compiler_params=_params("parallel"),
        name="conv_module",
    )(cc, dw32, db.reshape(1, C_W), ln_g.reshape(1, C_W), ln_b.reshape(1, C_W), gm)


def _top2_route(logits):
    lane = lax.broadcasted_iota(jnp.int32, logits.shape, 1)
    lg = jnp.where(lane < N_EXPERTS, logits, -jnp.inf)
    v1 = jnp.max(lg, axis=-1, keepdims=True)
    i1 = jnp.min(jnp.where(lg == v1, lane, 128), axis=-1, keepdims=True)
    lg2 = jnp.where(lane == i1, -jnp.inf, lg)
    v2 = jnp.max(lg2, axis=-1, keepdims=True)
    i2 = jnp.min(jnp.where(lg2 == v2, lane, 128), axis=-1, keepdims=True)
    e2 = jnp.exp(v2 - v1)
    w1 = 1.0 / (1.0 + e2)
    w2 = e2 / (1.0 + e2)
    return jnp.where(lane == 0, i1.astype(F32),
                     jnp.where(lane == 1, i2.astype(F32),
                               jnp.where(lane == 2, w1, jnp.where(lane == 3, w2, 0.0))))


def _pack_halves(y):
    w = y.shape[1] // 2
    lo = pltpu.bitcast(y[:, :w].astype(BF16).astype(F32), jnp.uint32)
    hi = pltpu.bitcast(y[:, w:].astype(BF16).astype(F32), jnp.uint32)
    return (hi & jnp.uint32(0xFFFF0000)) | (lo >> 16)


def _unpack_halves(p):
    lo = pltpu.bitcast(p << 16, F32)
    hi = pltpu.bitcast(p & jnp.uint32(0xFFFF0000), F32)
    return jnp.concatenate([lo, hi], axis=1)


def _out_kernel(hf_ref, hb_ref, o_ref, yf_ref, yc_ref, x_ref, mod_ref, hg_ref, g2_ref, wo_ref,
                *rest, with_router):
    if with_router:
        wr_ref, br_ref, tri_ref, xo_ref, hx_ref, route_ref, pos_ref, cnt_ref, carry_ref = rest
    else:
        xo_ref, hx_ref = rest
    hsum = hf_ref[0].astype(F32) + hb_ref[0].astype(F32)
    heads = [_rms(hsum[:, h * M_DH:(h + 1) * M_DH]) for h in range(M_H)]
    ym = jax.nn.sigmoid(o_ref[0].astype(F32)) * (jnp.concatenate(heads, axis=1) * hg_ref[...])
    proj = (_dot(ym.astype(BF16), wo_ref[0:M_W, :])
            + _dot(yf_ref[0], wo_ref[M_W:M_W + F_W, :])
            + _dot(yc_ref[0], wo_ref[M_W + F_W:, :]))
    xn = x_ref[0] + mod_ref[0, 2:3, :] * proj
    xo_ref[0] = xn
    hx = (_rms(xn) * g2_ref[...]) * (1.0 + mod_ref[0, 4:5, :]) + mod_ref[0, 3:4, :]
    if with_router:
        hx_ref[0] = _pack_halves(hx)
        route = _top2_route(_dot_precise(hx, wr_ref[...]) + br_ref[...])
        route_ref[0] = route

        @pl.when((pl.program_id(0) == 0) & (pl.program_id(1) == 0))
        def _():
            carry_ref[...] = jnp.zeros_like(carry_ref)

        lane = lax.broadcasted_iota(jnp.int32, route.shape, 1).astype(F32)
        e1, e2 = route[:, 0:1], route[:, 1:2]
        hit = jnp.where((lane == e1) | (lane == e2), 1.0, 0.0)
        before = _dot(tri_ref[...], hit.astype(BF16)) + carry_ref[0:1, :]
        rank1 = jnp.sum(jnp.where(lane == e1, before, 0.0), axis=-1, keepdims=True)
        rank2 = jnp.sum(jnp.where(lane == e2, before, 0.0), axis=-1, keepdims=True)
        pos_ref[0] = jnp.where(lane == 0.0, rank1, jnp.where(lane == 1.0, rank2, 0.0)).astype(jnp.int32)
        total = carry_ref[...] + jnp.sum(hit, axis=0, keepdims=True)
        carry_ref[...] = total
        cnt_ref[...] = total.astype(jnp.int32)
    else:
        hx_ref[0] = hx.astype(BF16)


def _out_proj(hf, hb, qvo, yf, yc, x, modv, head_g, norm2_g, w_out, router=None):
    b, n, d = x.shape
    tm = min(ROW_TILE, n)
    row = lambda w: pl.BlockSpec((1, tm, w), lambda i, j: (i, j, 0))
    in_specs = [row(M_W), row(M_W),
                pl.BlockSpec((1, tm, M_W), lambda i, j: (i, j, 2)),
                row(F_W), row(C_W), row(d),
                pl.BlockSpec((1, 8, d), lambda i, j: (i, 0, 0)),
                _const_spec((1, M_W)), _const_spec((1, d)), _const_spec((d, d))]
    args = [hf, hb, qvo, yf, yc, x, modv, head_g.reshape(1, M_W), norm2_g.reshape(1, d), w_out]
    out_specs = [row(d), row(d)]
    out_shape = [jax.ShapeDtypeStruct((b, n, d), F32), jax.ShapeDtypeStruct((b, n, d), BF16)]
    scratch = []
    sem = ("parallel", "arbitrary")
    if router is not None:
        out_specs[1] = row(d // 2)
        out_shape[1] = jax.ShapeDtypeStruct((b, n, d // 2), jnp.uint32)
        in_specs += [_const_spec((d, 128)), _const_spec((1, 128)), _const_spec((tm, tm))]
        args += list(router) + [jnp.asarray(np.tril(np.ones((tm, tm), np.float32), -1), BF16)]
        out_specs += [row(128), row(128), _const_spec((8, 128))]
        out_shape += [jax.ShapeDtypeStruct((b, n, 128), F32), jax.ShapeDtypeStruct((b, n, 128), jnp.int32),
                      jax.ShapeDtypeStruct((8, 128), jnp.int32)]
        scratch = [pltpu.VMEM((8, 128), F32)]
        sem = ("arbitrary", "arbitrary")
    return pl.pallas_call(
        functools.partial(_out_kernel, with_router=router is not None),
        grid=(b, n // tm),
        in_specs=in_specs,
        out_specs=tuple(out_specs),
        out_shape=tuple(out_shape),
        scratch_shapes=scratch,
        compiler_params=_params(*sem),
        name="out_proj",
    )(*args)


def _swiglu_tile(hx, wgu_ref, wd_ref):
    acc = jnp.zeros((hx.shape[0], D_MODEL), F32)
    for j in range(FFN_HIDDEN // HID_TILE):
        cs = slice(j * HID_TILE, (j + 1) * HID_TILE)
        us = slice(FFN_HIDDEN + j * HID_TILE, FFN_HIDDEN + (j + 1) * HID_TILE)
        a = _silu(_dot(hx, wgu_ref[:, cs])) * _dot(hx, wgu_ref[:, us])
        acc = acc + _dot(a.astype(BF16), wd_ref[cs, :])
    return acc


def _ffn_kernel(hx_ref, x_ref, mod_ref, wgu_ref, wd_ref, o_ref):
    o_ref[0] = x_ref[0] + mod_ref[0, 5:6, :] * _swiglu_tile(hx_ref[0], wgu_ref, wd_ref)


def _ffn(hx, x, modv, wgu, wd):
    b, n, d = x.shape
    tm = min(ROW_TILE, n)
    row = lambda: pl.BlockSpec((1, tm, d), lambda i, j: (i, j, 0))
    return pl.pallas_call(
        _ffn_kernel,
        grid=(b, n // tm),
        in_specs=[row(), row(), pl.BlockSpec((1, 8, d), lambda i, j: (i, 0, 0)),
                  _const_spec(wgu.shape), _const_spec(wd.shape)],
        out_specs=row(),
        out_shape=jax.ShapeDtypeStruct((b, n, d), F32),
        compiler_params=_params("parallel", "arbitrary"),
        name="ffn",
    )(hx, x, modv, wgu, wd)


def _sc_worker_rows(n_rows):
    assert n_rows % (SC_WORKERS * SC_CHUNK) == 0
    return n_rows // SC_WORKERS


def _sc_scatter_rows(x, dest, n_out):
    t, w = x.shape
    per_w = _sc_worker_rows(t)
    mesh = plsc.VectorSubcoreMesh(core_axis_name="c", subcore_axis_name="s")

    @functools.partial(
        pl.kernel, mesh=mesh,
        out_type=jax.ShapeDtypeStruct((n_out, w), x.dtype),
        scratch_types=[pltpu.VMEM((1, SC_CHUNK), jnp.int32),
                       pltpu.VMEM((SC_CHUNK, w), x.dtype),
                       pltpu.SemaphoreType.DMA],
    )
    def scatter(x_hbm, dest_hbm, out_hbm, idx_v, rows_v, sem):
        base = (lax.axis_index("s") * SC_CORES + lax.axis_index("c")) * per_w

        @pl.loop(0, per_w // SC_CHUNK)
        def _(g):
            off = base + g * SC_CHUNK
            pltpu.sync_copy(x_hbm.at[pl.ds(off, SC_CHUNK)], rows_v)
            for k in range(2):
                pltpu.sync_copy(dest_hbm.at[pl.ds(k, 1), pl.ds(off, SC_CHUNK)], idx_v)
                pltpu.async_copy(rows_v, out_hbm.at[idx_v.at[0]], sem).wait()

    return scatter(x, dest)


def _sc_gather_rows(table, idx):
    r = idx.shape[0]
    w = table.shape[1]
    per_w = _sc_worker_rows(r)
    mesh = plsc.VectorSubcoreMesh(core_axis_name="c", subcore_axis_name="s")

    @functools.partial(
        pl.kernel, mesh=mesh,
        out_type=jax.ShapeDtypeStruct((r, w), table.dtype),
        scratch_types=[pltpu.VMEM((1, SC_CHUNK), jnp.int32),
                       pltpu.VMEM((SC_CHUNK, w), table.dtype),
                       pltpu.SemaphoreType.DMA],
    )
    def gather(table_hbm, idx_hbm, out_hbm, idx_v, rows_v, sem):
        base = (lax.axis_index("s") * SC_CORES + lax.axis_index("c")) * per_w

        @pl.loop(0, per_w // SC_CHUNK)
        def _(g):
            off = base + g * SC_CHUNK
            pltpu.sync_copy(idx_hbm.at[pl.ds(0, 1), pl.ds(off, SC_CHUNK)], idx_v)
            pltpu.async_copy(table_hbm.at[idx_v.at[0]], rows_v, sem).wait()
            pltpu.sync_copy(rows_v, out_hbm.at[pl.ds(off, SC_CHUNK)])

    return gather(table, idx.reshape(1, r))


def _group_kernel(te_ref, nv_ref, xs_ref, wgu_ref, wd_ref, ys_ref):
    @pl.when(pl.program_id(0) < nv_ref[0])
    def _():
        hx = _unpack_halves(xs_ref[...]).astype(BF16)
        ys_ref[...] = _pack_halves(_swiglu_tile(hx, wgu_ref.at[0], wd_ref.at[0]))


def _grouped_swiglu(xs, tile_expert, n_valid, wgu, wd):
    r, wp = xs.shape
    grid_spec = pltpu.PrefetchScalarGridSpec(
        num_scalar_prefetch=2,
        grid=(r // GROUP_TILE,),
        in_specs=[pl.BlockSpec((GROUP_TILE, wp), lambda i, te, nv: (i, 0)),
                  pl.BlockSpec((1,) + wgu.shape[1:], lambda i, te, nv: (te[i], 0, 0)),
                  pl.BlockSpec((1,) + wd.shape[1:], lambda i, te, nv: (te[i], 0, 0))],
        out_specs=pl.BlockSpec((GROUP_TILE, wp), lambda i, te, nv: (i, 0)),
    )
    return pl.pallas_call(
        _group_kernel,
        grid_spec=grid_spec,
        out_shape=jax.ShapeDtypeStruct((r, wp), jnp.uint32),
        compiler_params=_params("arbitrary"),
        name="grouped_swiglu",
    )(tile_expert, n_valid, xs, wgu, wd)


def _combine_kernel(x_ref, y_ref, route_ref, mod_ref, fg_ref, o_ref):
    r = route_ref[0]
    moe = r[:, 2:3] * _unpack_halves(y_ref[0, 0]) + r[:, 3:4] * _unpack_halves(y_ref[1, 0])
    xn = x_ref[0] + mod_ref[0, 5:6, :] * moe
    o_ref[0] = _rms(xn) * fg_ref[...]


def _combine_final(x, y2, route, modv, final_g):
    b, n, d = x.shape
    tm = min(ROW_TILE, n)
    row = lambda w: pl.BlockSpec((1, tm, w), lambda i, j: (i, j, 0))
    return pl.pallas_call(
        _combine_kernel,
        grid=(b, n // tm),
        in_specs=[row(d), pl.BlockSpec((2, 1, tm, d // 2), lambda i, j: (0, i, j, 0)), row(128),
                  pl.BlockSpec((1, 8, d), lambda i, j: (i, 0, 0)), _const_spec((1, d))],
        out_specs=row(d),
        out_shape=jax.ShapeDtypeStruct((b, n, d), F32),
        compiler_params=_params("parallel", "arbitrary"),
        name="moe_combine",
    )(x, y2, route, modv, final_g.reshape(1, d))


def _moe_final(hxp, x, route, pos, counts, modv, final_g, wgu, wd):
    b, n, d = x.shape
    t = b * n
    counts = counts[0, :N_EXPERTS]
    padded = ((counts + GROUP_TILE - 1) // GROUP_TILE) * GROUP_TILE
    ends = jnp.cumsum(padded)
    starts = ends - padded
    experts = route.reshape(t, 128)[:, :2].astype(jnp.int32)
    dest = (starts[experts] + pos.reshape(t, 128)[:, :2]).T
    n_rows = 2 * t + N_EXPERTS * GROUP_TILE
    tile_start = jnp.arange(n_rows // GROUP_TILE, dtype=jnp.int32) * GROUP_TILE
    tile_expert = jnp.minimum(jnp.sum(ends[None, :] <= tile_start[:, None], axis=1), N_EXPERTS - 1).astype(jnp.int32)
    n_valid = (ends[-1:] // GROUP_TILE).astype(jnp.int32)

    xs = _sc_scatter_rows(hxp.reshape(t, d // 2), dest, n_rows)
    ys = _grouped_swiglu(xs, tile_expert, n_valid, wgu, wd)
    y2 = _sc_gather_rows(ys, dest.reshape(2 * t))
    return _combine_final(x, y2.reshape(2, b, n, d // 2), route, modv, final_g)


def _in_weights(w):
    wm = jnp.concatenate([w[:, O_Q:O_K], w[:, O_V:O_G], w[:, O_F:P_IN], w[:, O_G:O_F],
                          jnp.zeros((w.shape[0], 128 - 4 * M_H), w.dtype)], axis=1).astype(BF16)
    return wm, w[:, O_K:O_V].T.astype(BF16), w[:, O_G:O_F].T.astype(BF16)


def _zero_state(b):
    return (jnp.zeros((b, M_H, M_DH, 2 * M_DH), F32), jnp.full((b, M_H, 8, 128), -1e30, F32))


def _mod_rows(m):
    r = m.shape[0]
    m = m.reshape(r, 6, D_MODEL)
    return jnp.concatenate([m, jnp.zeros((r, 2, D_MODEL), F32)], axis=1)


def kernel(x, c, ctx, c_ctx, norm1_g, norm2_g, w_mod, b_mod, w_in, b_gate, mlstm_norm_g, conv_dw, conv_db, conv_norm_g, conv_norm_b, w_out, ffn_w_gate_up, ffn_w_down, moe_w_router, moe_b_router, moe_w_gate_up, moe_w_down, final_norm_g):
    b = x.shape[0]
    depth = w_in.shape[0]
    assert depth == 2 and b <= 7
    cond8 = jnp.concatenate([c, c_ctx[None, :], jnp.zeros((7 - b, D_MODEL), F32)], axis=0)
    xc = ctx
    for layer in range(depth):
        last = layer == depth - 1
        mods = _modulation(cond8, w_mod[layer], b_mod[layer])
        mod_lat = _mod_rows(mods[:b])
        mod_ctx = jnp.broadcast_to(_mod_rows(mods[b:b + 1]), (b, 8, D_MODEL))
        wm, wkt, wgt = _in_weights(w_in[layer])
        bg = b_gate[layer].reshape(16, 1)
        w_o = w_out[layer].astype(BF16)

        qvo_c, kt_c, zf_c, cc_c, gt_c, gc_c = _in_proj(xc, mod_ctx, norm1_g[layer], wm, wkt, wgt, bg)
        qvo_l, kt_l, zf_l, cc_l, gt_l, gc_l = _in_proj(x, mod_lat, norm1_g[layer], wm, wkt, wgt, bg)

        hf_c, hb_c, cf, cb, mf, mb = _mlstm(qvo_c, kt_c, gt_c, gc_c, _zero_state(b), _zero_state(b))
        hf_l, hb_l, _, _, _, _ = _mlstm(qvo_l, kt_l, gt_l, gc_l, (cf, mf), (cb, mb))

        conv_args = (conv_dw[layer], conv_db[layer], conv_norm_g[layer], conv_norm_b[layer])
        yf_l = _fourier(zf_l)
        yc_l = _conv(cc_l, *conv_args)
        if not last:
            x, hx = _out_proj(hf_l, hb_l, qvo_l, yf_l, yc_l, x, mod_lat, mlstm_norm_g[layer],
                              norm2_g[layer], w_o)
            yf_c = _fourier(zf_c)
            yc_c = _conv(cc_c, *conv_args)
            xc, hc = _out_proj(hf_c, hb_c, qvo_c, yf_c, yc_c, xc, mod_ctx, mlstm_norm_g[layer],
                               norm2_g[layer], w_o)
            wgu = ffn_w_gate_up[layer // 2].astype(BF16)
            wd = ffn_w_down[layer // 2].astype(BF16)
            x = _ffn(hx, x, mod_lat, wgu, wd)
            xc = _ffn(hc, xc, mod_ctx, wgu, wd)
        else:
            idx = layer // 2
            wr = jnp.concatenate([moe_w_router[idx], jnp.zeros((D_MODEL, 128 - N_EXPERTS), F32)], axis=1)
            br = jnp.concatenate([moe_b_router[idx], jnp.zeros((128 - N_EXPERTS,), F32)]).reshape(1, 128)
            x, hxp, route, pos, counts = _out_proj(hf_l, hb_l, qvo_l, yf_l, yc_l, x, mod_lat, mlstm_norm_g[layer],
                                                   norm2_g[layer], w_o, router=(wr, br))
            x = _moe_final(hxp, x, route, pos, counts, mod_lat, final_norm_g,
                           moe_w_gate_up[idx].astype(BF16), moe_w_down[idx].astype(BF16))
    return x
```

```python
import functools
import math

import numpy as np
import jax
import jax.numpy as jnp
from jax import lax
from jax.experimental import pallas as pl
from jax.experimental.pallas import tpu as pltpu
from jax.experimental.pallas import tpu_sc as plsc

F32 = jnp.float32
BF16 = jnp.bfloat16

D_MODEL = 1024
M_W = 512
M_H = 4
M_DH = 128
F_W = 256
F_G = 4
C_W = 256
C_G = 4
CONV_K = 31
FFN_HIDDEN = 2816
N_EXPERTS = 8
EPS = 1e-6
LOG2E = 1.4426950408889634

O_Q = 0
O_K = O_Q + M_W
O_V = O_K + M_W
O_O = O_V + M_W
O_G = O_O + M_W
O_F = O_G + 4 * M_H
O_CU = O_F + F_W
O_CG = O_CU + C_W
P_IN = O_CG + C_W

LC = 256
DFT_N1 = 128
DFT_DIRECT_MAX = 512
DFT_GROUP = 16
ROW_TILE = 512
HID_TILE = 256
CONV_TILE = 256
CONV_PAD = 16
GROUP_TILE = 512
SC_CORES = 2
SC_SUBCORES = 16
SC_WORKERS = SC_CORES * SC_SUBCORES
SC_CHUNK = 128
VMEM_LIMIT = 56 * 1024 * 1024


def _params(*sem):
    return pltpu.CompilerParams(dimension_semantics=sem, vmem_limit_bytes=VMEM_LIMIT)


def _const_spec(shape):
    zeros = (0,) * len(shape)
    return pl.BlockSpec(shape, lambda *_: zeros)


def _dot(a, b):
    return jnp.dot(a, b, preferred_element_type=F32)


def _dot_nt(a, b):
    return lax.dot_general(a, b, (((1,), (1,)), ((), ())), preferred_element_type=F32)


def _split2(a):
    hi = a.astype(BF16)
    lo = (a - hi.astype(F32)).astype(BF16)
    return hi, lo


def _split3(a):
    x1 = a.astype(BF16)
    r1 = a - x1.astype(F32)
    x2 = r1.astype(BF16)
    x3 = (r1 - x2.astype(F32)).astype(BF16)
    return x1, x2, x3


def _dot_precise(a, b):
    ah, al = _split2(a)
    bh, bl = _split2(b)
    return _dot(ah, bh) + _dot(al, bh) + _dot(ah, bl)


def _silu(x):
    return x * jax.nn.sigmoid(x)


def _log_sigmoid(x):
    return -(jnp.maximum(-x, 0.0) + jnp.log1p(jnp.exp(-jnp.abs(x))))


def _rms(x):
    return x * lax.rsqrt(jnp.mean(x * x, axis=-1, keepdims=True) + EPS)


def _mod_kernel(c_ref, w_ref, b_ref, o_ref):
    o_ref[...] = _dot_precise(_silu(c_ref[...]), w_ref[...]) + b_ref[...]


def _modulation(cond8, w_mod, b_mod):
    d = cond8.shape[1]
    n_out = w_mod.shape[1]
    tn = 1536
    return pl.pallas_call(
        _mod_kernel,
        grid=(n_out // tn,),
        in_specs=[_const_spec((8, d)),
                  pl.BlockSpec((d, tn), lambda j: (0, j)),
                  pl.BlockSpec((1, tn), lambda j: (0, j))],
        out_specs=pl.BlockSpec((8, tn), lambda j: (0, j)),
        out_shape=jax.ShapeDtypeStruct((8, n_out), F32),
        compiler_params=_params("arbitrary"),
        name="modulation",
    )(cond8, w_mod, b_mod.reshape(1, n_out))


def _in_kernel(x_ref, mod_ref, g_ref, wm_ref, wkt_ref, wgt_ref, bg_ref, bgr_ref,
               qvo_ref, kt_ref, zf_ref, cc_ref, gt_ref, gc_ref, *scratch):
    tm = x_ref.shape[1]
    h = _rms(x_ref[0]) * g_ref[...]
    h = h * (1.0 + mod_ref[0, 1:2, :]) + mod_ref[0, 0:1, :]
    hb = h.astype(BF16)
    p = _dot(hb, wm_ref[...])
    o_f, o_c, o_g = 3 * M_W, 3 * M_W + F_W, 3 * M_W + F_W + 2 * C_W
    qvo_ref[0] = p[:, :o_f].astype(BF16)
    if scratch:
        zs_ref, = scratch
        rows = zf_ref.shape[1]
        n2cnt = tm // rows
        for half in range(F_W // 128):
            zs_ref[half] = p[:, o_f + half * 128:o_f + (half + 1) * 128]
        for j in range(n2cnt):
            for half in range(F_W // 128):
                lanes = slice(j * F_W + half * 128, j * F_W + (half + 1) * 128)
                zf_ref[0, :, lanes] = zs_ref[half, pl.ds(j, rows, stride=n2cnt), :]
    else:
        zf_ref[0] = p[:, o_f:o_c].astype(BF16)
    cc_ref[0] = p[:, o_c:o_g].astype(BF16)
    kt_ref[0] = (_dot_nt(wkt_ref[...], hb) * (M_DH ** -0.5)).astype(BF16)
    gc_ref[0] = p[:, o_g:] + bgr_ref[...]
    gt = _dot_nt(wgt_ref[...], hb) + bg_ref[...]
    for j in range(tm // LC):
        gt_ref[0, j] = gt[:, j * LC:(j + 1) * LC]


def _in_proj(x, modv, norm_g, wm, wkt, wgt, bg):
    b, n, d = x.shape
    tm = min(ROW_TILE, n)
    nm = wm.shape[1]
    bg_row = jnp.concatenate([bg.reshape(1, 16), jnp.zeros((1, 128 - 16), F32)], axis=1)
    if n > DFT_DIRECT_MAX:
        n2cnt = n // DFT_N1
        assert tm % n2cnt == 0 and (tm // n2cnt) % 8 == 0
        zf_sds = jax.ShapeDtypeStruct((b, DFT_N1, n2cnt * F_W), F32)
        zf_spec = pl.BlockSpec((1, tm // n2cnt, n2cnt * F_W), lambda i, j: (i, j, 0))
        scratch = [pltpu.VMEM((F_W // 128, tm, 128), F32)]
    else:
        zf_sds = jax.ShapeDtypeStruct((b, n, F_W), BF16)
        zf_spec = pl.BlockSpec((1, tm, F_W), lambda i, j: (i, j, 0))
        scratch = []
    out_shape = (
        jax.ShapeDtypeStruct((b, n, 3 * M_W), BF16),
        jax.ShapeDtypeStruct((b, M_W, n), BF16),
        zf_sds,
        jax.ShapeDtypeStruct((b, n, 2 * C_W), BF16),
        jax.ShapeDtypeStruct((b, n // LC, 16, LC), F32),
        jax.ShapeDtypeStruct((b, n, 128), F32),
    )
    return pl.pallas_call(
        _in_kernel,
        grid=(b, n // tm),
        in_specs=[pl.BlockSpec((1, tm, d), lambda i, j: (i, j, 0)),
                  pl.BlockSpec((1, 8, d), lambda i, j: (i, 0, 0)),
                  _const_spec((1, d)),
                  _const_spec((d, nm)),
                  _const_spec((M_W, d)),
                  _const_spec((16, d)),
                  _const_spec((16, 1)),
                  _const_spec((1, 128))],
        out_specs=(pl.BlockSpec((1, tm, 3 * M_W), lambda i, j: (i, j, 0)),
                   pl.BlockSpec((1, M_W, tm), lambda i, j: (i, 0, j)),
                   zf_spec,
                   pl.BlockSpec((1, tm, 2 * C_W), lambda i, j: (i, j, 0)),
                   pl.BlockSpec((1, tm // LC, 16, LC), lambda i, j: (i, j, 0, 0)),
                   pl.BlockSpec((1, tm, 128), lambda i, j: (i, j, 0))),
        out_shape=out_shape,
        scratch_shapes=scratch,
        compiler_params=_params("parallel", "arbitrary"),
        name="in_proj",
    )(x, modv, norm_g.reshape(1, d), wm, wkt, wgt, bg, bg_row)


def _mlstm_kernel(qf_ref, vf_ref, ktf_ref, gtf_ref, gcf_ref, qb_ref, vb_ref, ktb_ref, gtb_ref, gcb_ref,
                  cf0_ref, cb0_ref, mf0_ref, mb0_ref, u3_ref, lo3_ref, ut3_ref, lot3_ref,
                  hf_ref, hb_ref, cf_ref, cb_ref, mf_ref, mb_ref):
    @pl.when(pl.program_id(1) == 0)
    def _():
        cf_ref[...] = cf0_ref[...]
        cb_ref[...] = cb0_ref[...]
        mf_ref[...] = mf0_ref[...]
        mb_ref[...] = mb0_ref[...]

    gf = gtf_ref[0, 0]
    gb = gtb_ref[0, 0]
    rows_f = _dot(jnp.concatenate(_split3(_log_sigmoid(gf)), axis=1), u3_ref[...])
    rows_b = _dot(jnp.concatenate(_split3(_log_sigmoid(gb)), axis=1), lo3_ref[...])
    cols_f = _dot(ut3_ref[...], jnp.concatenate(_split3(_log_sigmoid(gcf_ref[0])), axis=0))
    cols_b = _dot(lot3_ref[...], jnp.concatenate(_split3(_log_sigmoid(gcb_ref[0])), axis=0))

    t_idx = lax.broadcasted_iota(jnp.int32, (LC, LC), 0)
    s_idx = lax.broadcasted_iota(jnp.int32, (LC, LC), 1)
    ones = jnp.ones((LC, M_DH), BF16)

    dirs = (
        (qf_ref, vf_ref, ktf_ref, gf, rows_f, cols_f, s_idx <= t_idx, cf_ref, mf_ref, hf_ref, 0, 4, LC - 1),
        (qb_ref, vb_ref, ktb_ref, gb, rows_b, cols_b, s_idx >= t_idx, cb_ref, mb_ref, hb_ref, 8, 12, 0),
    )
    outs = []
    for q_ref, v_ref, kt_ref, g, rows, cols, mask, c_ref, m_ref, h_ref, i_off, f_off, last in dirs:
        for h in range(M_H):
            hs = slice(h * M_DH, (h + 1) * M_DH)
            q = q_ref[0, :, hs]
            kt = kt_ref[0, hs, :]
            vaug = jnp.concatenate([v_ref[0, :, hs], ones], axis=1)
            i_row = g[i_off + h:i_off + h + 1, :]
            b_row = rows[f_off + h:f_off + h + 1, :]
            b_col = jnp.broadcast_to(cols[:, f_off + h:f_off + h + 1], (LC, M_DH))
            b_last = b_row[:, last:last + 1]
            c0 = c_ref[0, h]
            m0 = m_ref[0, h, 0:1, 0:1]

            am = jnp.where(mask, (i_row - b_row) * LOG2E, -jnp.inf)
            g2 = jnp.maximum(m0 * LOG2E, jnp.max(am, axis=-1, keepdims=True))
            g2 = jnp.broadcast_to(g2, (LC, M_DH))
            e = jnp.exp2(am - jnp.concatenate([g2, g2], axis=1))
            s = (_dot(q, kt) * e).astype(BF16)
            qi = (q.astype(F32) * jnp.exp2(m0 * LOG2E - g2)).astype(BF16)
            na = _dot(s, vaug) + _dot(qi, c0.astype(BF16))
            floor = jnp.exp2(-(b_col * LOG2E + g2))
            h_new = (na[:, :M_DH] / jnp.maximum(jnp.abs(na[:, M_DH:]), floor)).astype(h_ref.dtype)

            gs = b_last - b_row + i_row
            m_new = jnp.maximum(b_last + m0, jnp.max(gs, axis=-1, keepdims=True))
            kw = (kt.astype(F32) * jnp.exp(gs - m_new)).astype(BF16)
            c_new = jnp.exp(b_last + m0 - m_new) * c0 + _dot(kw, vaug)
            outs.append((h_ref, c_ref, m_ref, h, hs, h_new, c_new, m_new))

    for h_ref, c_ref, m_ref, h, hs, h_new, c_new, m_new in outs:
        h_ref[0, :, hs] = h_new
        c_ref[0, h] = c_new
        m_ref[0, h] = jnp.broadcast_to(m_new, (8, 128))


def _tri_constants():
    s = np.arange(LC)[:, None]
    t = np.arange(LC)[None, :]
    u = (s <= t).astype(np.float32)
    lo = (s >= t).astype(np.float32)
    u3 = np.concatenate([u, u, u], axis=0)
    lo3 = np.concatenate([lo, lo, lo], axis=0)
    return (jnp.asarray(u3, BF16), jnp.asarray(lo3, BF16),
            jnp.asarray(u3.T.copy(), BF16), jnp.asarray(lo3.T.copy(), BF16))


def _mlstm(qvo, kt, gt, gc, state_f, state_b):
    b, n, _ = qvo.shape
    nc = n // LC
    cf0, mf0 = state_f
    cb0, mb0 = state_b
    fwd = lambda i, c: (i, c, 0)
    bwd = lambda i, c: (i, nc - 1 - c, 0)
    st_c = pl.BlockSpec((1, M_H, M_DH, 2 * M_DH), lambda i, c: (i, 0, 0, 0))
    st_m = pl.BlockSpec((1, M_H, 8, 128), lambda i, c: (i, 0, 0, 0))
    h_sds = jax.ShapeDtypeStruct((b, n, M_W), BF16)
    return pl.pallas_call(
        _mlstm_kernel,
        grid=(b, nc),
        in_specs=[pl.BlockSpec((1, LC, M_W), fwd),
                  pl.BlockSpec((1, LC, M_W), lambda i, c: (i, c, 1)),
                  pl.BlockSpec((1, M_W, LC), lambda i, c: (i, 0, c)),
                  pl.BlockSpec((1, 1, 16, LC), lambda i, c: (i, c, 0, 0)),
                  pl.BlockSpec((1, LC, 128), fwd),
                  pl.BlockSpec((1, LC, M_W), bwd),
                  pl.BlockSpec((1, LC, M_W), lambda i, c: (i, nc - 1 - c, 1)),
                  pl.BlockSpec((1, M_W, LC), lambda i, c: (i, 0, nc - 1 - c)),
                  pl.BlockSpec((1, 1, 16, LC), lambda i, c: (i, nc - 1 - c, 0, 0)),
                  pl.BlockSpec((1, LC, 128), bwd),
                  st_c, st_c, st_m, st_m,
                  _const_spec((3 * LC, LC)), _const_spec((3 * LC, LC)),
                  _const_spec((LC, 3 * LC)), _const_spec((LC, 3 * LC))],
        out_specs=(pl.BlockSpec((1, LC, M_W), fwd), pl.BlockSpec((1, LC, M_W), bwd),
                   st_c, st_c, st_m, st_m),
        out_shape=(h_sds, h_sds,
                   jax.ShapeDtypeStruct(cf0.shape, F32), jax.ShapeDtypeStruct(cb0.shape, F32),
                   jax.ShapeDtypeStruct(mf0.shape, F32), jax.ShapeDtypeStruct(mb0.shape, F32)),
        compiler_params=_params("parallel", "arbitrary"),
        name="mlstm",
    )(qvo, qvo, kt, gt, gc, qvo, qvo, kt, gt, gc, cf0, cb0, mf0, mb0, *_tri_constants())


def _channel_dft_matrix():
    gw = F_W // F_G
    j = np.arange(gw)[:, None]
    l = np.arange(gw)[None, :]
    ang = 2.0 * np.pi * j * l / gw
    eye = np.eye(F_G)
    bc = np.kron(eye, np.cos(ang)) / math.sqrt(gw)
    bs = np.kron(eye, np.sin(ang)) / math.sqrt(gw)
    return jnp.asarray(np.concatenate([bc, -bs], axis=1), BF16)


def _dft_cos_sin(n):
    k = np.arange(n)[:, None]
    m = np.arange(n)[None, :]
    ang = 2.0 * np.pi * ((k * m) % n) / n
    return np.cos(ang) / math.sqrt(n), np.sin(ang) / math.sqrt(n)


def _fourier1_kernel(z_ref, bcs_ref, f1_ref, ct_ref, st_ref, o_ref, ts_ref):
    n2cnt = o_ref.shape[1] // 2
    lane = lax.broadcasted_iota(jnp.int32, (DFT_N1, 128), 1)

    def group(g, carry):
        for r in range(DFT_GROUP):
            j = g * DFT_GROUP + r
            off = pl.multiple_of(j * F_W, F_W)
            ab = _dot(z_ref[0, :, pl.ds(off, F_W)].astype(BF16), bcs_ref[...])
            u = jnp.concatenate([ab[:, :F_W], ab[:, F_W:]], axis=0).astype(BF16)
            t = _dot(f1_ref[...], u)
            tre, tim = t[:DFT_N1], t[DFT_N1:]
            ct = jnp.sum(jnp.where(lane == j, ct_ref[...], 0.0), axis=-1, keepdims=True)
            st = jnp.sum(jnp.where(lane == j, st_ref[...], 0.0), axis=-1, keepdims=True)
            parts = (tre * ct + tim * st, tim * ct - tre * st)
            for part in range(2):
                for half in range(F_W // 128):
                    ts_ref[2 * part + half, r * DFT_N1:(r + 1) * DFT_N1, :] = parts[part][:, half * 128:(half + 1) * 128]
        for part in range(2):
            row0 = pl.multiple_of(part * n2cnt + g * DFT_GROUP, DFT_GROUP)
            for k1 in range(DFT_N1):
                for half in range(F_W // 128):
                    tile = ts_ref[2 * part + half, pl.ds(k1, DFT_GROUP, stride=DFT_N1), :]
                    lanes = slice(k1 * F_W + half * 128, k1 * F_W + (half + 1) * 128)
                    o_ref[0, pl.ds(row0, DFT_GROUP), lanes] = tile.astype(BF16)
        return carry

    lax.fori_loop(0, n2cnt // DFT_GROUP, group, 0)


def _fourier2_kernel(t_ref, f2_ref, o_ref, s_ref):
    n2cnt = f2_ref.shape[0]
    width = t_ref.shape[2]
    tw = min(4096, width)
    for c in range(width // tw):
        y = _dot(f2_ref[...], t_ref[0, :, c * tw:(c + 1) * tw])
        for kk in range(tw // F_W):
            k1 = c * (tw // F_W) + kk
            for half in range(F_W // 128):
                lanes = slice(kk * F_W + half * 128, kk * F_W + (half + 1) * 128)
                s_ref[half, k1 * n2cnt:(k1 + 1) * n2cnt, :] = y[:, lanes]
    for k2 in range(n2cnt):
        for a in range(DFT_N1 // DFT_GROUP):
            for half in range(F_W // 128):
                rows = s_ref[half, pl.ds(DFT_GROUP * a * n2cnt + k2, DFT_GROUP, stride=n2cnt), :]
                r0 = k2 * DFT_N1 + DFT_GROUP * a
                o_ref[0, r0:r0 + DFT_GROUP, half * 128:(half + 1) * 128] = rows.astype(o_ref.dtype)


def _fourier_direct_kernel(z_ref, bcs_ref, f_ref, o_ref):
    ab = _dot(z_ref[0], bcs_ref[...])
    u = jnp.concatenate([ab[:, :F_W], ab[:, F_W:]], axis=0).astype(BF16)
    o_ref[0] = _dot(f_ref[...], u).astype(o_ref.dtype)


def _fourier(zf, n):
    b = zf.shape[0]
    bcs = _channel_dft_matrix()
    if n <= DFT_DIRECT_MAX:
        c, s = _dft_cos_sin(n)
        fmat = jnp.asarray(np.concatenate([c, s], axis=1), BF16)
        return pl.pallas_call(
            _fourier_direct_kernel,
            grid=(b,),
            in_specs=[pl.BlockSpec((1, n, F_W), lambda i: (i, 0, 0)),
                      _const_spec((F_W, 2 * F_W)), _const_spec((n, 2 * n))],
            out_specs=pl.BlockSpec((1, n, F_W), lambda i: (i, 0, 0)),
            out_shape=jax.ShapeDtypeStruct((b, n, F_W), BF16),
            compiler_params=_params("parallel"),
            name="fourier_direct",
        )(zf, bcs, fmat)

    n1 = DFT_N1
    n2 = n // n1
    c1, s1 = _dft_cos_sin(n1)
    f1 = jnp.asarray(np.block([[c1, s1], [-s1, c1]]), BF16)
    k1 = np.arange(n1)[None, :]
    m2 = np.arange(n2)[:, None]
    ang = 2.0 * np.pi * ((m2 * k1) % n) / n
    assert n2 % DFT_GROUP == 0 and n2 <= 128
    pad = np.zeros((n1, 128 - n2))
    ct = jnp.asarray(np.concatenate([np.cos(ang).T, pad], axis=1), F32)
    st = jnp.asarray(np.concatenate([np.sin(ang).T, pad], axis=1), F32)
    c2, s2 = _dft_cos_sin(n2)
    f2 = jnp.asarray(np.concatenate([c2, s2], axis=1), BF16)
    width = n1 * F_W

    t = pl.pallas_call(
        _fourier1_kernel,
        grid=(b,),
        in_specs=[pl.BlockSpec((1, n1, n2 * F_W), lambda i: (i, 0, 0)),
                  _const_spec((F_W, 2 * F_W)), _const_spec((2 * n1, 2 * n1)),
                  _const_spec((n1, 128)), _const_spec((n1, 128))],
        out_specs=pl.BlockSpec((1, 2 * n2, width), lambda i: (i, 0, 0)),
        out_shape=jax.ShapeDtypeStruct((b, 2 * n2, width), BF16),
        scratch_shapes=[pltpu.VMEM((2 * (F_W // 128), DFT_GROUP * n1, 128), F32)],
        compiler_params=_params("parallel"),
        name="fourier_stage1",
    )(zf, bcs, f1, ct, st)
    return pl.pallas_call(
        _fourier2_kernel,
        grid=(b,),
        in_specs=[pl.BlockSpec((1, 2 * n2, width), lambda i: (i, 0, 0)),
                  _const_spec((n2, 2 * n2))],
        out_specs=pl.BlockSpec((1, n, F_W), lambda i: (i, 0, 0)),
        out_shape=jax.ShapeDtypeStruct((b, n, F_W), BF16),
        scratch_shapes=[pltpu.VMEM((F_W // 128, n, 128), F32)],
        compiler_params=_params("parallel"),
        name="fourier_stage2",
    )(t, f2)


def _group_mean(y, gm_ref):
    hi, lo = _split2(y)
    return _dot(hi, gm_ref[...]) + _dot(lo, gm_ref[...])


def _conv_kernel(cc_ref, dw_ref, db_ref, lg_ref, lb_ref, gm_ref, o_ref, a_ref):
    n = cc_ref.shape[1]
    rt = min(CONV_TILE, n)
    a_ref[0:CONV_PAD, :] = jnp.zeros((CONV_PAD, C_W), F32)
    a_ref[n + CONV_PAD:n + 2 * CONV_PAD, :] = jnp.zeros((CONV_PAD, C_W), F32)

    def glu(i, carry):
        r = pl.multiple_of(i * rt, rt)
        blk = cc_ref[0, pl.ds(r, rt), :].astype(F32)
        a_ref[pl.ds(r + CONV_PAD, rt), :] = blk[:, :C_W] * jax.nn.sigmoid(blk[:, C_W:])
        return carry

    lax.fori_loop(0, n // rt, glu, 0)

    def conv(i, carry):
        r = pl.multiple_of(i * rt, rt)
        win = a_ref[pl.ds(r, rt + 2 * CONV_PAD), :]
        span = rt + 2 * CONV_PAD - 8
        phases = [win[s:s + span] for s in range(8)]
        acc = jnp.broadcast_to(db_ref[...], (rt, C_W))
        for j in range(CONV_K):
            off = CONV_PAD - CONV_K // 2 + j
            base = 8 * (off // 8)
            acc = acc + phases[off % 8][base:base + rt] * dw_ref[j:j + 1, :]
        d = acc - _group_mean(acc, gm_ref)
        var = _group_mean(d * d, gm_ref)
        yn = d * lax.rsqrt(var + EPS) * lg_ref[...] + lb_ref[...]
        o_ref[0, pl.ds(r, rt), :] = _silu(yn).astype(o_ref.dtype)
        return carry

    lax.fori_loop(0, n // rt, conv, 0)


def _conv(cc, dw, db, ln_g, ln_b):
    b, n, _ = cc.shape
    gw = C_W // C_G
    gm = jnp.asarray(np.kron(np.eye(C_G), np.full((gw, gw), 1.0 / gw)), BF16)
    dw32 = jnp.concatenate([dw, jnp.zeros((32 - CONV_K, C_W), F32)], axis=0)
    return pl.pallas_call(
        _conv_kernel,
        grid=(b,),
        in_specs=[pl.BlockSpec((1, n, 2 * C_W), lambda i: (i, 0, 0)),
                  _const_spec((32, C_W)), _const_spec((1, C_W)), _const_spec((1, C_W)),
                  _const_spec((1, C_W)), _const_spec((C_W, C_W))],
        out_specs=pl.BlockSpec((1, n, C_W), lambda i: (i, 0, 0)),
        out_shape=jax.ShapeDtypeStruct((b, n, C_W), BF16),
        scratch_shapes=[pltpu.VMEM((n + 2 * CONV_PAD, C_W), F32)],
        compiler_params=_params("parallel"),
        name="conv_module",
    )(cc, dw32, db.reshape(1, C_W), ln_g.reshape(1, C_W), ln_b.reshape(1, C_W), gm)


def _top2_route(logits):
    lane = lax.broadcasted_iota(jnp.int32, logits.shape, 1)
    lg = jnp.where(lane < N_EXPERTS, logits, -jnp.inf)
    v1 = jnp.max(lg, axis=-1, keepdims=True)
    i1 = jnp.min(jnp.where(lg == v1, lane, 128), axis=-1, keepdims=True)
    lg2 = jnp.where(lane == i1, -jnp.inf, lg)
    v2 = jnp.max(lg2, axis=-1, keepdims=True)
    i2 = jnp.min(jnp.where(lg2 == v2, lane, 128), axis=-1, keepdims=True)
    e2 = jnp.exp(v2 - v1)
    w1 = 1.0 / (1.0 + e2)
    w2 = e2 / (1.0 + e2)
    return jnp.where(lane == 0, i1.astype(F32),
                     jnp.where(lane == 1, i2.astype(F32),
                               jnp.where(lane == 2, w1, jnp.where(lane == 3, w2, 0.0))))


def _pack_halves(y):
    w = y.shape[1] // 2
    lo = pltpu.bitcast(y[:, :w].astype(BF16).astype(F32), jnp.uint32)
    hi = pltpu.bitcast(y[:, w:].astype(BF16).astype(F32), jnp.uint32)
    return (hi & jnp.uint32(0xFFFF0000)) | (lo >> 16)


def _unpack_halves(p):
    lo = pltpu.bitcast(p << 16, F32)
    hi = pltpu.bitcast(p & jnp.uint32(0xFFFF0000), F32)
    return jnp.concatenate([lo, hi], axis=1)


def _out_kernel(hf_ref, hb_ref, o_ref, yf_ref, yc_ref, x_ref, mod_ref, hg_ref, g2_ref, wo_ref,
                *rest, with_router):
    if with_router:
        wr_ref, br_ref, tri_ref, xo_ref, hx_ref, route_ref, pos_ref, cnt_ref, carry_ref = rest
    else:
        xo_ref, hx_ref = rest
    hsum = hf_ref[0].astype(F32) + hb_ref[0].astype(F32)
    heads = [_rms(hsum[:, h * M_DH:(h + 1) * M_DH]) for h in range(M_H)]
    ym = jax.nn.sigmoid(o_ref[0].astype(F32)) * (jnp.concatenate(heads, axis=1) * hg_ref[...])
    proj = (_dot(ym.astype(BF16), wo_ref[0:M_W, :])
            + _dot(yf_ref[0], wo_ref[M_W:M_W + F_W, :])
            + _dot(yc_ref[0], wo_ref[M_W + F_W:, :]))
    xn = x_ref[0] + mod_ref[0, 2:3, :] * proj
    xo_ref[0] = xn
    hx = (_rms(xn) * g2_ref[...]) * (1.0 + mod_ref[0, 4:5, :]) + mod_ref[0, 3:4, :]
    if with_router:
        hx_ref[0] = _pack_halves(hx)
        route = _top2_route(_dot_precise(hx, wr_ref[...]) + br_ref[...])
        route_ref[0] = route

        @pl.when((pl.program_id(0) == 0) & (pl.program_id(1) == 0))
        def _():
            carry_ref[...] = jnp.zeros_like(carry_ref)

        lane = lax.broadcasted_iota(jnp.int32, route.shape, 1).astype(F32)
        e1, e2 = route[:, 0:1], route[:, 1:2]
        hit = jnp.where((lane == e1) | (lane == e2), 1.0, 0.0)
        before = _dot(tri_ref[...], hit.astype(BF16)) + carry_ref[0:1, :]
        rank1 = jnp.sum(jnp.where(lane == e1, before, 0.0), axis=-1, keepdims=True)
        rank2 = jnp.sum(jnp.where(lane == e2, before, 0.0), axis=-1, keepdims=True)
        pos_ref[0] = jnp.where(lane == 0.0, rank1, jnp.where(lane == 1.0, rank2, 0.0)).astype(jnp.int32)
        total = carry_ref[...] + jnp.sum(hit, axis=0, keepdims=True)
        carry_ref[...] = total
        cnt_ref[...] = total.astype(jnp.int32)
    else:
        hx_ref[0] = hx.astype(BF16)


def _out_proj(hf, hb, qvo, yf, yc, x, modv, head_g, norm2_g, w_out, router=None):
    b, n, d = x.shape
    tm = min(ROW_TILE, n)
    row = lambda w: pl.BlockSpec((1, tm, w), lambda i, j: (i, j, 0))
    in_specs = [row(M_W), row(M_W),
                pl.BlockSpec((1, tm, M_W), lambda i, j: (i, j, 2)),
                row(F_W), row(C_W), row(d),
                pl.BlockSpec((1, 8, d), lambda i, j: (i, 0, 0)),
                _const_spec((1, M_W)), _const_spec((1, d)), _const_spec((d, d))]
    args = [hf, hb, qvo, yf, yc, x, modv, head_g.reshape(1, M_W), norm2_g.reshape(1, d), w_out]
    out_specs = [row(d), row(d)]
    out_shape = [jax.ShapeDtypeStruct((b, n, d), F32), jax.ShapeDtypeStruct((b, n, d), BF16)]
    scratch = []
    sem = ("parallel", "arbitrary")
    if router is not None:
        out_specs[1] = row(d // 2)
        out_shape[1] = jax.ShapeDtypeStruct((b, n, d // 2), jnp.uint32)
        in_specs += [_const_spec((d, 128)), _const_spec((1, 128)), _const_spec((tm, tm))]
        args += list(router) + [jnp.asarray(np.tril(np.ones((tm, tm), np.float32), -1), BF16)]
        out_specs += [row(128), row(128), _const_spec((8, 128))]
        out_shape += [jax.ShapeDtypeStruct((b, n, 128), F32), jax.ShapeDtypeStruct((b, n, 128), jnp.int32),
                      jax.ShapeDtypeStruct((8, 128), jnp.int32)]
        scratch = [pltpu.VMEM((8, 128), F32)]
        sem = ("arbitrary", "arbitrary")
    return pl.pallas_call(
        functools.partial(_out_kernel, with_router=router is not None),
        grid=(b, n // tm),
        in_specs=in_specs,
        out_specs=tuple(out_specs),
        out_shape=tuple(out_shape),
        scratch_shapes=scratch,
        compiler_params=_params(*sem),
        name="out_proj",
    )(*args)


def _swiglu_tile(hx, wgu_ref, wd_ref):
    acc = jnp.zeros((hx.shape[0], D_MODEL), F32)
    for j in range(FFN_HIDDEN // HID_TILE):
        cs = slice(j * HID_TILE, (j + 1) * HID_TILE)
        us = slice(FFN_HIDDEN + j * HID_TILE, FFN_HIDDEN + (j + 1) * HID_TILE)
        a = _silu(_dot(hx, wgu_ref[:, cs])) * _dot(hx, wgu_ref[:, us])
        acc = acc + _dot(a.astype(BF16), wd_ref[cs, :])
    return acc


def _ffn_kernel(hx_ref, x_ref, mod_ref, wgu_ref, wd_ref, o_ref):
    o_ref[0] = x_ref[0] + mod_ref[0, 5:6, :] * _swiglu_tile(hx_ref[0], wgu_ref, wd_ref)


def _ffn(hx, x, modv, wgu, wd):
    b, n, d = x.shape
    tm = min(ROW_TILE, n)
    row = lambda: pl.BlockSpec((1, tm, d), lambda i, j: (i, j, 0))
    return pl.pallas_call(
        _ffn_kernel,
        grid=(b, n // tm),
        in_specs=[row(), row(), pl.BlockSpec((1, 8, d), lambda i, j: (i, 0, 0)),
                  _const_spec(wgu.shape), _const_spec(wd.shape)],
        out_specs=row(),
        out_shape=jax.ShapeDtypeStruct((b, n, d), F32),
        compiler_params=_params("parallel", "arbitrary"),
        name="ffn",
    )(hx, x, modv, wgu, wd)


def _sc_worker_rows(n_rows):
    assert n_rows % (SC_WORKERS * SC_CHUNK) == 0
    return n_rows // SC_WORKERS


def _sc_scatter_rows(x, dest, n_out):
    t, w = x.shape
    per_w = _sc_worker_rows(t)
    mesh = plsc.VectorSubcoreMesh(core_axis_name="c", subcore_axis_name="s")

    @functools.partial(
        pl.kernel, mesh=mesh,
        out_type=jax.ShapeDtypeStruct((n_out, w), x.dtype),
        scratch_types=[pltpu.VMEM((1, SC_CHUNK), jnp.int32),
                       pltpu.VMEM((SC_CHUNK, w), x.dtype),
                       pltpu.SemaphoreType.DMA],
    )
    def scatter(x_hbm, dest_hbm, out_hbm, idx_v, rows_v, sem):
        base = (lax.axis_index("s") * SC_CORES + lax.axis_index("c")) * per_w

        @pl.loop(0, per_w // SC_CHUNK)
        def _(g):
            off = base + g * SC_CHUNK
            pltpu.sync_copy(x_hbm.at[pl.ds(off, SC_CHUNK)], rows_v)
            for k in range(2):
                pltpu.sync_copy(dest_hbm.at[pl.ds(k, 1), pl.ds(off, SC_CHUNK)], idx_v)
                pltpu.async_copy(rows_v, out_hbm.at[idx_v.at[0]], sem).wait()

    return scatter(x, dest)


def _sc_gather_rows(table, idx):
    r = idx.shape[0]
    w = table.shape[1]
    per_w = _sc_worker_rows(r)
    mesh = plsc.VectorSubcoreMesh(core_axis_name="c", subcore_axis_name="s")

    @functools.partial(
        pl.kernel, mesh=mesh,
        out_type=jax.ShapeDtypeStruct((r, w), table.dtype),
        scratch_types=[pltpu.VMEM((1, SC_CHUNK), jnp.int32),
                       pltpu.VMEM((SC_CHUNK, w), table.dtype),
                       pltpu.SemaphoreType.DMA],
    )
    def gather(table_hbm, idx_hbm, out_hbm, idx_v, rows_v, sem):
        base = (lax.axis_index("s") * SC_CORES + lax.axis_index("c")) * per_w

        @pl.loop(0, per_w // SC_CHUNK)
        def _(g):
            off = base + g * SC_CHUNK
            pltpu.sync_copy(idx_hbm.at[pl.ds(0, 1), pl.ds(off, SC_CHUNK)], idx_v)
            pltpu.async_copy(table_hbm.at[idx_v.at[0]], rows_v, sem).wait()
            pltpu.sync_copy(rows_v, out_hbm.at[pl.ds(off, SC_CHUNK)])

    return gather(table, idx.reshape(1, r))


def _group_kernel(te_ref, nv_ref, xs_ref, wgu_ref, wd_ref, ys_ref):
    @pl.when(pl.program_id(0) < nv_ref[0])
    def _():
        hx = _unpack_halves(xs_ref[...]).astype(BF16)
        ys_ref[...] = _pack_halves(_swiglu_tile(hx, wgu_ref.at[0], wd_ref.at[0]))


def _grouped_swiglu(xs, tile_expert, n_valid, wgu, wd):
    r, wp = xs.shape
    grid_spec = pltpu.PrefetchScalarGridSpec(
        num_scalar_prefetch=2,
        grid=(r // GROUP_TILE,),
        in_specs=[pl.BlockSpec((GROUP_TILE, wp), lambda i, te, nv: (i, 0)),
                  pl.BlockSpec((1,) + wgu.shape[1:], lambda i, te, nv: (te[i], 0, 0)),
                  pl.BlockSpec((1,) + wd.shape[1:], lambda i, te, nv: (te[i], 0, 0))],
        out_specs=pl.BlockSpec((GROUP_TILE, wp), lambda i, te, nv: (i, 0)),
    )
    return pl.pallas_call(
        _group_kernel,
        grid_spec=grid_spec,
        out_shape=jax.ShapeDtypeStruct((r, wp), jnp.uint32),
        compiler_params=_params("arbitrary"),
        name="grouped_swiglu",
    )(tile_expert, n_valid, xs, wgu, wd)


def _combine_kernel(x_ref, y_ref, route_ref, mod_ref, fg_ref, o_ref):
    r = route_ref[0]
    moe = r[:, 2:3] * _unpack_halves(y_ref[0, 0]) + r[:, 3:4] * _unpack_halves(y_ref[1, 0])
    xn = x_ref[0] + mod_ref[0, 5:6, :] * moe
    o_ref[0] = _rms(xn) * fg_ref[...]


def _combine_final(x, y2, route, modv, final_g):
    b, n, d = x.shape
    tm = min(ROW_TILE, n)
    row = lambda w: pl.BlockSpec((1, tm, w), lambda i, j: (i, j, 0))
    return pl.pallas_call(
        _combine_kernel,
        grid=(b, n // tm),
        in_specs=[row(d), pl.BlockSpec((2, 1, tm, d // 2), lambda i, j: (0, i, j, 0)), row(128),
                  pl.BlockSpec((1, 8, d), lambda i, j: (i, 0, 0)), _const_spec((1, d))],
        out_specs=row(d),
        out_shape=jax.ShapeDtypeStruct((b, n, d), F32),
        compiler_params=_params("parallel", "arbitrary"),
        name="moe_combine",
    )(x, y2, route, modv, final_g.reshape(1, d))


def _moe_final(hxp, x, route, pos, counts, modv, final_g, wgu, wd):
    b, n, d = x.shape
    t = b * n
    counts = counts[0, :N_EXPERTS]
    padded = ((counts + GROUP_TILE - 1) // GROUP_TILE) * GROUP_TILE
    ends = jnp.cumsum(padded)
    starts = ends - padded
    experts = route.reshape(t, 128)[:, :2].astype(jnp.int32)
    dest = (starts[experts] + pos.reshape(t, 128)[:, :2]).T
    n_rows = 2 * t + N_EXPERTS * GROUP_TILE
    tile_start = jnp.arange(n_rows // GROUP_TILE, dtype=jnp.int32) * GROUP_TILE
    tile_expert = jnp.minimum(jnp.sum(ends[None, :] <= tile_start[:, None], axis=1), N_EXPERTS - 1).astype(jnp.int32)
    n_valid = (ends[-1:] // GROUP_TILE).astype(jnp.int32)

    xs = _sc_scatter_rows(hxp.reshape(t, d // 2), dest, n_rows)
    ys = _grouped_swiglu(xs, tile_expert, n_valid, wgu, wd)
    y2 = _sc_gather_rows(ys, dest.reshape(2 * t))
    return _combine_final(x, y2.reshape(2, b, n, d // 2), route, modv, final_g)


def _in_weights(w):
    wm = jnp.concatenate([w[:, O_Q:O_K], w[:, O_V:O_G], w[:, O_F:P_IN], w[:, O_G:O_F],
                          jnp.zeros((w.shape[0], 128 - 4 * M_H), w.dtype)], axis=1).astype(BF16)
    return wm, w[:, O_K:O_V].T.astype(BF16), w[:, O_G:O_F].T.astype(BF16)


def _zero_state(b):
    return (jnp.zeros((b, M_H, M_DH, 2 * M_DH), F32), jnp.full((b, M_H, 8, 128), -1e30, F32))


def _mod_rows(m):
    r = m.shape[0]
    m = m.reshape(r, 6, D_MODEL)
    return jnp.concatenate([m, jnp.zeros((r, 2, D_MODEL), F32)], axis=1)


def kernel(x, c, ctx, c_ctx, norm1_g, norm2_g, w_mod, b_mod, w_in, b_gate, mlstm_norm_g, conv_dw, conv_db, conv_norm_g, conv_norm_b, w_out, ffn_w_gate_up, ffn_w_down, moe_w_router, moe_b_router, moe_w_gate_up, moe_w_down, final_norm_g):
    b = x.shape[0]
    depth = w_in.shape[0]
    assert depth == 2 and b <= 7
    cond8 = jnp.concatenate([c, c_ctx[None, :], jnp.zeros((7 - b, D_MODEL), F32)], axis=0)
    xc = ctx
    for layer in range(depth):
        last = layer == depth - 1
        mods = _modulation(cond8, w_mod[layer], b_mod[layer])
        mod_lat = _mod_rows(mods[:b])
        mod_ctx = jnp.broadcast_to(_mod_rows(mods[b:b + 1]), (b, 8, D_MODEL))
        wm, wkt, wgt = _in_weights(w_in[layer])
        bg = b_gate[layer].reshape(16, 1)
        w_o = w_out[layer].astype(BF16)

        qvo_c, kt_c, zf_c, cc_c, gt_c, gc_c = _in_proj(xc, mod_ctx, norm1_g[layer], wm, wkt, wgt, bg)
        qvo_l, kt_l, zf_l, cc_l, gt_l, gc_l = _in_proj(x, mod_lat, norm1_g[layer], wm, wkt, wgt, bg)

        hf_c, hb_c, cf, cb, mf, mb = _mlstm(qvo_c, kt_c, gt_c, gc_c, _zero_state(b), _zero_state(b))
        hf_l, hb_l, _, _, _, _ = _mlstm(qvo_l, kt_l, gt_l, gc_l, (cf, mf), (cb, mb))

        conv_args = (conv_dw[layer], conv_db[layer], conv_norm_g[layer], conv_norm_b[layer])
        yf_l = _fourier(zf_l, x.shape[1])
        yc_l = _conv(cc_l, *conv_args)
        if not last:
            x, hx = _out_proj(hf_l, hb_l, qvo_l, yf_l, yc_l, x, mod_lat, mlstm_norm_g[layer],
                              norm2_g[layer], w_o)
            yf_c = _fourier(zf_c, xc.shape[1])
            yc_c = _conv(cc_c, *conv_args)
            xc, hc = _out_proj(hf_c, hb_c, qvo_c, yf_c, yc_c, xc, mod_ctx, mlstm_norm_g[layer],
                               norm2_g[layer], w_o)
            wgu = ffn_w_gate_up[layer // 2].astype(BF16)
            wd = ffn_w_down[layer // 2].astype(BF16)
            x = _ffn(hx, x, mod_lat, wgu, wd)
            xc = _ffn(hc, xc, mod_ctx, wgu, wd)
        else:
            idx = layer // 2
            wr = jnp.concatenate([moe_w_router[idx], jnp.zeros((D_MODEL, 128 - N_EXPERTS), F32)], axis=1)
            br = jnp.concatenate([moe_b_router[idx], jnp.zeros((128 - N_EXPERTS,), F32)]).reshape(1, 128)
            x, hxp, route, pos, counts = _out_proj(hf_l, hb_l, qvo_l, yf_l, yc_l, x, mod_lat, mlstm_norm_g[layer],
                                                   norm2_g[layer], w_o, router=(wr, br))
            x = _moe_final(hxp, x, route, pos, counts, mod_lat, final_norm_g,
                           moe_w_gate_up[idx].astype(BF16), moe_w_down[idx].astype(BF16))
    return x
```

```python
import functools
import math

import numpy as np
import jax
import jax.numpy as jnp
from jax import lax
from jax.experimental import pallas as pl
from jax.experimental.pallas import tpu as pltpu
from jax.experimental.pallas import tpu_sc as plsc

F32 = jnp.float32
BF16 = jnp.bfloat16

D_MODEL = 1024
M_W = 512
M_H = 4
M_DH = 128
F_W = 256
F_G = 4
C_W = 256
C_G = 4
CONV_K = 31
FFN_HIDDEN = 2816
N_EXPERTS = 8
EPS = 1e-6
LOG2E = 1.4426950408889634

O_Q = 0
O_K = O_Q + M_W
O_V = O_K + M_W
O_O = O_V + M_W
O_G = O_O + M_W
O_F = O_G + 4 * M_H
O_CU = O_F + F_W
O_CG = O_CU + C_W
P_IN = O_CG + C_W

LC = 256
DFT_N1 = 128
DFT_DIRECT_MAX = 512
DFT_GROUP = 16
ROW_TILE = 512
HID_TILE = 256
CONV_TILE = 256
CONV_PAD = 16
GROUP_TILE = 512
SC_CORES = 2
SC_SUBCORES = 16
SC_WORKERS = SC_CORES * SC_SUBCORES
SC_CHUNK = 128
VMEM_LIMIT = 56 * 1024 * 1024


def _params(*sem):
    return pltpu.CompilerParams(dimension_semantics=sem, vmem_limit_bytes=VMEM_LIMIT)


def _const_spec(shape):
    zeros = (0,) * len(shape)
    return pl.BlockSpec(shape, lambda *_: zeros)


def _dot(a, b):
    return jnp.dot(a, b, preferred_element_type=F32)


def _dot_nt(a, b):
    return lax.dot_general(a, b, (((1,), (1,)), ((), ())), preferred_element_type=F32)


def _split2(a):
    hi = a.astype(BF16)
    lo = (a - hi.astype(F32)).astype(BF16)
    return hi, lo


def _split3(a):
    x1 = a.astype(BF16)
    r1 = a - x1.astype(F32)
    x2 = r1.astype(BF16)
    x3 = (r1 - x2.astype(F32)).astype(BF16)
    return x1, x2, x3


def _dot_precise(a, b):
    ah, al = _split2(a)
    bh, bl = _split2(b)
    return _dot(ah, bh) + _dot(al, bh) + _dot(ah, bl)


def _silu(x):
    return x * jax.nn.sigmoid(x)


def _log_sigmoid(x):
    return -(jnp.maximum(-x, 0.0) + jnp.log1p(jnp.exp(-jnp.abs(x))))


def _rms(x):
    return x * lax.rsqrt(jnp.mean(x * x, axis=-1, keepdims=True) + EPS)


def _mod_kernel(c_ref, w_ref, b_ref, o_ref):
    o_ref[...] = _dot_precise(_silu(c_ref[...]), w_ref[...]) + b_ref[...]


def _modulation(cond8, w_mod, b_mod):
    d = cond8.shape[1]
    n_out = w_mod.shape[1]
    tn = 1536
    return pl.pallas_call(
        _mod_kernel,
        grid=(n_out // tn,),
        in_specs=[_const_spec((8, d)),
                  pl.BlockSpec((d, tn), lambda j: (0, j)),
                  pl.BlockSpec((1, tn), lambda j: (0, j))],
        out_specs=pl.BlockSpec((8, tn), lambda j: (0, j)),
        out_shape=jax.ShapeDtypeStruct((8, n_out), F32),
        compiler_params=_params("arbitrary"),
        name="modulation",
    )(cond8, w_mod, b_mod.reshape(1, n_out))


def _in_kernel(x_ref, mod_ref, g_ref, wm_ref, wkt_ref, wgt_ref, bg_ref, bgr_ref,
               qvo_ref, kt_ref, zf_ref, cc_ref, gt_ref, gc_ref, *scratch):
    tm = x_ref.shape[1]
    h = _rms(x_ref[0]) * g_ref[...]
    h = h * (1.0 + mod_ref[0, 1:2, :]) + mod_ref[0, 0:1, :]
    hb = h.astype(BF16)
    p = _dot(hb, wm_ref[...])
    kt = _dot_nt(wkt_ref[...], hb) * (M_DH ** -0.5)
    gt = _dot_nt(wgt_ref[...], hb) + bg_ref[...]
    o_f, o_c, o_g = 3 * M_W, 3 * M_W + F_W, 3 * M_W + F_W + 2 * C_W
    qvo_ref[0] = p[:, :o_f].astype(BF16)
    if scratch:
        zs_ref, = scratch
        rows = zf_ref.shape[1]
        n2cnt = tm // rows
        for half in range(F_W // 128):
            zs_ref[half] = p[:, o_f + half * 128:o_f + (half + 1) * 128]
        for j in range(n2cnt):
            for half in range(F_W // 128):
                lanes = slice(j * F_W + half * 128, j * F_W + (half + 1) * 128)
                zf_ref[0, :, lanes] = zs_ref[half, pl.ds(j, rows, stride=n2cnt), :]
    else:
        zf_ref[0] = p[:, o_f:o_c].astype(BF16)
    cc_ref[0] = p[:, o_c:o_g].astype(BF16)
    kt_ref[0] = kt.astype(BF16)
    gc_ref[0] = p[:, o_g:] + bgr_ref[...]
    for j in range(tm // LC):
        gt_ref[0, j] = gt[:, j * LC:(j + 1) * LC]


def _in_proj(x, modv, norm_g, wm, wkt, wgt, bg):
    b, n, d = x.shape
    tm = min(ROW_TILE, n)
    nm = wm.shape[1]
    bg_row = jnp.concatenate([bg.reshape(1, 16), jnp.zeros((1, 128 - 16), F32)], axis=1)
    if n > DFT_DIRECT_MAX:
        n2cnt = n // DFT_N1
        assert tm % n2cnt == 0 and (tm // n2cnt) % 8 == 0
        zf_sds = jax.ShapeDtypeStruct((b, DFT_N1, n2cnt * F_W), F32)
        zf_spec = pl.BlockSpec((1, tm // n2cnt, n2cnt * F_W), lambda i, j: (i, j, 0))
        scratch = [pltpu.VMEM((F_W // 128, tm, 128), F32)]
    else:
        zf_sds = jax.ShapeDtypeStruct((b, n, F_W), BF16)
        zf_spec = pl.BlockSpec((1, tm, F_W), lambda i, j: (i, j, 0))
        scratch = []
    out_shape = (
        jax.ShapeDtypeStruct((b, n, 3 * M_W), BF16),
        jax.ShapeDtypeStruct((b, M_W, n), BF16),
        zf_sds,
        jax.ShapeDtypeStruct((b, n, 2 * C_W), BF16),
        jax.ShapeDtypeStruct((b, n // LC, 16, LC), F32),
        jax.ShapeDtypeStruct((b, n, 128), F32),
    )
    return pl.pallas_call(
        _in_kernel,
        grid=(b, n // tm),
        in_specs=[pl.BlockSpec((1, tm, d), lambda i, j: (i, j, 0)),
                  pl.BlockSpec((1, 8, d), lambda i, j: (i, 0, 0)),
                  _const_spec((1, d)),
                  _const_spec((d, nm)),
                  _const_spec((M_W, d)),
                  _const_spec((16, d)),
                  _const_spec((16, 1)),
                  _const_spec((1, 128))],
        out_specs=(pl.BlockSpec((1, tm, 3 * M_W), lambda i, j: (i, j, 0)),
                   pl.BlockSpec((1, M_W, tm), lambda i, j: (i, 0, j)),
                   zf_spec,
                   pl.BlockSpec((1, tm, 2 * C_W), lambda i, j: (i, j, 0)),
                   pl.BlockSpec((1, tm // LC, 16, LC), lambda i, j: (i, j, 0, 0)),
                   pl.BlockSpec((1, tm, 128), lambda i, j: (i, j, 0))),
        out_shape=out_shape,
        scratch_shapes=scratch,
        compiler_params=_params("parallel", "arbitrary"),
        name="in_proj",
    )(x, modv, norm_g.reshape(1, d), wm, wkt, wgt, bg, bg_row)


def _mlstm_kernel(qf_ref, vf_ref, ktf_ref, gtf_ref, gcf_ref, qb_ref, vb_ref, ktb_ref, gtb_ref, gcb_ref,
                  cf0_ref, cb0_ref, mf0_ref, mb0_ref, u3_ref, lo3_ref, ut3_ref, lot3_ref,
                  hf_ref, hb_ref, cf_ref, cb_ref, mf_ref, mb_ref):
    @pl.when(pl.program_id(1) == 0)
    def _():
        cf_ref[...] = cf0_ref[...]
        cb_ref[...] = cb0_ref[...]
        mf_ref[...] = mf0_ref[...]
        mb_ref[...] = mb0_ref[...]

    gf = gtf_ref[0, 0]
    gb = gtb_ref[0, 0]
    rows_f = _dot(jnp.concatenate(_split3(_log_sigmoid(gf)), axis=1), u3_ref[...])
    rows_b = _dot(jnp.concatenate(_split3(_log_sigmoid(gb)), axis=1), lo3_ref[...])
    cols_f = _dot(ut3_ref[...], jnp.concatenate(_split3(_log_sigmoid(gcf_ref[0])), axis=0))
    cols_b = _dot(lot3_ref[...], jnp.concatenate(_split3(_log_sigmoid(gcb_ref[0])), axis=0))

    t_idx = lax.broadcasted_iota(jnp.int32, (LC, LC), 0)
    s_idx = lax.broadcasted_iota(jnp.int32, (LC, LC), 1)
    ones = jnp.ones((LC, M_DH), BF16)

    dirs = (
        (qf_ref, vf_ref, ktf_ref, gf, rows_f, cols_f, s_idx <= t_idx, cf_ref, mf_ref, hf_ref, 0, 4, LC - 1),
        (qb_ref, vb_ref, ktb_ref, gb, rows_b, cols_b, s_idx >= t_idx, cb_ref, mb_ref, hb_ref, 8, 12, 0),
    )
    outs = []
    for q_ref, v_ref, kt_ref, g, rows, cols, mask, c_ref, m_ref, h_ref, i_off, f_off, last in dirs:
        for h in range(M_H):
            hs = slice(h * M_DH, (h + 1) * M_DH)
            q = q_ref[0, :, hs]
            kt = kt_ref[0, hs, :]
            vaug = jnp.concatenate([v_ref[0, :, hs], ones], axis=1)
            i_row = g[i_off + h:i_off + h + 1, :]
            b_row = rows[f_off + h:f_off + h + 1, :]
            b_col = jnp.broadcast_to(cols[:, f_off + h:f_off + h + 1], (LC, M_DH))
            b_last = b_row[:, last:last + 1]
            c0 = c_ref[0, h]
            m0 = m_ref[0, h, 0:1, 0:1]

            am = jnp.where(mask, (i_row - b_row) * LOG2E, -jnp.inf)
            g2 = jnp.maximum(m0 * LOG2E, jnp.max(am, axis=-1, keepdims=True))
            g2 = jnp.broadcast_to(g2, (LC, M_DH))
            e = jnp.exp2(am - jnp.concatenate([g2, g2], axis=1))
            s = (_dot(q, kt) * e).astype(BF16)
            qi = (q.astype(F32) * jnp.exp2(m0 * LOG2E - g2)).astype(BF16)
            na = _dot(s, vaug) + _dot(qi, c0.astype(BF16))
            floor = jnp.exp2(-(b_col * LOG2E + g2))
            h_new = (na[:, :M_DH] / jnp.maximum(jnp.abs(na[:, M_DH:]), floor)).astype(h_ref.dtype)

            gs = b_last - b_row + i_row
            m_new = jnp.maximum(b_last + m0, jnp.max(gs, axis=-1, keepdims=True))
            kw = (kt.astype(F32) * jnp.exp(gs - m_new)).astype(BF16)
            c_new = jnp.exp(b_last + m0 - m_new) * c0 + _dot(kw, vaug)
            outs.append((h_ref, c_ref, m_ref, h, hs, h_new, c_new, m_new))

    for h_ref, c_ref, m_ref, h, hs, h_new, c_new, m_new in outs:
        h_ref[0, :, hs] = h_new
        c_ref[0, h] = c_new
        m_ref[0, h] = jnp.broadcast_to(m_new, (8, 128))


def _tri_constants():
    s = np.arange(LC)[:, None]
    t = np.arange(LC)[None, :]
    u = (s <= t).astype(np.float32)
    lo = (s >= t).astype(np.float32)
    u3 = np.concatenate([u, u, u], axis=0)
    lo3 = np.concatenate([lo, lo, lo], axis=0)
    return (jnp.asarray(u3, BF16), jnp.asarray(lo3, BF16),
            jnp.asarray(u3.T.copy(), BF16), jnp.asarray(lo3.T.copy(), BF16))


def _mlstm(qvo, kt, gt, gc, state_f, state_b):
    b, n, _ = qvo.shape
    nc = n // LC
    cf0, mf0 = state_f
    cb0, mb0 = state_b
    fwd = lambda i, c: (i, c, 0)
    bwd = lambda i, c: (i, nc - 1 - c, 0)
    st_c = pl.BlockSpec((1, M_H, M_DH, 2 * M_DH), lambda i, c: (i, 0, 0, 0))
    st_m = pl.BlockSpec((1, M_H, 8, 128), lambda i, c: (i, 0, 0, 0))
    h_sds = jax.ShapeDtypeStruct((b, n, M_W), BF16)
    return pl.pallas_call(
        _mlstm_kernel,
        grid=(b, nc),
        in_specs=[pl.BlockSpec((1, LC, M_W), fwd),
                  pl.BlockSpec((1, LC, M_W), lambda i, c: (i, c, 1)),
                  pl.BlockSpec((1, M_W, LC), lambda i, c: (i, 0, c)),
                  pl.BlockSpec((1, 1, 16, LC), lambda i, c: (i, c, 0, 0)),
                  pl.BlockSpec((1, LC, 128), fwd),
                  pl.BlockSpec((1, LC, M_W), bwd),
                  pl.BlockSpec((1, LC, M_W), lambda i, c: (i, nc - 1 - c, 1)),
                  pl.BlockSpec((1, M_W, LC), lambda i, c: (i, 0, nc - 1 - c)),
                  pl.BlockSpec((1, 1, 16, LC), lambda i, c: (i, nc - 1 - c, 0, 0)),
                  pl.BlockSpec((1, LC, 128), bwd),
                  st_c, st_c, st_m, st_m,
                  _const_spec((3 * LC, LC)), _const_spec((3 * LC, LC)),
                  _const_spec((LC, 3 * LC)), _const_spec((LC, 3 * LC))],
        out_specs=(pl.BlockSpec((1, LC, M_W), fwd), pl.BlockSpec((1, LC, M_W), bwd),
                   st_c, st_c, st_m, st_m),
        out_shape=(h_sds, h_sds,
                   jax.ShapeDtypeStruct(cf0.shape, F32), jax.ShapeDtypeStruct(cb0.shape, F32),
                   jax.ShapeDtypeStruct(mf0.shape, F32), jax.ShapeDtypeStruct(mb0.shape, F32)),
        compiler_params=_params("parallel", "arbitrary"),
        name="mlstm",
    )(qvo, qvo, kt, gt, gc, qvo, qvo, kt, gt, gc, cf0, cb0, mf0, mb0, *_tri_constants())


def _channel_dft_matrix():
    gw = F_W // F_G
    j = np.arange(gw)[:, None]
    l = np.arange(gw)[None, :]
    ang = 2.0 * np.pi * j * l / gw
    eye = np.eye(F_G)
    bc = np.kron(eye, np.cos(ang)) / math.sqrt(gw)
    bs = np.kron(eye, np.sin(ang)) / math.sqrt(gw)
    return jnp.asarray(np.concatenate([bc, -bs], axis=1), BF16)


def _dft_cos_sin(n):
    k = np.arange(n)[:, None]
    m = np.arange(n)[None, :]
    ang = 2.0 * np.pi * ((k * m) % n) / n
    return np.cos(ang) / math.sqrt(n), np.sin(ang) / math.sqrt(n)


def _fourier1_kernel(z_ref, bcs_ref, f1_ref, ct_ref, st_ref, o_ref, ts_ref):
    n2cnt = o_ref.shape[1] // 2
    lane = lax.broadcasted_iota(jnp.int32, (DFT_N1, 128), 1)

    def group(g, carry):
        for r in range(DFT_GROUP):
            j = g * DFT_GROUP + r
            off = pl.multiple_of(j * F_W, F_W)
            ab = _dot(z_ref[0, :, pl.ds(off, F_W)].astype(BF16), bcs_ref[...])
            u = jnp.concatenate([ab[:, :F_W], ab[:, F_W:]], axis=0).astype(BF16)
            t = _dot(f1_ref[...], u)
            tre, tim = t[:DFT_N1], t[DFT_N1:]
            ct = jnp.sum(jnp.where(lane == j, ct_ref[...], 0.0), axis=-1, keepdims=True)
            st = jnp.sum(jnp.where(lane == j, st_ref[...], 0.0), axis=-1, keepdims=True)
            parts = (tre * ct + tim * st, tim * ct - tre * st)
            for part in range(2):
                for half in range(F_W // 128):
                    ts_ref[2 * part + half, r * DFT_N1:(r + 1) * DFT_N1, :] = parts[part][:, half * 128:(half + 1) * 128]
        for part in range(2):
            row0 = pl.multiple_of(part * n2cnt + g * DFT_GROUP, DFT_GROUP)
            for k1 in range(DFT_N1):
                for half in range(F_W // 128):
                    tile = ts_ref[2 * part + half, pl.ds(k1, DFT_GROUP, stride=DFT_N1), :]
                    lanes = slice(k1 * F_W + half * 128, k1 * F_W + (half + 1) * 128)
                    o_ref[0, pl.ds(row0, DFT_GROUP), lanes] = tile.astype(BF16)
        return carry

    lax.fori_loop(0, n2cnt // DFT_GROUP, group, 0)


def _fourier2_kernel(t_ref, f2_ref, o_ref, s_ref):
    n2cnt = f2_ref.shape[0]
    width = t_ref.shape[2]
    tw = min(4096, width)
    for c in range(width // tw):
        y = _dot(f2_ref[...], t_ref[0, :, c * tw:(c + 1) * tw])
        for kk in range(tw // F_W):
            k1 = c * (tw // F_W) + kk
            for half in range(F_W // 128):
                lanes = slice(kk * F_W + half * 128, kk * F_W + (half + 1) * 128)
                s_ref[half, k1 * n2cnt:(k1 + 1) * n2cnt, :] = y[:, lanes]
    for k2 in range(n2cnt):
        for a in range(DFT_N1 // DFT_GROUP):
            for half in range(F_W // 128):
                rows = s_ref[half, pl.ds(DFT_GROUP * a * n2cnt + k2, DFT_GROUP, stride=n2cnt), :]
                r0 = k2 * DFT_N1 + DFT_GROUP * a
                o_ref[0, r0:r0 + DFT_GROUP, half * 128:(half + 1) * 128] = rows.astype(o_ref.dtype)


def _fourier_direct_kernel(z_ref, bcs_ref, f_ref, o_ref):
    ab = _dot(z_ref[0], bcs_ref[...])
    u = jnp.concatenate([ab[:, :F_W], ab[:, F_W:]], axis=0).astype(BF16)
    o_ref[0] = _dot(f_ref[...], u).astype(o_ref.dtype)


def _fourier(zf, n):
    b = zf.shape[0]
    bcs = _channel_dft_matrix()
    if n <= DFT_DIRECT_MAX:
        c, s = _dft_cos_sin(n)
        fmat = jnp.asarray(np.concatenate([c, s], axis=1), BF16)
        return pl.pallas_call(
            _fourier_direct_kernel,
            grid=(b,),
            in_specs=[pl.BlockSpec((1, n, F_W), lambda i: (i, 0, 0)),
                      _const_spec((F_W, 2 * F_W)), _const_spec((n, 2 * n))],
            out_specs=pl.BlockSpec((1, n, F_W), lambda i: (i, 0, 0)),
            out_shape=jax.ShapeDtypeStruct((b, n, F_W), BF16),
            compiler_params=_params("parallel"),
            name="fourier_direct",
        )(zf, bcs, fmat)

    n1 = DFT_N1
    n2 = n // n1
    c1, s1 = _dft_cos_sin(n1)
    f1 = jnp.asarray(np.block([[c1, s1], [-s1, c1]]), BF16)
    k1 = np.arange(n1)[None, :]
    m2 = np.arange(n2)[:, None]
    ang = 2.0 * np.pi * ((m2 * k1) % n) / n
    assert n2 % DFT_GROUP == 0 and n2 <= 128
    pad = np.zeros((n1, 128 - n2))
    ct = jnp.asarray(np.concatenate([np.cos(ang).T, pad], axis=1), F32)
    st = jnp.asarray(np.concatenate([np.sin(ang).T, pad], axis=1), F32)
    c2, s2 = _dft_cos_sin(n2)
    f2 = jnp.asarray(np.concatenate([c2, s2], axis=1), BF16)
    width = n1 * F_W

    t = pl.pallas_call(
        _fourier1_kernel,
        grid=(b,),
        in_specs=[pl.BlockSpec((1, n1, n2 * F_W), lambda i: (i, 0, 0)),
                  _const_spec((F_W, 2 * F_W)), _const_spec((2 * n1, 2 * n1)),
                  _const_spec((n1, 128)), _const_spec((n1, 128))],
        out_specs=pl.BlockSpec((1, 2 * n2, width), lambda i: (i, 0, 0)),
        out_shape=jax.ShapeDtypeStruct((b, 2 * n2, width), BF16),
        scratch_shapes=[pltpu.VMEM((2 * (F_W // 128), DFT_GROUP * n1, 128), F32)],
        compiler_params=_params("parallel"),
        name="fourier_stage1",
    )(zf, bcs, f1, ct, st)
    return pl.pallas_call(
        _fourier2_kernel,
        grid=(b,),
        in_specs=[pl.BlockSpec((1, 2 * n2, width), lambda i: (i, 0, 0)),
                  _const_spec((n2, 2 * n2))],
        out_specs=pl.BlockSpec((1, n, F_W), lambda i: (i, 0, 0)),
        out_shape=jax.ShapeDtypeStruct((b, n, F_W), BF16),
        scratch_shapes=[pltpu.VMEM((F_W // 128, n, 128), F32)],
        compiler_params=_params("parallel"),
        name="fourier_stage2",
    )(t, f2)


def _group_mean(y, gm_ref):
    hi, lo = _split2(y)
    return _dot(hi, gm_ref[...]) + _dot(lo, gm_ref[...])


def _conv_kernel(cc_ref, dw_ref, db_ref, lg_ref, lb_ref, gm_ref, o_ref, a_ref, ph_ref):
    n = cc_ref.shape[1]
    rt = min(CONV_TILE, n)
    span = rt + 2 * CONV_PAD - 8
    a_ref[0:CONV_PAD, :] = jnp.zeros((CONV_PAD, C_W), F32)
    a_ref[n + CONV_PAD:n + 2 * CONV_PAD, :] = jnp.zeros((CONV_PAD, C_W), F32)

    def glu(i, carry):
        r = pl.multiple_of(i * rt, rt)
        blk = cc_ref[0, pl.ds(r, rt), :].astype(F32)
        a_ref[pl.ds(r + CONV_PAD, rt), :] = blk[:, :C_W] * jax.nn.sigmoid(blk[:, C_W:])
        return carry

    lax.fori_loop(0, n // rt, glu, 0)

    def conv(i, carry):
        r = pl.multiple_of(i * rt, rt)
        win = a_ref[pl.ds(r, rt + 2 * CONV_PAD), :]
        for s in range(1, 8):
            ph_ref[s - 1] = win[s:s + span]
        acc = jnp.broadcast_to(db_ref[...], (rt, C_W))
        for j in range(CONV_K):
            off = CONV_PAD - CONV_K // 2 + j
            base = 8 * (off // 8)
            if off % 8 == 0:
                tap = a_ref[pl.ds(r + base, rt), :]
            else:
                tap = ph_ref[off % 8 - 1, base:base + rt, :]
            acc = acc + tap * dw_ref[j:j + 1, :]
        d = acc - _group_mean(acc, gm_ref)
        var = _group_mean(d * d, gm_ref)
        yn = d * lax.rsqrt(var + EPS) * lg_ref[...] + lb_ref[...]
        o_ref[0, pl.ds(r, rt), :] = _silu(yn).astype(o_ref.dtype)
        return carry

    lax.fori_loop(0, n // rt, conv, 0)


def _conv(cc, dw, db, ln_g, ln_b):
    b, n, _ = cc.shape
    gw = C_W // C_G
    gm = jnp.asarray(np.kron(np.eye(C_G), np.full((gw, gw), 1.0 / gw)), BF16)
    dw32 = jnp.concatenate([dw, jnp.zeros((32 - CONV_K, C_W), F32)], axis=0)
    return pl.pallas_call(
        _conv_kernel,
        grid=(b,),
        in_specs=[pl.BlockSpec((1, n, 2 * C_W), lambda i: (i, 0, 0)),
                  _const_spec((32, C_W)), _const_spec((1, C_W)), _const_spec((1, C_W)),
                  _const_spec((1, C_W)), _const_spec((C_W, C_W))],
        out_specs=pl.BlockSpec((1, n, C_W), lambda i: (i, 0, 0)),
        out_shape=jax.ShapeDtypeStruct((b, n, C_W), BF16),
        scratch_shapes=[pltpu.VMEM((n + 2 * CONV_PAD, C_W), F32),
                        pltpu.VMEM((7, min(CONV_TILE, n) + 2 * CONV_PAD - 8, C_W), F32)],
        compiler_params=_params("parallel"),
        name="conv_module",
    )(cc, dw32, db.reshape(1, C_W), ln_g.reshape(1, C_W), ln_b.reshape(1, C_W), gm)


def _top2_route(logits):
    lane = lax.broadcasted_iota(jnp.int32, logits.shape, 1).astype(F32)
    lg = jnp.where(lane < N_EXPERTS, logits, -jnp.inf)
    v1 = jnp.max(lg, axis=-1, keepdims=True)
    i1 = jnp.min(jnp.where(lg == v1, lane, 128.0), axis=-1, keepdims=True)
    lg2 = jnp.where(lane == i1, -jnp.inf, lg)
    v2 = jnp.max(lg2, axis=-1, keepdims=True)
    i2 = jnp.min(jnp.where(lg2 == v2, lane, 128.0), axis=-1, keepdims=True)
    e2 = jnp.exp(v2 - v1)
    w1 = 1.0 / (1.0 + e2)
    w2 = e2 / (1.0 + e2)
    return jnp.where(lane == 0.0, i1,
                     jnp.where(lane == 1.0, i2, jnp.where(lane == 2.0, w1, jnp.where(lane == 3.0, w2, 0.0))))


def _pack_halves(y):
    w = y.shape[1] // 2
    lo = pltpu.bitcast(y[:, :w].astype(BF16).astype(F32), jnp.uint32)
    hi = pltpu.bitcast(y[:, w:].astype(BF16).astype(F32), jnp.uint32)
    return (hi & jnp.uint32(0xFFFF0000)) | (lo >> 16)


def _unpack_halves(p):
    lo = pltpu.bitcast(p << 16, F32)
    hi = pltpu.bitcast(p & jnp.uint32(0xFFFF0000), F32)
    return jnp.concatenate([lo, hi], axis=1)


def _out_kernel(hf_ref, hb_ref, o_ref, yf_ref, yc_ref, x_ref, mod_ref, hg_ref, g2_ref, wo_ref,
                *rest, with_router):
    if with_router:
        wr_ref, br_ref, tri_ref, xo_ref, hx_ref, route_ref, pos_ref, cnt_ref, carry_ref = rest
    else:
        xo_ref, hx_ref = rest
    hsum = hf_ref[0].astype(F32) + hb_ref[0].astype(F32)
    heads = [_rms(hsum[:, h * M_DH:(h + 1) * M_DH]) for h in range(M_H)]
    ym = jax.nn.sigmoid(o_ref[0].astype(F32)) * (jnp.concatenate(heads, axis=1) * hg_ref[...])
    proj = (_dot(ym.astype(BF16), wo_ref[0:M_W, :])
            + _dot(yf_ref[0], wo_ref[M_W:M_W + F_W, :])
            + _dot(yc_ref[0], wo_ref[M_W + F_W:, :]))
    xn = x_ref[0] + mod_ref[0, 2:3, :] * proj
    hx = (_rms(xn) * g2_ref[...]) * (1.0 + mod_ref[0, 4:5, :]) + mod_ref[0, 3:4, :]
    if with_router:
        route = _top2_route(_dot_precise(hx, wr_ref[...]) + br_ref[...])

        @pl.when((pl.program_id(0) == 0) & (pl.program_id(1) == 0))
        def _():
            carry_ref[...] = jnp.zeros_like(carry_ref)

        lane = lax.broadcasted_iota(jnp.int32, route.shape, 1).astype(F32)
        e1, e2 = route[:, 0:1], route[:, 1:2]
        hit = jnp.where((lane == e1) | (lane == e2), 1.0, 0.0)
        before = _dot(tri_ref[...], hit.astype(BF16)) + carry_ref[0:1, :]
        rank1 = jnp.sum(jnp.where(lane == e1, before, 0.0), axis=-1, keepdims=True)
        rank2 = jnp.sum(jnp.where(lane == e2, before, 0.0), axis=-1, keepdims=True)
        total = carry_ref[...] + jnp.sum(hit, axis=0, keepdims=True)
        xo_ref[0] = xn
        hx_ref[0] = _pack_halves(hx)
        route_ref[0] = route
        pos_ref[0] = jnp.where(lane == 0.0, rank1, jnp.where(lane == 1.0, rank2, 0.0)).astype(jnp.int32)
        carry_ref[...] = total
        cnt_ref[...] = total.astype(jnp.int32)
    else:
        xo_ref[0] = xn
        hx_ref[0] = hx.astype(BF16)


def _out_proj(hf, hb, qvo, yf, yc, x, modv, head_g, norm2_g, w_out, router=None):
    b, n, d = x.shape
    tm = min(ROW_TILE, n)
    row = lambda w: pl.BlockSpec((1, tm, w), lambda i, j: (i, j, 0))
    in_specs = [row(M_W), row(M_W),
                pl.BlockSpec((1, tm, M_W), lambda i, j: (i, j, 2)),
                row(F_W), row(C_W), row(d),
                pl.BlockSpec((1, 8, d), lambda i, j: (i, 0, 0)),
                _const_spec((1, M_W)), _const_spec((1, d)), _const_spec((d, d))]
    args = [hf, hb, qvo, yf, yc, x, modv, head_g.reshape(1, M_W), norm2_g.reshape(1, d), w_out]
    out_specs = [row(d), row(d)]
    out_shape = [jax.ShapeDtypeStruct((b, n, d), F32), jax.ShapeDtypeStruct((b, n, d), BF16)]
    scratch = []
    sem = ("parallel", "arbitrary")
    if router is not None:
        out_specs[1] = row(d // 2)
        out_shape[1] = jax.ShapeDtypeStruct((b, n, d // 2), jnp.uint32)
        in_specs += [_const_spec((d, 128)), _const_spec((1, 128)), _const_spec((tm, tm))]
        args += list(router) + [jnp.asarray(np.tril(np.ones((tm, tm), np.float32), -1), BF16)]
        out_specs += [row(128), row(128), _const_spec((8, 128))]
        out_shape += [jax.ShapeDtypeStruct((b, n, 128), F32), jax.ShapeDtypeStruct((b, n, 128), jnp.int32),
                      jax.ShapeDtypeStruct((8, 128), jnp.int32)]
        scratch = [pltpu.VMEM((8, 128), F32)]
        sem = ("arbitrary", "arbitrary")
    return pl.pallas_call(
        functools.partial(_out_kernel, with_router=router is not None),
        grid=(b, n // tm),
        in_specs=in_specs,
        out_specs=tuple(out_specs),
        out_shape=tuple(out_shape),
        scratch_shapes=scratch,
        compiler_params=_params(*sem),
        name="out_proj",
    )(*args)


def _swiglu_tile(hx, wgu_ref, wd_ref):
    acc = jnp.zeros((hx.shape[0], D_MODEL), F32)
    for j in range(FFN_HIDDEN // HID_TILE):
        cs = slice(j * HID_TILE, (j + 1) * HID_TILE)
        us = slice(FFN_HIDDEN + j * HID_TILE, FFN_HIDDEN + (j + 1) * HID_TILE)
        a = _silu(_dot(hx, wgu_ref[:, cs])) * _dot(hx, wgu_ref[:, us])
        acc = acc + _dot(a.astype(BF16), wd_ref[cs, :])
    return acc


def _ffn_kernel(hx_ref, x_ref, mod_ref, wgu_ref, wd_ref, o_ref):
    o_ref[0] = x_ref[0] + mod_ref[0, 5:6, :] * _swiglu_tile(hx_ref[0], wgu_ref, wd_ref)


def _ffn(hx, x, modv, wgu, wd):
    b, n, d = x.shape
    tm = min(ROW_TILE, n)
    row = lambda: pl.BlockSpec((1, tm, d), lambda i, j: (i, j, 0))
    return pl.pallas_call(
        _ffn_kernel,
        grid=(b, n // tm),
        in_specs=[row(), row(), pl.BlockSpec((1, 8, d), lambda i, j: (i, 0, 0)),
                  _const_spec(wgu.shape), _const_spec(wd.shape)],
        out_specs=row(),
        out_shape=jax.ShapeDtypeStruct((b, n, d), F32),
        compiler_params=_params("parallel", "arbitrary"),
        name="ffn",
    )(hx, x, modv, wgu, wd)


def _sc_worker_rows(n_rows):
    assert n_rows % (SC_WORKERS * SC_CHUNK) == 0
    return n_rows // SC_WORKERS


def _sc_scatter_rows(x, dest, n_out):
    t, w = x.shape
    per_w = _sc_worker_rows(t)
    mesh = plsc.VectorSubcoreMesh(core_axis_name="c", subcore_axis_name="s")

    @functools.partial(
        pl.kernel, mesh=mesh,
        out_type=jax.ShapeDtypeStruct((n_out, w), x.dtype),
        scratch_types=[pltpu.VMEM((1, SC_CHUNK), jnp.int32),
                       pltpu.VMEM((SC_CHUNK, w), x.dtype),
                       pltpu.SemaphoreType.DMA],
    )
    def scatter(x_hbm, dest_hbm, out_hbm, idx_v, rows_v, sem):
        base = (lax.axis_index("s") * SC_CORES + lax.axis_index("c")) * per_w

        @pl.loop(0, per_w // SC_CHUNK)
        def _(g):
            off = base + g * SC_CHUNK
            pltpu.sync_copy(x_hbm.at[pl.ds(off, SC_CHUNK)], rows_v)
            for k in range(2):
                pltpu.sync_copy(dest_hbm.at[pl.ds(k, 1), pl.ds(off, SC_CHUNK)], idx_v)
                pltpu.async_copy(rows_v, out_hbm.at[idx_v.at[0]], sem).wait()

    return scatter(x, dest)


def _sc_gather_rows(table, idx):
    r = idx.shape[0]
    w = table.shape[1]
    per_w = _sc_worker_rows(r)
    mesh = plsc.VectorSubcoreMesh(core_axis_name="c", subcore_axis_name="s")

    @functools.partial(
        pl.kernel, mesh=mesh,
        out_type=jax.ShapeDtypeStruct((r, w), table.dtype),
        scratch_types=[pltpu.VMEM((1, SC_CHUNK), jnp.int32),
                       pltpu.VMEM((SC_CHUNK, w), table.dtype),
                       pltpu.SemaphoreType.DMA],
    )
    def gather(table_hbm, idx_hbm, out_hbm, idx_v, rows_v, sem):
        base = (lax.axis_index("s") * SC_CORES + lax.axis_index("c")) * per_w

        @pl.loop(0, per_w // SC_CHUNK)
        def _(g):
            off = base + g * SC_CHUNK
            pltpu.sync_copy(idx_hbm.at[pl.ds(0, 1), pl.ds(off, SC_CHUNK)], idx_v)
            pltpu.async_copy(table_hbm.at[idx_v.at[0]], rows_v, sem).wait()
            pltpu.sync_copy(rows_v, out_hbm.at[pl.ds(off, SC_CHUNK)])

    return gather(table, idx.reshape(1, r))


def _group_kernel(te_ref, nv_ref, xs_ref, wgu_ref, wd_ref, ys_ref):
    @pl.when(pl.program_id(0) < nv_ref[0])
    def _():
        hx = _unpack_halves(xs_ref[...]).astype(BF16)
        ys_ref[...] = _pack_halves(_swiglu_tile(hx, wgu_ref.at[0], wd_ref.at[0]))


def _grouped_swiglu(xs, tile_expert, n_valid, wgu, wd):
    r, wp = xs.shape
    grid_spec = pltpu.PrefetchScalarGridSpec(
        num_scalar_prefetch=2,
        grid=(r // GROUP_TILE,),
        in_specs=[pl.BlockSpec((GROUP_TILE, wp), lambda i, te, nv: (i, 0)),
                  pl.BlockSpec((1,) + wgu.shape[1:], lambda i, te, nv: (te[i], 0, 0)),
                  pl.BlockSpec((1,) + wd.shape[1:], lambda i, te, nv: (te[i], 0, 0))],
        out_specs=pl.BlockSpec((GROUP_TILE, wp), lambda i, te, nv: (i, 0)),
    )
    return pl.pallas_call(
        _group_kernel,
        grid_spec=grid_spec,
        out_shape=jax.ShapeDtypeStruct((r, wp), jnp.uint32),
        compiler_params=_params("arbitrary"),
        name="grouped_swiglu",
    )(tile_expert, n_valid, xs, wgu, wd)


def _combine_kernel(x_ref, y_ref, route_ref, mod_ref, fg_ref, o_ref):
    r = route_ref[0]
    moe = r[:, 2:3] * _unpack_halves(y_ref[0, 0]) + r[:, 3:4] * _unpack_halves(y_ref[1, 0])
    xn = x_ref[0] + mod_ref[0, 5:6, :] * moe
    o_ref[0] = _rms(xn) * fg_ref[...]


def _combine_final(x, y2, route, modv, final_g):
    b, n, d = x.shape
    tm = min(ROW_TILE, n)
    row = lambda w: pl.BlockSpec((1, tm, w), lambda i, j: (i, j, 0))
    return pl.pallas_call(
        _combine_kernel,
        grid=(b, n // tm),
        in_specs=[row(d), pl.BlockSpec((2, 1, tm, d // 2), lambda i, j: (0, i, j, 0)), row(128),
                  pl.BlockSpec((1, 8, d), lambda i, j: (i, 0, 0)), _const_spec((1, d))],
        out_specs=row(d),
        out_shape=jax.ShapeDtypeStruct((b, n, d), F32),
        compiler_params=_params("parallel", "arbitrary"),
        name="moe_combine",
    )(x, y2, route, modv, final_g.reshape(1, d))


def _moe_final(hxp, x, route, pos, counts, modv, final_g, wgu, wd):
    b, n, d = x.shape
    t = b * n
    counts = counts[0, :N_EXPERTS]
    padded = ((counts + GROUP_TILE - 1) // GROUP_TILE) * GROUP_TILE
    ends = jnp.cumsum(padded)
    starts = ends - padded
    experts = route.reshape(t, 128)[:, :2].astype(jnp.int32)
    dest = (starts[experts] + pos.reshape(t, 128)[:, :2]).T
    n_rows = 2 * t + N_EXPERTS * GROUP_TILE
    tile_start = jnp.arange(n_rows // GROUP_TILE, dtype=jnp.int32) * GROUP_TILE
    tile_expert = jnp.minimum(jnp.sum(ends[None, :] <= tile_start[:, None], axis=1), N_EXPERTS - 1).astype(jnp.int32)
    n_valid = (ends[-1:] // GROUP_TILE).astype(jnp.int32)

    xs = _sc_scatter_rows(hxp.reshape(t, d // 2), dest, n_rows)
    ys = _grouped_swiglu(xs, tile_expert, n_valid, wgu, wd)
    y2 = _sc_gather_rows(ys, dest.reshape(2 * t))
    return _combine_final(x, y2.reshape(2, b, n, d // 2), route, modv, final_g)


def _in_weights(w):
    wm = jnp.concatenate([w[:, O_Q:O_K], w[:, O_V:O_G], w[:, O_F:P_IN], w[:, O_G:O_F],
                          jnp.zeros((w.shape[0], 128 - 4 * M_H), w.dtype)], axis=1).astype(BF16)
    return wm, w[:, O_K:O_V].T.astype(BF16), w[:, O_G:O_F].T.astype(BF16)


def _zero_state(b):
    return (jnp.zeros((b, M_H, M_DH, 2 * M_DH), F32), jnp.full((b, M_H, 8, 128), -1e30, F32))


def _mod_rows(m):
    r = m.shape[0]
    m = m.reshape(r, 6, D_MODEL)
    return jnp.concatenate([m, jnp.zeros((r, 2, D_MODEL), F32)], axis=1)


def kernel(x, c, ctx, c_ctx, norm1_g, norm2_g, w_mod, b_mod, w_in, b_gate, mlstm_norm_g, conv_dw, conv_db, conv_norm_g, conv_norm_b, w_out, ffn_w_gate_up, ffn_w_down, moe_w_router, moe_b_router, moe_w_gate_up, moe_w_down, final_norm_g):
    b = x.shape[0]
    depth = w_in.shape[0]
    assert depth == 2 and b <= 7
    cond8 = jnp.concatenate([c, c_ctx[None, :], jnp.zeros((7 - b, D_MODEL), F32)], axis=0)
    xc = ctx
    for layer in range(depth):
        last = layer == depth - 1
        mods = _modulation(cond8, w_mod[layer], b_mod[layer])
        mod_lat = _mod_rows(mods[:b])
        mod_ctx = jnp.broadcast_to(_mod_rows(mods[b:b + 1]), (b, 8, D_MODEL))
        wm, wkt, wgt = _in_weights(w_in[layer])
        bg = b_gate[layer].reshape(16, 1)
        w_o = w_out[layer].astype(BF16)

        qvo_c, kt_c, zf_c, cc_c, gt_c, gc_c = _in_proj(xc, mod_ctx, norm1_g[layer], wm, wkt, wgt, bg)
        qvo_l, kt_l, zf_l, cc_l, gt_l, gc_l = _in_proj(x, mod_lat, norm1_g[layer], wm, wkt, wgt, bg)

        hf_c, hb_c, cf, cb, mf, mb = _mlstm(qvo_c, kt_c, gt_c, gc_c, _zero_state(b), _zero_state(b))
        hf_l, hb_l, _, _, _, _ = _mlstm(qvo_l, kt_l, gt_l, gc_l, (cf, mf), (cb, mb))

        conv_args = (conv_dw[layer], conv_db[layer], conv_norm_g[layer], conv_norm_b[layer])
        yf_l = _fourier(zf_l, x.shape[1])
        yc_l = _conv(cc_l, *conv_args)
        if not last:
            x, hx = _out_proj(hf_l, hb_l, qvo_l, yf_l, yc_l, x, mod_lat, mlstm_norm_g[layer],
                              norm2_g[layer], w_o)
            yf_c = _fourier(zf_c, xc.shape[1])
            yc_c = _conv(cc_c, *conv_args)
            xc, hc = _out_proj(hf_c, hb_c, qvo_c, yf_c, yc_c, xc, mod_ctx, mlstm_norm_g[layer],
                               norm2_g[layer], w_o)
            wgu = ffn_w_gate_up[layer // 2].astype(BF16)
            wd = ffn_w_down[layer // 2].astype(BF16)
            x = _ffn(hx, x, mod_lat, wgu, wd)
            xc = _ffn(hc, xc, mod_ctx, wgu, wd)
        else:
            idx = layer // 2
            wr = jnp.concatenate([moe_w_router[idx], jnp.zeros((D_MODEL, 128 - N_EXPERTS), F32)], axis=1)
            br = jnp.concatenate([moe_b_router[idx], jnp.zeros((128 - N_EXPERTS,), F32)]).reshape(1, 128)
            x, hxp, route, pos, counts = _out_proj(hf_l, hb_l, qvo_l, yf_l, yc_l, x, mod_lat, mlstm_norm_g[layer],
                                                   norm2_g[layer], w_o, router=(wr, br))
            x = _moe_final(hxp, x, route, pos, counts, mod_lat, final_norm_g,
                           moe_w_gate_up[idx].astype(BF16), moe_w_down[idx].astype(BF16))
    return x
```

```python
import functools
import math

import numpy as np
import jax
import jax.numpy as jnp
from jax import lax
from jax.experimental import pallas as pl
from jax.experimental.pallas import tpu as pltpu
from jax.experimental.pallas import tpu_sc as plsc

F32 = jnp.float32
BF16 = jnp.bfloat16

D_MODEL = 1024
M_W = 512
M_H = 4
M_DH = 128
F_W = 256
F_G = 4
C_W = 256
C_G = 4
CONV_K = 31
FFN_HIDDEN = 2816
N_EXPERTS = 8
EPS = 1e-6
LOG2E = 1.4426950408889634

O_Q = 0
O_K = O_Q + M_W
O_V = O_K + M_W
O_O = O_V + M_W
O_G = O_O + M_W
O_F = O_G + 4 * M_H
O_CU = O_F + F_W
O_CG = O_CU + C_W
P_IN = O_CG + C_W

LC = 256
DFT_N1 = 128
DFT_DIRECT_MAX = 512
DFT_GROUP = 16
ROW_TILE = 512
IN_TILE = 512
HID_TILE = 256
CONV_TILE = 256
CONV_PAD = 16
GROUP_TILE = 512
SC_CORES = 2
SC_SUBCORES = 16
SC_WORKERS = SC_CORES * SC_SUBCORES
SC_CHUNK = 128
VMEM_LIMIT = 56 * 1024 * 1024


def _params(*sem):
    return pltpu.CompilerParams(dimension_semantics=sem, vmem_limit_bytes=VMEM_LIMIT)


def _const_spec(shape):
    zeros = (0,) * len(shape)
    return pl.BlockSpec(shape, lambda *_: zeros, pipeline_mode=pl.Buffered(1))


def _dot(a, b):
    return jnp.dot(a, b, preferred_element_type=F32)


def _dot_nt(a, b):
    return lax.dot_general(a, b, (((1,), (1,)), ((), ())), preferred_element_type=F32)


def _split2(a):
    hi = a.astype(BF16)
    lo = (a - hi.astype(F32)).astype(BF16)
    return hi, lo


def _split3(a):
    x1 = a.astype(BF16)
    r1 = a - x1.astype(F32)
    x2 = r1.astype(BF16)
    x3 = (r1 - x2.astype(F32)).astype(BF16)
    return x1, x2, x3


def _dot_precise(a, b):
    ah, al = _split2(a)
    bh, bl = _split2(b)
    return _dot(ah, bh) + _dot(al, bh) + _dot(ah, bl)


def _silu(x):
    return x * jax.nn.sigmoid(x)


def _log_sigmoid(x):
    return -(jnp.maximum(-x, 0.0) + jnp.log1p(jnp.exp(-jnp.abs(x))))


def _rms(x):
    return x * lax.rsqrt(jnp.mean(x * x, axis=-1, keepdims=True) + EPS)


def _mod_kernel(c_ref, w_ref, b_ref, o_ref):
    o_ref[...] = _dot_precise(_silu(c_ref[...]), w_ref[...]) + b_ref[...]


def _modulation(cond8, w_mod, b_mod):
    d = cond8.shape[1]
    n_out = w_mod.shape[1]
    tn = 1536
    return pl.pallas_call(
        _mod_kernel,
        grid=(n_out // tn,),
        in_specs=[_const_spec((8, d)),
                  pl.BlockSpec((d, tn), lambda j: (0, j)),
                  pl.BlockSpec((1, tn), lambda j: (0, j))],
        out_specs=pl.BlockSpec((8, tn), lambda j: (0, j)),
        out_shape=jax.ShapeDtypeStruct((8, n_out), F32),
        compiler_params=_params("arbitrary"),
        name="modulation",
    )(cond8, w_mod, b_mod.reshape(1, n_out))


def _in_kernel(x_ref, mod_ref, g_ref, wm_ref, wkt_ref, wgt_ref, bg_ref, bgr_ref,
               qvo_ref, kt_ref, zf_ref, cc_ref, gt_ref, gc_ref, *scratch):
    tm = x_ref.shape[1]
    h = _rms(x_ref[0]) * g_ref[...]
    h = h * (1.0 + mod_ref[0, 1:2, :]) + mod_ref[0, 0:1, :]
    hb = h.astype(BF16)
    p = _dot(hb, wm_ref[...])
    kt = _dot_nt(wkt_ref[...], hb) * (M_DH ** -0.5)
    gt = _dot_nt(wgt_ref[...], hb) + bg_ref[...]
    o_f, o_c, o_g = 3 * M_W, 3 * M_W + F_W, 3 * M_W + F_W + 2 * C_W
    qvo_ref[0] = p[:, :o_f].astype(BF16)
    if scratch:
        zs_ref, = scratch
        rows = zf_ref.shape[1]
        n2cnt = tm // rows
        for half in range(F_W // 128):
            zs_ref[half] = p[:, o_f + half * 128:o_f + (half + 1) * 128]
        for j in range(n2cnt):
            for half in range(F_W // 128):
                lanes = slice(j * F_W + half * 128, j * F_W + (half + 1) * 128)
                zf_ref[0, :, lanes] = zs_ref[half, pl.ds(j, rows, stride=n2cnt), :]
    else:
        zf_ref[0] = p[:, o_f:o_c].astype(BF16)
    cc_ref[0] = p[:, o_c:o_g].astype(BF16)
    kt_ref[0] = kt.astype(BF16)
    gc_ref[0] = p[:, o_g:] + bgr_ref[...]
    for j in range(tm // LC):
        gt_ref[0, j] = gt[:, j * LC:(j + 1) * LC]


def _in_proj(x, modv, norm_g, wm, wkt, wgt, bg):
    b, n, d = x.shape
    tm = min(IN_TILE, n)
    nm = wm.shape[1]
    bg_row = jnp.concatenate([bg.reshape(1, 16), jnp.zeros((1, 128 - 16), F32)], axis=1)
    if n > DFT_DIRECT_MAX:
        n2cnt = n // DFT_N1
        assert tm % n2cnt == 0 and (tm // n2cnt) % 8 == 0
        zf_sds = jax.ShapeDtypeStruct((b, DFT_N1, n2cnt * F_W), F32)
        zf_spec = pl.BlockSpec((1, tm // n2cnt, n2cnt * F_W), lambda i, j: (i, j, 0))
        scratch = [pltpu.VMEM((F_W // 128, tm, 128), F32)]
    else:
        zf_sds = jax.ShapeDtypeStruct((b, n, F_W), BF16)
        zf_spec = pl.BlockSpec((1, tm, F_W), lambda i, j: (i, j, 0))
        scratch = []
    out_shape = (
        jax.ShapeDtypeStruct((b, n, 3 * M_W), BF16),
        jax.ShapeDtypeStruct((b, M_W, n), BF16),
        zf_sds,
        jax.ShapeDtypeStruct((b, n, 2 * C_W), BF16),
        jax.ShapeDtypeStruct((b, n // LC, 16, LC), F32),
        jax.ShapeDtypeStruct((b, n, 128), F32),
    )
    return pl.pallas_call(
        _in_kernel,
        grid=(b, n // tm),
        in_specs=[pl.BlockSpec((1, tm, d), lambda i, j: (i, j, 0)),
                  pl.BlockSpec((1, 8, d), lambda i, j: (i, 0, 0)),
                  _const_spec((1, d)),
                  _const_spec((d, nm)),
                  _const_spec((M_W, d)),
                  _const_spec((16, d)),
                  _const_spec((16, 1)),
                  _const_spec((1, 128))],
        out_specs=(pl.BlockSpec((1, tm, 3 * M_W), lambda i, j: (i, j, 0)),
                   pl.BlockSpec((1, M_W, tm), lambda i, j: (i, 0, j)),
                   zf_spec,
                   pl.BlockSpec((1, tm, 2 * C_W), lambda i, j: (i, j, 0)),
                   pl.BlockSpec((1, tm // LC, 16, LC), lambda i, j: (i, j, 0, 0)),
                   pl.BlockSpec((1, tm, 128), lambda i, j: (i, j, 0))),
        out_shape=out_shape,
        scratch_shapes=scratch,
        compiler_params=_params("parallel", "arbitrary"),
        name="in_proj",
    )(x, modv, norm_g.reshape(1, d), wm, wkt, wgt, bg, bg_row)


def _mlstm_kernel(qf_ref, vf_ref, ktf_ref, gtf_ref, gcf_ref, qb_ref, vb_ref, ktb_ref, gtb_ref, gcb_ref,
                  cf0_ref, cb0_ref, mf0_ref, mb0_ref, u3_ref, lo3_ref, ut3_ref, lot3_ref,
                  hf_ref, hb_ref, cf_ref, cb_ref, mf_ref, mb_ref):
    @pl.when(pl.program_id(1) == 0)
    def _():
        cf_ref[...] = cf0_ref[...]
        cb_ref[...] = cb0_ref[...]
        mf_ref[...] = mf0_ref[...]
        mb_ref[...] = mb0_ref[...]

    gf = gtf_ref[0, 0]
    gb = gtb_ref[0, 0]
    rows_f = _dot(jnp.concatenate(_split3(_log_sigmoid(gf)), axis=1), u3_ref[...])
    rows_b = _dot(jnp.concatenate(_split3(_log_sigmoid(gb)), axis=1), lo3_ref[...])
    cols_f = _dot(ut3_ref[...], jnp.concatenate(_split3(_log_sigmoid(gcf_ref[0])), axis=0))
    cols_b = _dot(lot3_ref[...], jnp.concatenate(_split3(_log_sigmoid(gcb_ref[0])), axis=0))

    t_idx = lax.broadcasted_iota(jnp.int32, (LC, LC), 0)
    s_idx = lax.broadcasted_iota(jnp.int32, (LC, LC), 1)
    ones = jnp.ones((LC, M_DH), BF16)

    dirs = (
        (qf_ref, vf_ref, ktf_ref, gf, rows_f, cols_f, s_idx <= t_idx, cf_ref, mf_ref, hf_ref, 0, 4, LC - 1),
        (qb_ref, vb_ref, ktb_ref, gb, rows_b, cols_b, s_idx >= t_idx, cb_ref, mb_ref, hb_ref, 8, 12, 0),
    )
    outs = []
    for q_ref, v_ref, kt_ref, g, rows, cols, mask, c_ref, m_ref, h_ref, i_off, f_off, last in dirs:
        for h in range(M_H):
            hs = slice(h * M_DH, (h + 1) * M_DH)
            q = q_ref[0, :, hs]
            kt = kt_ref[0, hs, :]
            vaug = jnp.concatenate([v_ref[0, :, hs], ones], axis=1)
            i_row = g[i_off + h:i_off + h + 1, :]
            b_row = rows[f_off + h:f_off + h + 1, :]
            b_col = jnp.broadcast_to(cols[:, f_off + h:f_off + h + 1], (LC, M_DH))
            b_last = b_row[:, last:last + 1]
            c0 = c_ref[0, h]
            m0 = m_ref[0, h, 0:1, 0:1]

            am = jnp.where(mask, (i_row - b_row) * LOG2E, -jnp.inf)
            g2 = jnp.maximum(m0 * LOG2E, jnp.max(am, axis=-1, keepdims=True))
            g2 = jnp.broadcast_to(g2, (LC, M_DH))
            e = jnp.exp2(am - jnp.concatenate([g2, g2], axis=1))
            s = (_dot(q, kt) * e).astype(BF16)
            qi = (q.astype(F32) * jnp.exp2(m0 * LOG2E - g2)).astype(BF16)
            na = _dot(s, vaug) + _dot(qi, c0.astype(BF16))
            floor = jnp.exp2(-(b_col * LOG2E + g2))
            h_new = (na[:, :M_DH] / jnp.maximum(jnp.abs(na[:, M_DH:]), floor)).astype(h_ref.dtype)

            gs = b_last - b_row + i_row
            m_new = jnp.maximum(b_last + m0, jnp.max(gs, axis=-1, keepdims=True))
            kw = (kt.astype(F32) * jnp.exp(gs - m_new)).astype(BF16)
            c_new = jnp.exp(b_last + m0 - m_new) * c0 + _dot(kw, vaug)
            outs.append((h_ref, c_ref, m_ref, h, hs, h_new, c_new, m_new))

    for h_ref, c_ref, m_ref, h, hs, h_new, c_new, m_new in outs:
        h_ref[0, :, hs] = h_new
        c_ref[0, h] = c_new
        m_ref[0, h] = jnp.broadcast_to(m_new, (8, 128))


def _tri_constants():
    s = np.arange(LC)[:, None]
    t = np.arange(LC)[None, :]
    u = (s <= t).astype(np.float32)
    lo = (s >= t).astype(np.float32)
    u3 = np.concatenate([u, u, u], axis=0)
    lo3 = np.concatenate([lo, lo, lo], axis=0)
    return (jnp.asarray(u3, BF16), jnp.asarray(lo3, BF16),
            jnp.asarray(u3.T.copy(), BF16), jnp.asarray(lo3.T.copy(), BF16))


def _mlstm(qvo, kt, gt, gc, state_f, state_b):
    b, n, _ = qvo.shape
    nc = n // LC
    cf0, mf0 = state_f
    cb0, mb0 = state_b
    fwd = lambda i, c: (i, c, 0)
    bwd = lambda i, c: (i, nc - 1 - c, 0)
    st_c = pl.BlockSpec((1, M_H, M_DH, 2 * M_DH), lambda i, c: (i, 0, 0, 0))
    st_m = pl.BlockSpec((1, M_H, 8, 128), lambda i, c: (i, 0, 0, 0))
    h_sds = jax.ShapeDtypeStruct((b, n, M_W), BF16)
    return pl.pallas_call(
        _mlstm_kernel,
        grid=(b, nc),
        in_specs=[pl.BlockSpec((1, LC, M_W), fwd),
                  pl.BlockSpec((1, LC, M_W), lambda i, c: (i, c, 1)),
                  pl.BlockSpec((1, M_W, LC), lambda i, c: (i, 0, c)),
                  pl.BlockSpec((1, 1, 16, LC), lambda i, c: (i, c, 0, 0)),
                  pl.BlockSpec((1, LC, 128), fwd),
                  pl.BlockSpec((1, LC, M_W), bwd),
                  pl.BlockSpec((1, LC, M_W), lambda i, c: (i, nc - 1 - c, 1)),
                  pl.BlockSpec((1, M_W, LC), lambda i, c: (i, 0, nc - 1 - c)),
                  pl.BlockSpec((1, 1, 16, LC), lambda i, c: (i, nc - 1 - c, 0, 0)),
                  pl.BlockSpec((1, LC, 128), bwd),
                  st_c, st_c, st_m, st_m,
                  _const_spec((3 * LC, LC)), _const_spec((3 * LC, LC)),
                  _const_spec((LC, 3 * LC)), _const_spec((LC, 3 * LC))],
        out_specs=(pl.BlockSpec((1, LC, M_W), fwd), pl.BlockSpec((1, LC, M_W), bwd),
                   st_c, st_c, st_m, st_m),
        out_shape=(h_sds, h_sds,
                   jax.ShapeDtypeStruct(cf0.shape, F32), jax.ShapeDtypeStruct(cb0.shape, F32),
                   jax.ShapeDtypeStruct(mf0.shape, F32), jax.ShapeDtypeStruct(mb0.shape, F32)),
        compiler_params=_params("parallel", "arbitrary"),
        name="mlstm",
    )(qvo, qvo, kt, gt, gc, qvo, qvo, kt, gt, gc, cf0, cb0, mf0, mb0, *_tri_constants())


def _channel_dft_matrix():
    gw = F_W // F_G
    j = np.arange(gw)[:, None]
    l = np.arange(gw)[None, :]
    ang = 2.0 * np.pi * j * l / gw
    eye = np.eye(F_G)
    bc = np.kron(eye, np.cos(ang)) / math.sqrt(gw)
    bs = np.kron(eye, np.sin(ang)) / math.sqrt(gw)
    return jnp.asarray(np.concatenate([bc, -bs], axis=1), BF16)


def _dft_cos_sin(n):
    k = np.arange(n)[:, None]
    m = np.arange(n)[None, :]
    ang = 2.0 * np.pi * ((k * m) % n) / n
    return np.cos(ang) / math.sqrt(n), np.sin(ang) / math.sqrt(n)


def _fourier1_kernel(z_ref, bcs_ref, f1_ref, o_ref, ts_ref):
    n2cnt = o_ref.shape[1] // 2

    def group(g, carry):
        for r in range(DFT_GROUP):
            j = g * DFT_GROUP + r
            off = pl.multiple_of(j * F_W, F_W)
            ab = _dot(z_ref[0, :, pl.ds(off, F_W)].astype(BF16), bcs_ref[...])
            u = jnp.concatenate([ab[:, :F_W], ab[:, F_W:]], axis=0).astype(BF16)
            t = _dot(f1_ref[j], u)
            parts = (t[:DFT_N1], t[DFT_N1:])
            for part in range(2):
                for half in range(F_W // 128):
                    ts_ref[2 * part + half, r * DFT_N1:(r + 1) * DFT_N1, :] = parts[part][:, half * 128:(half + 1) * 128]
        for part in range(2):
            row0 = pl.multiple_of(part * n2cnt + g * DFT_GROUP, DFT_GROUP)
            for k1 in range(DFT_N1):
                for half in range(F_W // 128):
                    tile = ts_ref[2 * part + half, pl.ds(k1, DFT_GROUP, stride=DFT_N1), :]
                    lanes = slice(k1 * F_W + half * 128, k1 * F_W + (half + 1) * 128)
                    o_ref[0, pl.ds(row0, DFT_GROUP), lanes] = tile.astype(BF16)
        return carry

    lax.fori_loop(0, n2cnt // DFT_GROUP, group, 0)


def _fourier2_kernel(t_ref, f2_ref, o_ref, s_ref):
    n2cnt = f2_ref.shape[0]
    width = t_ref.shape[2]
    tw = min(4096, width)
    for c in range(width // tw):
        y = _dot(f2_ref[...], t_ref[0, :, c * tw:(c + 1) * tw])
        for kk in range(tw // F_W):
            k1 = c * (tw // F_W) + kk
            for half in range(F_W // 128):
                lanes = slice(kk * F_W + half * 128, kk * F_W + (half + 1) * 128)
                s_ref[half, k1 * n2cnt:(k1 + 1) * n2cnt, :] = y[:, lanes]
    for k2 in range(n2cnt):
        for a in range(DFT_N1 // DFT_GROUP):
            for half in range(F_W // 128):
                rows = s_ref[half, pl.ds(DFT_GROUP * a * n2cnt + k2, DFT_GROUP, stride=n2cnt), :]
                r0 = k2 * DFT_N1 + DFT_GROUP * a
                o_ref[0, r0:r0 + DFT_GROUP, half * 128:(half + 1) * 128] = rows.astype(o_ref.dtype)


def _fourier_direct_kernel(z_ref, bcs_ref, f_ref, o_ref):
    ab = _dot(z_ref[0], bcs_ref[...])
    u = jnp.concatenate([ab[:, :F_W], ab[:, F_W:]], axis=0).astype(BF16)
    o_ref[0] = _dot(f_ref[...], u).astype(o_ref.dtype)


def _fourier(zf, n):
    b = zf.shape[0]
    bcs = _channel_dft_matrix()
    if n <= DFT_DIRECT_MAX:
        c, s = _dft_cos_sin(n)
        fmat = jnp.asarray(np.concatenate([c, s], axis=1), BF16)
        return pl.pallas_call(
            _fourier_direct_kernel,
            grid=(b,),
            in_specs=[pl.BlockSpec((1, n, F_W), lambda i: (i, 0, 0)),
                      _const_spec((F_W, 2 * F_W)), _const_spec((n, 2 * n))],
            out_specs=pl.BlockSpec((1, n, F_W), lambda i: (i, 0, 0)),
            out_shape=jax.ShapeDtypeStruct((b, n, F_W), BF16),
            compiler_params=_params("parallel"),
            name="fourier_direct",
        )(zf, bcs, fmat)

    n1 = DFT_N1
    n2 = n // n1
    assert n2 % DFT_GROUP == 0
    jj = np.arange(n2)[:, None, None]
    k1 = np.arange(n1)[None, :, None]
    aa = np.arange(n1)[None, None, :]
    ang = 2.0 * np.pi * (((jj * k1) % n) / n + ((k1 * aa) % n1) / n1)
    c1, s1 = np.cos(ang) / math.sqrt(n1), np.sin(ang) / math.sqrt(n1)
    f1 = jnp.asarray(np.concatenate([np.concatenate([c1, s1], axis=2),
                                     np.concatenate([-s1, c1], axis=2)], axis=1), BF16)
    c2, s2 = _dft_cos_sin(n2)
    f2 = jnp.asarray(np.concatenate([c2, s2], axis=1), BF16)
    width = n1 * F_W

    t = pl.pallas_call(
        _fourier1_kernel,
        grid=(b,),
        in_specs=[pl.BlockSpec((1, n1, n2 * F_W), lambda i: (i, 0, 0)),
                  _const_spec((F_W, 2 * F_W)), _const_spec((n2, 2 * n1, 2 * n1))],
        out_specs=pl.BlockSpec((1, 2 * n2, width), lambda i: (i, 0, 0)),
        out_shape=jax.ShapeDtypeStruct((b, 2 * n2, width), BF16),
        scratch_shapes=[pltpu.VMEM((2 * (F_W // 128), DFT_GROUP * n1, 128), F32)],
        compiler_params=_params("parallel"),
        name="fourier_stage1",
    )(zf, bcs, f1)
    return pl.pallas_call(
        _fourier2_kernel,
        grid=(b,),
        in_specs=[pl.BlockSpec((1, 2 * n2, width), lambda i: (i, 0, 0)),
                  _const_spec((n2, 2 * n2))],
        out_specs=pl.BlockSpec((1, n, F_W), lambda i: (i, 0, 0)),
        out_shape=jax.ShapeDtypeStruct((b, n, F_W), BF16),
        scratch_shapes=[pltpu.VMEM((F_W // 128, n, 128), F32)],
        compiler_params=_params("parallel"),
        name="fourier_stage2",
    )(t, f2)


def _group_mean(y, gm_ref):
    hi, lo = _split2(y)
    return _dot(hi, gm_ref[...]) + _dot(lo, gm_ref[...])


def _conv_kernel(cc_ref, dw_ref, db_ref, lg_ref, lb_ref, gm_ref, o_ref, a_ref, ph_ref):
    n = cc_ref.shape[1]
    rt = min(CONV_TILE, n)
    span = rt + 2 * CONV_PAD - 8
    a_ref[0:CONV_PAD, :] = jnp.zeros((CONV_PAD, C_W), F32)
    a_ref[n + CONV_PAD:n + 2 * CONV_PAD, :] = jnp.zeros((CONV_PAD, C_W), F32)

    def glu(i, carry):
        r = pl.multiple_of(i * rt, rt)
        blk = cc_ref[0, pl.ds(r, rt), :].astype(F32)
        a_ref[pl.ds(r + CONV_PAD, rt), :] = blk[:, :C_W] * jax.nn.sigmoid(blk[:, C_W:])
        return carry

    lax.fori_loop(0, n // rt, glu, 0)

    def conv(i, carry):
        r = pl.multiple_of(i * rt, rt)
        win = a_ref[pl.ds(r, rt + 2 * CONV_PAD), :]
        for s in range(1, 8):
            ph_ref[s - 1] = win[s:s + span]
        acc = jnp.broadcast_to(db_ref[...], (rt, C_W))
        for j in range(CONV_K):
            off = CONV_PAD - CONV_K // 2 + j
            base = 8 * (off // 8)
            if off % 8 == 0:
                tap = a_ref[pl.ds(r + base, rt), :]
            else:
                tap = ph_ref[off % 8 - 1, base:base + rt, :]
            acc = acc + tap * dw_ref[j:j + 1, :]
        d = acc - _group_mean(acc, gm_ref)
        var = _group_mean(d * d, gm_ref)
        yn = d * lax.rsqrt(var + EPS) * lg_ref[...] + lb_ref[...]
        o_ref[0, pl.ds(r, rt), :] = _silu(yn).astype(o_ref.dtype)
        return carry

    lax.fori_loop(0, n // rt, conv, 0)


def _conv(cc, dw, db, ln_g, ln_b):
    b, n, _ = cc.shape
    gw = C_W // C_G
    gm = jnp.asarray(np.kron(np.eye(C_G), np.full((gw, gw), 1.0 / gw)), BF16)
    dw32 = jnp.concatenate([dw, jnp.zeros((32 - CONV_K, C_W), F32)], axis=0)
    return pl.pallas_call(
        _conv_kernel,
        grid=(b,),
        in_specs=[pl.BlockSpec((1, n, 2 * C_W), lambda i: (i, 0, 0)),
                  _const_spec((32, C_W)), _const_spec((1, C_W)), _const_spec((1, C_W)),
                  _const_spec((1, C_W)), _const_spec((C_W, C_W))],
        out_specs=pl.BlockSpec((1, n, C_W), lambda i: (i, 0, 0)),
        out_shape=jax.ShapeDtypeStruct((b, n, C_W), BF16),
        scratch_shapes=[pltpu.VMEM((n + 2 * CONV_PAD, C_W), F32),
                        pltpu.VMEM((7, min(CONV_TILE, n) + 2 * CONV_PAD - 8, C_W), F32)],
        compiler_params=_params("parallel"),
        name="conv_module",
    )(cc, dw32, db.reshape(1, C_W), ln_g.reshape(1, C_W), ln_b.reshape(1, C_W), gm)


def _top2_route(logits):
    lane = lax.broadcasted_iota(jnp.int32, logits.shape, 1).astype(F32)
    lg = jnp.where(lane < N_EXPERTS, logits, -jnp.inf)
    v1 = jnp.max(lg, axis=-1, keepdims=True)
    i1 = jnp.min(jnp.where(lg == v1, lane, 128.0), axis=-1, keepdims=True)
    lg2 = jnp.where(lane == i1, -jnp.inf, lg)
    v2 = jnp.max(lg2, axis=-1, keepdims=True)
    i2 = jnp.min(jnp.where(lg2 == v2, lane, 128.0), axis=-1, keepdims=True)
    e2 = jnp.exp(v2 - v1)
    w1 = 1.0 / (1.0 + e2)
    w2 = e2 / (1.0 + e2)
    return jnp.where(lane == 0.0, i1,
                     jnp.where(lane == 1.0, i2, jnp.where(lane == 2.0, w1, jnp.where(lane == 3.0, w2, 0.0))))


def _pack_halves(y):
    w = y.shape[1] // 2
    lo = pltpu.bitcast(y[:, :w].astype(BF16).astype(F32), jnp.uint32)
    hi = pltpu.bitcast(y[:, w:].astype(BF16).astype(F32), jnp.uint32)
    return (hi & jnp.uint32(0xFFFF0000)) | (lo >> 16)


def _unpack_halves(p):
    lo = pltpu.bitcast(p << 16, F32)
    hi = pltpu.bitcast(p & jnp.uint32(0xFFFF0000), F32)
    return jnp.concatenate([lo, hi], axis=1)


def _merge_heads(hf_ref, hb_ref, o_ref, yf_ref, yc_ref, x_ref, mod_ref, hg_ref, g2_ref, wo_ref):
    hsum = hf_ref[0].astype(F32) + hb_ref[0].astype(F32)
    heads = [_rms(hsum[:, h * M_DH:(h + 1) * M_DH]) for h in range(M_H)]
    ym = jax.nn.sigmoid(o_ref[0].astype(F32)) * (jnp.concatenate(heads, axis=1) * hg_ref[...])
    proj = (_dot(ym.astype(BF16), wo_ref[0:M_W, :])
            + _dot(yf_ref[0], wo_ref[M_W:M_W + F_W, :])
            + _dot(yc_ref[0], wo_ref[M_W + F_W:, :]))
    xn = x_ref[0] + mod_ref[0, 2:3, :] * proj
    hx = (_rms(xn) * g2_ref[...]) * (1.0 + mod_ref[0, 4:5, :]) + mod_ref[0, 3:4, :]
    return xn, hx


def _out_ffn_kernel(hf_ref, hb_ref, o_ref, yf_ref, yc_ref, x_ref, mod_ref, hg_ref, g2_ref, wo_ref,
                    wgu_ref, wd_ref, xo_ref):
    xn, hx = _merge_heads(hf_ref, hb_ref, o_ref, yf_ref, yc_ref, x_ref, mod_ref, hg_ref, g2_ref, wo_ref)
    xo_ref[0] = xn + mod_ref[0, 5:6, :] * _swiglu_tile(hx.astype(BF16), wgu_ref, wd_ref)


def _out_kernel(hf_ref, hb_ref, o_ref, yf_ref, yc_ref, x_ref, mod_ref, hg_ref, g2_ref, wo_ref,
                wr_ref, br_ref, tri_ref, xo_ref, hx_ref, route_ref, pos_ref, cnt_ref, carry_ref):
    xn, hx = _merge_heads(hf_ref, hb_ref, o_ref, yf_ref, yc_ref, x_ref, mod_ref, hg_ref, g2_ref, wo_ref)
    route = _top2_route(_dot_precise(hx, wr_ref[...]) + br_ref[...])

    @pl.when((pl.program_id(0) == 0) & (pl.program_id(1) == 0))
    def _():
        carry_ref[...] = jnp.zeros_like(carry_ref)

    lane = lax.broadcasted_iota(jnp.int32, route.shape, 1).astype(F32)
    e1, e2 = route[:, 0:1], route[:, 1:2]
    hit = jnp.where((lane == e1) | (lane == e2), 1.0, 0.0)
    before = _dot(tri_ref[...], hit.astype(BF16)) + carry_ref[0:1, :]
    rank1 = jnp.sum(jnp.where(lane == e1, before, 0.0), axis=-1, keepdims=True)
    rank2 = jnp.sum(jnp.where(lane == e2, before, 0.0), axis=-1, keepdims=True)
    total = carry_ref[...] + jnp.sum(hit, axis=0, keepdims=True)
    xo_ref[0] = xn
    hx_ref[0] = _pack_halves(hx)
    route_ref[0] = route
    pos_ref[0] = jnp.where(lane == 0.0, rank1, jnp.where(lane == 1.0, rank2, 0.0)).astype(jnp.int32)
    carry_ref[...] = total
    cnt_ref[...] = total.astype(jnp.int32)


def _merge_specs(hf, hb, qvo, yf, yc, x, modv, head_g, norm2_g, w_out):
    b, n, d = x.shape
    tm = min(ROW_TILE, n)
    row = lambda w: pl.BlockSpec((1, tm, w), lambda i, j: (i, j, 0))
    in_specs = [row(M_W), row(M_W),
                pl.BlockSpec((1, tm, M_W), lambda i, j: (i, j, 2)),
                row(F_W), row(C_W), row(d),
                pl.BlockSpec((1, 8, d), lambda i, j: (i, 0, 0)),
                _const_spec((1, M_W)), _const_spec((1, d)), _const_spec((d, d))]
    args = [hf, hb, qvo, yf, yc, x, modv, head_g.reshape(1, M_W), norm2_g.reshape(1, d), w_out]
    return tm, row, in_specs, args


def _out_ffn(hf, hb, qvo, yf, yc, x, modv, head_g, norm2_g, w_out, wgu, wd):
    b, n, d = x.shape
    tm, row, in_specs, args = _merge_specs(hf, hb, qvo, yf, yc, x, modv, head_g, norm2_g, w_out)
    return pl.pallas_call(
        _out_ffn_kernel,
        grid=(b, n // tm),
        in_specs=in_specs + [_const_spec(wgu.shape), _const_spec(wd.shape)],
        out_specs=row(d),
        out_shape=jax.ShapeDtypeStruct((b, n, d), F32),
        compiler_params=_params("parallel", "arbitrary"),
        name="out_ffn",
    )(*args, wgu, wd)


def _out_route(hf, hb, qvo, yf, yc, x, modv, head_g, norm2_g, w_out, wr, br):
    b, n, d = x.shape
    tm, row, in_specs, args = _merge_specs(hf, hb, qvo, yf, yc, x, modv, head_g, norm2_g, w_out)
    tri = jnp.asarray(np.tril(np.ones((tm, tm), np.float32), -1), BF16)
    return pl.pallas_call(
        _out_kernel,
        grid=(b, n // tm),
        in_specs=in_specs + [_const_spec((d, 128)), _const_spec((1, 128)), _const_spec((tm, tm))],
        out_specs=(row(d), row(d // 2), row(128), row(128), _const_spec((8, 128))),
        out_shape=(jax.ShapeDtypeStruct((b, n, d), F32), jax.ShapeDtypeStruct((b, n, d // 2), jnp.uint32),
                   jax.ShapeDtypeStruct((b, n, 128), F32), jax.ShapeDtypeStruct((b, n, 128), jnp.int32),
                   jax.ShapeDtypeStruct((8, 128), jnp.int32)),
        scratch_shapes=[pltpu.VMEM((8, 128), F32)],
        compiler_params=_params("arbitrary", "arbitrary"),
        name="out_route",
    )(*args, wr, br, tri)


def _swiglu_tile(hx, wgu_ref, wd_ref):
    acc = jnp.zeros((hx.shape[0], D_MODEL), F32)
    for j in range(FFN_HIDDEN // HID_TILE):
        cs = slice(j * HID_TILE, (j + 1) * HID_TILE)
        us = slice(FFN_HIDDEN + j * HID_TILE, FFN_HIDDEN + (j + 1) * HID_TILE)
        a = _silu(_dot(hx, wgu_ref[:, cs])) * _dot(hx, wgu_ref[:, us])
        acc = acc + _dot(a.astype(BF16), wd_ref[cs, :])
    return acc


def _sc_worker_rows(n_rows):
    assert n_rows % (SC_WORKERS * SC_CHUNK) == 0
    return n_rows // SC_WORKERS


def _sc_scatter_rows(x, dest, n_out):
    t, w = x.shape
    per_w = _sc_worker_rows(t)
    mesh = plsc.VectorSubcoreMesh(core_axis_name="c", subcore_axis_name="s")

    @functools.partial(
        pl.kernel, mesh=mesh,
        out_type=jax.ShapeDtypeStruct((n_out, w), x.dtype),
        scratch_types=[pltpu.VMEM((1, SC_CHUNK), jnp.int32),
                       pltpu.VMEM((SC_CHUNK, w), x.dtype),
                       pltpu.SemaphoreType.DMA],
    )
    def scatter(x_hbm, dest_hbm, out_hbm, idx_v, rows_v, sem):
        base = (lax.axis_index("s") * SC_CORES + lax.axis_index("c")) * per_w

        @pl.loop(0, per_w // SC_CHUNK)
        def _(g):
            off = base + g * SC_CHUNK
            pltpu.sync_copy(x_hbm.at[pl.ds(off, SC_CHUNK)], rows_v)
            for k in range(2):
                pltpu.sync_copy(dest_hbm.at[pl.ds(k, 1), pl.ds(off, SC_CHUNK)], idx_v)
                pltpu.async_copy(rows_v, out_hbm.at[idx_v.at[0]], sem).wait()

    return scatter(x, dest)


def _sc_gather_rows(table, idx):
    r = idx.shape[0]
    w = table.shape[1]
    per_w = _sc_worker_rows(r)
    mesh = plsc.VectorSubcoreMesh(core_axis_name="c", subcore_axis_name="s")

    @functools.partial(
        pl.kernel, mesh=mesh,
        out_type=jax.ShapeDtypeStruct((r, w), table.dtype),
        scratch_types=[pltpu.VMEM((1, SC_CHUNK), jnp.int32),
                       pltpu.VMEM((SC_CHUNK, w), table.dtype),
                       pltpu.SemaphoreType.DMA],
    )
    def gather(table_hbm, idx_hbm, out_hbm, idx_v, rows_v, sem):
        base = (lax.axis_index("s") * SC_CORES + lax.axis_index("c")) * per_w

        @pl.loop(0, per_w // SC_CHUNK)
        def _(g):
            off = base + g * SC_CHUNK
            pltpu.sync_copy(idx_hbm.at[pl.ds(0, 1), pl.ds(off, SC_CHUNK)], idx_v)
            pltpu.async_copy(table_hbm.at[idx_v.at[0]], rows_v, sem).wait()
            pltpu.sync_copy(rows_v, out_hbm.at[pl.ds(off, SC_CHUNK)])

    return gather(table, idx.reshape(1, r))


def _group_kernel(te_ref, nv_ref, xs_ref, wgu_ref, wd_ref, ys_ref):
    @pl.when(pl.program_id(0) < nv_ref[0])
    def _():
        hx = _unpack_halves(xs_ref[...]).astype(BF16)
        ys_ref[...] = _pack_halves(_swiglu_tile(hx, wgu_ref.at[0], wd_ref.at[0]))


def _grouped_swiglu(xs, tile_expert, n_valid, wgu, wd):
    r, wp = xs.shape
    grid_spec = pltpu.PrefetchScalarGridSpec(
        num_scalar_prefetch=2,
        grid=(r // GROUP_TILE,),
        in_specs=[pl.BlockSpec((GROUP_TILE, wp), lambda i, te, nv: (i, 0)),
                  pl.BlockSpec((1,) + wgu.shape[1:], lambda i, te, nv: (te[i], 0, 0)),
                  pl.BlockSpec((1,) + wd.shape[1:], lambda i, te, nv: (te[i], 0, 0))],
        out_specs=pl.BlockSpec((GROUP_TILE, wp), lambda i, te, nv: (i, 0)),
    )
    return pl.pallas_call(
        _group_kernel,
        grid_spec=grid_spec,
        out_shape=jax.ShapeDtypeStruct((r, wp), jnp.uint32),
        compiler_params=_params("arbitrary"),
        name="grouped_swiglu",
    )(tile_expert, n_valid, xs, wgu, wd)


def _combine_kernel(x_ref, y_ref, route_ref, mod_ref, fg_ref, *rest):
    o_ref = rest[-1]
    r = route_ref[0]
    moe = r[:, 2:3] * _unpack_halves(y_ref[0]) + r[:, 3:4] * _unpack_halves(y_ref[1])
    xn = x_ref[0] + mod_ref[0, 5:6, :] * moe
    o_ref[0] = _rms(xn) * fg_ref[...]


def _combine_final(x, y2, route, modv, final_g, bi, prev):
    b, n, d = x.shape
    tm = min(ROW_TILE, n)
    row = lambda w: pl.BlockSpec((1, tm, w), lambda j: (bi, j, 0))
    in_specs = [row(d), pl.BlockSpec((2, tm, d // 2), lambda j: (0, j, 0)), row(128),
                pl.BlockSpec((1, 8, d), lambda j: (bi, 0, 0)), _const_spec((1, d))]
    args = [x, y2, route, modv, final_g.reshape(1, d)]
    aliases = {}
    if prev is not None:
        in_specs.append(pl.BlockSpec(memory_space=pl.ANY))
        args.append(prev)
        aliases = {len(args) - 1: 0}
    return pl.pallas_call(
        _combine_kernel,
        grid=(n // tm,),
        in_specs=in_specs,
        out_specs=row(d),
        out_shape=jax.ShapeDtypeStruct((b, n, d), F32),
        input_output_aliases=aliases,
        compiler_params=_params("arbitrary"),
        name="moe_combine",
    )(*args)


def _moe_final(hxp, x, route, pos, counts, modv, final_g, wgu, wd):
    b, n, d = x.shape
    t = b * n
    counts = counts[0, :N_EXPERTS]
    padded = ((counts + GROUP_TILE - 1) // GROUP_TILE) * GROUP_TILE
    ends = jnp.cumsum(padded)
    starts = ends - padded
    experts = route.reshape(t, 128)[:, :2].astype(jnp.int32)
    dest = (starts[experts] + pos.reshape(t, 128)[:, :2]).T
    n_rows = 2 * t + N_EXPERTS * GROUP_TILE
    tile_start = jnp.arange(n_rows // GROUP_TILE, dtype=jnp.int32) * GROUP_TILE
    tile_expert = jnp.minimum(jnp.sum(ends[None, :] <= tile_start[:, None], axis=1), N_EXPERTS - 1).astype(jnp.int32)
    n_valid = (ends[-1:] // GROUP_TILE).astype(jnp.int32)

    xs = _sc_scatter_rows(hxp.reshape(t, d // 2), dest, n_rows)
    ys = _grouped_swiglu(xs, tile_expert, n_valid, wgu, wd)
    out = None
    for bi in range(b):
        y2 = _sc_gather_rows(ys, dest[:, bi * n:(bi + 1) * n].reshape(2 * n))
        out = _combine_final(x, y2.reshape(2, n, d // 2), route, modv, final_g, bi, out)
    return out


def _in_weights(w):
    wm = jnp.concatenate([w[:, O_Q:O_K], w[:, O_V:O_G], w[:, O_F:P_IN], w[:, O_G:O_F],
                          jnp.zeros((w.shape[0], 128 - 4 * M_H), w.dtype)], axis=1).astype(BF16)
    return wm, w[:, O_K:O_V].T.astype(BF16), w[:, O_G:O_F].T.astype(BF16)


def _zero_state(b):
    return (jnp.zeros((b, M_H, M_DH, 2 * M_DH), F32), jnp.full((b, M_H, 8, 128), -1e30, F32))


def _mod_rows(m):
    r = m.shape[0]
    m = m.reshape(r, 6, D_MODEL)
    return jnp.concatenate([m, jnp.zeros((r, 2, D_MODEL), F32)], axis=1)


def kernel(x, c, ctx, c_ctx, norm1_g, norm2_g, w_mod, b_mod, w_in, b_gate, mlstm_norm_g, conv_dw, conv_db, conv_norm_g, conv_norm_b, w_out, ffn_w_gate_up, ffn_w_down, moe_w_router, moe_b_router, moe_w_gate_up, moe_w_down, final_norm_g):
    b = x.shape[0]
    depth = w_in.shape[0]
    assert depth == 2 and b <= 7
    cond8 = jnp.concatenate([c, c_ctx[None, :], jnp.zeros((7 - b, D_MODEL), F32)], axis=0)
    xc = ctx
    for layer in range(depth):
        last = layer == depth - 1
        mods = _modulation(cond8, w_mod[layer], b_mod[layer])
        mod_lat = _mod_rows(mods[:b])
        mod_ctx = jnp.broadcast_to(_mod_rows(mods[b:b + 1]), (b, 8, D_MODEL))
        wm, wkt, wgt = _in_weights(w_in[layer])
        bg = b_gate[layer].reshape(16, 1)
        w_o = w_out[layer].astype(BF16)

        qvo_c, kt_c, zf_c, cc_c, gt_c, gc_c = _in_proj(xc, mod_ctx, norm1_g[layer], wm, wkt, wgt, bg)
        qvo_l, kt_l, zf_l, cc_l, gt_l, gc_l = _in_proj(x, mod_lat, norm1_g[layer], wm, wkt, wgt, bg)

        hf_c, hb_c, cf, cb, mf, mb = _mlstm(qvo_c, kt_c, gt_c, gc_c, _zero_state(b), _zero_state(b))
        hf_l, hb_l, _, _, _, _ = _mlstm(qvo_l, kt_l, gt_l, gc_l, (cf, mf), (cb, mb))

        conv_args = (conv_dw[layer], conv_db[layer], conv_norm_g[layer], conv_norm_b[layer])
        yf_l = _fourier(zf_l, x.shape[1])
        yc_l = _conv(cc_l, *conv_args)
        if not last:
            wgu = ffn_w_gate_up[layer // 2].astype(BF16)
            wd = ffn_w_down[layer // 2].astype(BF16)
            x = _out_ffn(hf_l, hb_l, qvo_l, yf_l, yc_l, x, mod_lat, mlstm_norm_g[layer],
                         norm2_g[layer], w_o, wgu, wd)
            yf_c = _fourier(zf_c, xc.shape[1])
            yc_c = _conv(cc_c, *conv_args)
            xc = _out_ffn(hf_c, hb_c, qvo_c, yf_c, yc_c, xc, mod_ctx, mlstm_norm_g[layer],
                          norm2_g[layer], w_o, wgu, wd)
        else:
            idx = layer // 2
            wr = jnp.concatenate([moe_w_router[idx], jnp.zeros((D_MODEL, 128 - N_EXPERTS), F32)], axis=1)
            br = jnp.concatenate([moe_b_router[idx], jnp.zeros((128 - N_EXPERTS,), F32)]).reshape(1, 128)
            x, hxp, route, pos, counts = _out_route(hf_l, hb_l, qvo_l, yf_l, yc_l, x, mod_lat,
                                                    mlstm_norm_g[layer], norm2_g[layer], w_o, wr, br)
            x = _moe_final(hxp, x, route, pos, counts, mod_lat, final_norm_g,
                           moe_w_gate_up[idx].astype(BF16), moe_w_down[idx].astype(BF16))
    return x
```

```python
import functools
import math

import numpy as np
import jax
import jax.numpy as jnp
from jax import lax
from jax.experimental import pallas as pl
from jax.experimental.pallas import tpu as pltpu
from jax.experimental.pallas import tpu_sc as plsc

F32 = jnp.float32
BF16 = jnp.bfloat16

D_MODEL = 1024
M_W = 512
M_H = 4
M_DH = 128
F_W = 256
F_G = 4
C_W = 256
C_G = 4
CONV_K = 31
FFN_HIDDEN = 2816
N_EXPERTS = 8
EPS = 1e-6
LOG2E = 1.4426950408889634

O_Q = 0
O_K = O_Q + M_W
O_V = O_K + M_W
O_O = O_V + M_W
O_G = O_O + M_W
O_F = O_G + 4 * M_H
O_CU = O_F + F_W
O_CG = O_CU + C_W
P_IN = O_CG + C_W

LC = 256
MLSTM_SUB = 4
DFT_N1 = 128
DFT_DIRECT_MAX = 512
DFT_GROUP = 16
ROW_TILE = 512
IN_TILE = 512
HID_TILE = 256
CONV_TILE = 256
CONV_PAD = 16
GROUP_TILE = 512
SC_CORES = 2
SC_SUBCORES = 16
SC_WORKERS = SC_CORES * SC_SUBCORES
SC_CHUNK = 128
VMEM_LIMIT = 56 * 1024 * 1024


def _params(*sem):
    return pltpu.CompilerParams(dimension_semantics=sem, vmem_limit_bytes=VMEM_LIMIT)


def _const_spec(shape):
    zeros = (0,) * len(shape)
    return pl.BlockSpec(shape, lambda *_: zeros, pipeline_mode=pl.Buffered(1))


def _dot(a, b):
    return jnp.dot(a, b, preferred_element_type=F32)


def _dot_nt(a, b):
    return lax.dot_general(a, b, (((1,), (1,)), ((), ())), preferred_element_type=F32)


def _split2(a):
    hi = a.astype(BF16)
    lo = (a - hi.astype(F32)).astype(BF16)
    return hi, lo


def _split3(a):
    x1 = a.astype(BF16)
    r1 = a - x1.astype(F32)
    x2 = r1.astype(BF16)
    x3 = (r1 - x2.astype(F32)).astype(BF16)
    return x1, x2, x3


def _dot_precise(a, b):
    ah, al = _split2(a)
    bh, bl = _split2(b)
    return _dot(ah, bh) + _dot(al, bh) + _dot(ah, bl)


def _silu(x):
    return x * jax.nn.sigmoid(x)


def _log_sigmoid(x):
    return -(jnp.maximum(-x, 0.0) + jnp.log1p(jnp.exp(-jnp.abs(x))))


def _rms(x):
    return x * lax.rsqrt(jnp.mean(x * x, axis=-1, keepdims=True) + EPS)


def _mod_kernel(c_ref, w_ref, b_ref, o_ref):
    o_ref[0] = _dot_precise(_silu(c_ref[...]), w_ref[0]) + b_ref[0]


def _modulation(cond8, w_mod, b_mod):
    d = cond8.shape[1]
    depth, _, n_out = w_mod.shape
    tn = 1536
    return pl.pallas_call(
        _mod_kernel,
        grid=(depth, n_out // tn),
        in_specs=[_const_spec((8, d)),
                  pl.BlockSpec((1, d, tn), lambda l, j: (l, 0, j)),
                  pl.BlockSpec((1, 1, tn), lambda l, j: (l, 0, j))],
        out_specs=pl.BlockSpec((1, 8, tn), lambda l, j: (l, 0, j)),
        out_shape=jax.ShapeDtypeStruct((depth, 8, n_out), F32),
        compiler_params=_params("arbitrary", "arbitrary"),
        name="modulation",
    )(cond8, w_mod, b_mod.reshape(depth, 1, n_out))


def _in_kernel(x_ref, mod_ref, g_ref, wm_ref, wkt_ref, wgt_ref, bg_ref, bgr_ref,
               qvo_ref, kt_ref, zf_ref, cc_ref, gt_ref, gc_ref, *scratch):
    tm = x_ref.shape[1]
    h = _rms(x_ref[0]) * g_ref[...]
    h = h * (1.0 + mod_ref[0, 1:2, :]) + mod_ref[0, 0:1, :]
    hb = h.astype(BF16)
    p = _dot(hb, wm_ref[...])
    kt = _dot_nt(wkt_ref[...], hb) * (M_DH ** -0.5)
    gt = _dot_nt(wgt_ref[...], hb) + bg_ref[...]
    o_f, o_c, o_g = 3 * M_W, 3 * M_W + F_W, 3 * M_W + F_W + 2 * C_W
    qvo_ref[0] = p[:, :o_f].astype(BF16)
    if scratch:
        zs_ref, = scratch
        rows = zf_ref.shape[1]
        n2cnt = tm // rows
        for half in range(F_W // 128):
            zs_ref[half] = p[:, o_f + half * 128:o_f + (half + 1) * 128]
        for j in range(n2cnt):
            for half in range(F_W // 128):
                lanes = slice(j * F_W + half * 128, j * F_W + (half + 1) * 128)
                zf_ref[0, :, lanes] = zs_ref[half, pl.ds(j, rows, stride=n2cnt), :]
    else:
        zf_ref[0] = p[:, o_f:o_c].astype(BF16)
    cc_ref[0] = p[:, o_c:o_g].astype(BF16)
    kt_ref[0] = kt.astype(BF16)
    gc_ref[0] = p[:, o_g:] + bgr_ref[...]
    for j in range(tm // LC):
        gt_ref[0, j] = gt[:, j * LC:(j + 1) * LC]


def _in_proj(x, modv, norm_g, wm, wkt, wgt, bg):
    b, n, d = x.shape
    tm = min(IN_TILE, n)
    nm = wm.shape[1]
    bg_row = jnp.concatenate([bg.reshape(1, 16), jnp.zeros((1, 128 - 16), F32)], axis=1)
    if n > DFT_DIRECT_MAX:
        n2cnt = n // DFT_N1
        assert tm % n2cnt == 0 and (tm // n2cnt) % 8 == 0
        zf_sds = jax.ShapeDtypeStruct((b, DFT_N1, n2cnt * F_W), F32)
        zf_spec = pl.BlockSpec((1, tm // n2cnt, n2cnt * F_W), lambda i, j: (i, j, 0))
        scratch = [pltpu.VMEM((F_W // 128, tm, 128), F32)]
    else:
        zf_sds = jax.ShapeDtypeStruct((b, n, F_W), BF16)
        zf_spec = pl.BlockSpec((1, tm, F_W), lambda i, j: (i, j, 0))
        scratch = []
    out_shape = (
        jax.ShapeDtypeStruct((b, n, 3 * M_W), BF16),
        jax.ShapeDtypeStruct((b, M_W, n), BF16),
        zf_sds,
        jax.ShapeDtypeStruct((b, n, 2 * C_W), BF16),
        jax.ShapeDtypeStruct((b, n // LC, 16, LC), F32),
        jax.ShapeDtypeStruct((b, n, 128), F32),
    )
    return pl.pallas_call(
        _in_kernel,
        grid=(b, n // tm),
        in_specs=[pl.BlockSpec((1, tm, d), lambda i, j: (i, j, 0)),
                  pl.BlockSpec((1, 8, d), lambda i, j: (i, 0, 0)),
                  _const_spec((1, d)),
                  _const_spec((d, nm)),
                  _const_spec((M_W, d)),
                  _const_spec((16, d)),
                  _const_spec((16, 1)),
                  _const_spec((1, 128))],
        out_specs=(pl.BlockSpec((1, tm, 3 * M_W), lambda i, j: (i, j, 0)),
                   pl.BlockSpec((1, M_W, tm), lambda i, j: (i, 0, j)),
                   zf_spec,
                   pl.BlockSpec((1, tm, 2 * C_W), lambda i, j: (i, j, 0)),
                   pl.BlockSpec((1, tm // LC, 16, LC), lambda i, j: (i, j, 0, 0)),
                   pl.BlockSpec((1, tm, 128), lambda i, j: (i, j, 0))),
        out_shape=out_shape,
        scratch_shapes=scratch,
        compiler_params=_params("parallel", "arbitrary"),
        name="in_proj",
    )(x, modv, norm_g.reshape(1, d), wm, wkt, wgt, bg, bg_row)


def _mlstm_kernel(qf_ref, vf_ref, ktf_ref, gtf_ref, gcf_ref, qb_ref, vb_ref, ktb_ref, gtb_ref, gcb_ref,
                  cf0_ref, cb0_ref, mf0_ref, mb0_ref, u3_ref, lo3_ref, ut3_ref, lot3_ref,
                  hf_ref, hb_ref, cf_ref, cb_ref, mf_ref, mb_ref):
    @pl.when(pl.program_id(1) == 0)
    def _():
        cf_ref[...] = cf0_ref[...]
        cb_ref[...] = cb0_ref[...]
        mf_ref[...] = mf0_ref[...]
        mb_ref[...] = mb0_ref[...]

    sub = qf_ref.shape[1] // LC
    t_idx = lax.broadcasted_iota(jnp.int32, (LC, LC), 0)
    s_idx = lax.broadcasted_iota(jnp.int32, (LC, LC), 1)
    ones = jnp.ones((LC, M_DH), BF16)
    state = [[(c_ref[0, h], m_ref[0, h, 0:1, 0:1]) for h in range(M_H)] for c_ref, m_ref in
             ((cf_ref, mf_ref), (cb_ref, mb_ref))]
    h_out = []
    for k in range(sub):
        kf, kb = k, sub - 1 - k
        gf = gtf_ref[0, kf]
        gb = gtb_ref[0, kb]
        rows_f = _dot(jnp.concatenate(_split3(_log_sigmoid(gf)), axis=1), u3_ref[...])
        rows_b = _dot(jnp.concatenate(_split3(_log_sigmoid(gb)), axis=1), lo3_ref[...])
        cols_f = _dot(ut3_ref[...], jnp.concatenate(
            _split3(_log_sigmoid(gcf_ref[0, kf * LC:(kf + 1) * LC, :])), axis=0))
        cols_b = _dot(lot3_ref[...], jnp.concatenate(
            _split3(_log_sigmoid(gcb_ref[0, kb * LC:(kb + 1) * LC, :])), axis=0))
        dirs = (
            (0, qf_ref, vf_ref, ktf_ref, kf, gf, rows_f, cols_f, s_idx <= t_idx, hf_ref, 0, 4, LC - 1),
            (1, qb_ref, vb_ref, ktb_ref, kb, gb, rows_b, cols_b, s_idx >= t_idx, hb_ref, 8, 12, 0),
        )
        for d, q_ref, v_ref, kt_ref, kc, g, rows, cols, mask, h_ref, i_off, f_off, last in dirs:
            rs = slice(kc * LC, (kc + 1) * LC)
            for h in range(M_H):
                hs = slice(h * M_DH, (h + 1) * M_DH)
                q = q_ref[0, rs, hs]
                kt = kt_ref[0, hs, rs]
                vaug = jnp.concatenate([v_ref[0, rs, hs], ones], axis=1)
                i_row = g[i_off + h:i_off + h + 1, :]
                b_row = rows[f_off + h:f_off + h + 1, :]
                b_col = jnp.broadcast_to(cols[:, f_off + h:f_off + h + 1], (LC, M_DH))
                b_last = b_row[:, last:last + 1]
                c0, m0 = state[d][h]

                am = jnp.where(mask, (i_row - b_row) * LOG2E, -jnp.inf)
                g2 = jnp.maximum(m0 * LOG2E, jnp.max(am, axis=-1, keepdims=True))
                g2 = jnp.broadcast_to(g2, (LC, M_DH))
                e = jnp.exp2(am - jnp.concatenate([g2, g2], axis=1))
                s = (_dot(q, kt) * e).astype(BF16)
                qi = (q.astype(F32) * jnp.exp2(m0 * LOG2E - g2)).astype(BF16)
                na = _dot(s, vaug) + _dot(qi, c0.astype(BF16))
                floor = jnp.exp2(-(b_col * LOG2E + g2))
                h_new = (na[:, :M_DH] / jnp.maximum(jnp.abs(na[:, M_DH:]), floor)).astype(h_ref.dtype)
                h_out.append((h_ref, rs, hs, h_new))

                gs = b_last - b_row + i_row
                m_new = jnp.maximum(b_last + m0, jnp.max(gs, axis=-1, keepdims=True))
                kw = (kt.astype(F32) * jnp.exp(gs - m_new)).astype(BF16)
                state[d][h] = (jnp.exp(b_last + m0 - m_new) * c0 + _dot(kw, vaug), m_new)

    for h_ref, rs, hs, h_new in h_out:
        h_ref[0, rs, hs] = h_new
    for d, (c_ref, m_ref) in enumerate(((cf_ref, mf_ref), (cb_ref, mb_ref))):
        for h in range(M_H):
            c_ref[0, h] = state[d][h][0]
            m_ref[0, h] = jnp.broadcast_to(state[d][h][1], (8, 128))


def _tri_constants():
    s = np.arange(LC)[:, None]
    t = np.arange(LC)[None, :]
    u = (s <= t).astype(np.float32)
    lo = (s >= t).astype(np.float32)
    u3 = np.concatenate([u, u, u], axis=0)
    lo3 = np.concatenate([lo, lo, lo], axis=0)
    return (jnp.asarray(u3, BF16), jnp.asarray(lo3, BF16),
            jnp.asarray(u3.T.copy(), BF16), jnp.asarray(lo3.T.copy(), BF16))


def _mlstm(qvo, kt, gt, gc, state_f, state_b):
    b, n, _ = qvo.shape
    sub = min(MLSTM_SUB, n // LC)
    rows = sub * LC
    nc = n // rows
    cf0, mf0 = state_f
    cb0, mb0 = state_b
    fwd = lambda i, c: (i, c, 0)
    bwd = lambda i, c: (i, nc - 1 - c, 0)
    st_c = pl.BlockSpec((1, M_H, M_DH, 2 * M_DH), lambda i, c: (i, 0, 0, 0))
    st_m = pl.BlockSpec((1, M_H, 8, 128), lambda i, c: (i, 0, 0, 0))
    h_sds = jax.ShapeDtypeStruct((b, n, M_W), BF16)
    return pl.pallas_call(
        _mlstm_kernel,
        grid=(b, nc),
        in_specs=[pl.BlockSpec((1, rows, M_W), fwd),
                  pl.BlockSpec((1, rows, M_W), lambda i, c: (i, c, 1)),
                  pl.BlockSpec((1, M_W, rows), lambda i, c: (i, 0, c)),
                  pl.BlockSpec((1, sub, 16, LC), lambda i, c: (i, c, 0, 0)),
                  pl.BlockSpec((1, rows, 128), fwd),
                  pl.BlockSpec((1, rows, M_W), bwd),
                  pl.BlockSpec((1, rows, M_W), lambda i, c: (i, nc - 1 - c, 1)),
                  pl.BlockSpec((1, M_W, rows), lambda i, c: (i, 0, nc - 1 - c)),
                  pl.BlockSpec((1, sub, 16, LC), lambda i, c: (i, nc - 1 - c, 0, 0)),
                  pl.BlockSpec((1, rows, 128), bwd),
                  st_c, st_c, st_m, st_m,
                  _const_spec((3 * LC, LC)), _const_spec((3 * LC, LC)),
                  _const_spec((LC, 3 * LC)), _const_spec((LC, 3 * LC))],
        out_specs=(pl.BlockSpec((1, rows, M_W), fwd), pl.BlockSpec((1, rows, M_W), bwd),
                   st_c, st_c, st_m, st_m),
        out_shape=(h_sds, h_sds,
                   jax.ShapeDtypeStruct(cf0.shape, F32), jax.ShapeDtypeStruct(cb0.shape, F32),
                   jax.ShapeDtypeStruct(mf0.shape, F32), jax.ShapeDtypeStruct(mb0.shape, F32)),
        compiler_params=_params("parallel", "arbitrary"),
        name="mlstm",
    )(qvo, qvo, kt, gt, gc, qvo, qvo, kt, gt, gc, cf0, cb0, mf0, mb0, *_tri_constants())


def _channel_dft_matrix():
    gw = F_W // F_G
    j = np.arange(gw)[:, None]
    l = np.arange(gw)[None, :]
    ang = 2.0 * np.pi * j * l / gw
    eye = np.eye(F_G)
    bc = np.kron(eye, np.cos(ang)) / math.sqrt(gw)
    bs = np.kron(eye, np.sin(ang)) / math.sqrt(gw)
    return jnp.asarray(np.concatenate([bc, -bs], axis=1), BF16)


def _dft_cos_sin(n):
    k = np.arange(n)[:, None]
    m = np.arange(n)[None, :]
    ang = 2.0 * np.pi * ((k * m) % n) / n
    return np.cos(ang) / math.sqrt(n), np.sin(ang) / math.sqrt(n)


def _fourier1_kernel(z_ref, bcs_ref, f1_ref, o_ref, ts_ref):
    n2cnt = o_ref.shape[1] // 2

    def group(g, carry):
        for r in range(DFT_GROUP):
            j = g * DFT_GROUP + r
            off = pl.multiple_of(j * F_W, F_W)
            ab = _dot(z_ref[0, :, pl.ds(off, F_W)].astype(BF16), bcs_ref[...])
            u = jnp.concatenate([ab[:, :F_W], ab[:, F_W:]], axis=0).astype(BF16)
            t = _dot(f1_ref[j], u)
            parts = (t[:DFT_N1], t[DFT_N1:])
            for part in range(2):
                for half in range(F_W // 128):
                    ts_ref[2 * part + half, r * DFT_N1:(r + 1) * DFT_N1, :] = parts[part][:, half * 128:(half + 1) * 128]
        for part in range(2):
            row0 = pl.multiple_of(part * n2cnt + g * DFT_GROUP, DFT_GROUP)
            for k1 in range(DFT_N1):
                for half in range(F_W // 128):
                    tile = ts_ref[2 * part + half, pl.ds(k1, DFT_GROUP, stride=DFT_N1), :]
                    lanes = slice(k1 * F_W + half * 128, k1 * F_W + (half + 1) * 128)
                    o_ref[0, pl.ds(row0, DFT_GROUP), lanes] = tile.astype(BF16)
        return carry

    lax.fori_loop(0, n2cnt // DFT_GROUP, group, 0)


def _fourier2_kernel(t_ref, f2_ref, o_ref, s_ref):
    n2cnt = f2_ref.shape[0]
    width = t_ref.shape[2]
    tw = min(4096, width)
    for c in range(width // tw):
        y = _dot(f2_ref[...], t_ref[0, :, c * tw:(c + 1) * tw])
        for kk in range(tw // F_W):
            k1 = c * (tw // F_W) + kk
            for half in range(F_W // 128):
                lanes = slice(kk * F_W + half * 128, kk * F_W + (half + 1) * 128)
                s_ref[half, k1 * n2cnt:(k1 + 1) * n2cnt, :] = y[:, lanes]
    for k2 in range(n2cnt):
        for a in range(DFT_N1 // DFT_GROUP):
            for half in range(F_W // 128):
                rows = s_ref[half, pl.ds(DFT_GROUP * a * n2cnt + k2, DFT_GROUP, stride=n2cnt), :]
                r0 = k2 * DFT_N1 + DFT_GROUP * a
                o_ref[0, r0:r0 + DFT_GROUP, half * 128:(half + 1) * 128] = rows.astype(o_ref.dtype)


def _fourier_direct_kernel(z_ref, bcs_ref, f_ref, o_ref):
    ab = _dot(z_ref[0], bcs_ref[...])
    u = jnp.concatenate([ab[:, :F_W], ab[:, F_W:]], axis=0).astype(BF16)
    o_ref[0] = _dot(f_ref[...], u).astype(o_ref.dtype)


def _fourier(zf, n):
    b = zf.shape[0]
    bcs = _channel_dft_matrix()
    if n <= DFT_DIRECT_MAX:
        c, s = _dft_cos_sin(n)
        fmat = jnp.asarray(np.concatenate([c, s], axis=1), BF16)
        return pl.pallas_call(
            _fourier_direct_kernel,
            grid=(b,),
            in_specs=[pl.BlockSpec((1, n, F_W), lambda i: (i, 0, 0)),
                      _const_spec((F_W, 2 * F_W)), _const_spec((n, 2 * n))],
            out_specs=pl.BlockSpec((1, n, F_W), lambda i: (i, 0, 0)),
            out_shape=jax.ShapeDtypeStruct((b, n, F_W), BF16),
            compiler_params=_params("parallel"),
            name="fourier_direct",
        )(zf, bcs, fmat)

    n1 = DFT_N1
    n2 = n // n1
    assert n2 % DFT_GROUP == 0
    jj = np.arange(n2)[:, None, None]
    k1 = np.arange(n1)[None, :, None]
    aa = np.arange(n1)[None, None, :]
    ang = 2.0 * np.pi * (((jj * k1) % n) / n + ((k1 * aa) % n1) / n1)
    c1, s1 = np.cos(ang) / math.sqrt(n1), np.sin(ang) / math.sqrt(n1)
    f1 = jnp.asarray(np.concatenate([np.concatenate([c1, s1], axis=2),
                                     np.concatenate([-s1, c1], axis=2)], axis=1), BF16)
    c2, s2 = _dft_cos_sin(n2)
    f2 = jnp.asarray(np.concatenate([c2, s2], axis=1), BF16)
    width = n1 * F_W

    t = pl.pallas_call(
        _fourier1_kernel,
        grid=(b,),
        in_specs=[pl.BlockSpec((1, n1, n2 * F_W), lambda i: (i, 0, 0)),
                  _const_spec((F_W, 2 * F_W)), _const_spec((n2, 2 * n1, 2 * n1))],
        out_specs=pl.BlockSpec((1, 2 * n2, width), lambda i: (i, 0, 0)),
        out_shape=jax.ShapeDtypeStruct((b, 2 * n2, width), BF16),
        scratch_shapes=[pltpu.VMEM((2 * (F_W // 128), DFT_GROUP * n1, 128), F32)],
        compiler_params=_params("parallel"),
        name="fourier_stage1",
    )(zf, bcs, f1)
    return pl.pallas_call(
        _fourier2_kernel,
        grid=(b,),
        in_specs=[pl.BlockSpec((1, 2 * n2, width), lambda i: (i, 0, 0)),
                  _const_spec((n2, 2 * n2))],
        out_specs=pl.BlockSpec((1, n, F_W), lambda i: (i, 0, 0)),
        out_shape=jax.ShapeDtypeStruct((b, n, F_W), BF16),
        scratch_shapes=[pltpu.VMEM((F_W // 128, n, 128), F32)],
        compiler_params=_params("parallel"),
        name="fourier_stage2",
    )(t, f2)


def _group_mean(y, gm_ref):
    hi, lo = _split2(y)
    return _dot(hi, gm_ref[...]) + _dot(lo, gm_ref[...])


def _conv_kernel(cc_ref, dw_ref, db_ref, lg_ref, lb_ref, gm_ref, o_ref, a_ref, ph_ref):
    n = cc_ref.shape[1]
    rt = min(CONV_TILE, n)
    span = rt + 2 * CONV_PAD - 8
    a_ref[0:CONV_PAD, :] = jnp.zeros((CONV_PAD, C_W), F32)
    a_ref[n + CONV_PAD:n + 2 * CONV_PAD, :] = jnp.zeros((CONV_PAD, C_W), F32)

    def glu(i, carry):
        r = pl.multiple_of(i * rt, rt)
        blk = cc_ref[0, pl.ds(r, rt), :].astype(F32)
        a_ref[pl.ds(r + CONV_PAD, rt), :] = blk[:, :C_W] * jax.nn.sigmoid(blk[:, C_W:])
        return carry

    lax.fori_loop(0, n // rt, glu, 0)

    def conv(i, carry):
        r = pl.multiple_of(i * rt, rt)
        win = a_ref[pl.ds(r, rt + 2 * CONV_PAD), :]
        for s in range(1, 8):
            ph_ref[s - 1] = win[s:s + span]
        acc = jnp.broadcast_to(db_ref[...], (rt, C_W))
        for j in range(CONV_K):
            off = CONV_PAD - CONV_K // 2 + j
            base = 8 * (off // 8)
            if off % 8 == 0:
                tap = a_ref[pl.ds(r + base, rt), :]
            else:
                tap = ph_ref[off % 8 - 1, base:base + rt, :]
            acc = acc + tap * dw_ref[j:j + 1, :]
        d = acc - _group_mean(acc, gm_ref)
        var = _group_mean(d * d, gm_ref)
        yn = d * lax.rsqrt(var + EPS) * lg_ref[...] + lb_ref[...]
        o_ref[0, pl.ds(r, rt), :] = _silu(yn).astype(o_ref.dtype)
        return carry

    lax.fori_loop(0, n // rt, conv, 0)


def _conv(cc, dw, db, ln_g, ln_b):
    b, n, _ = cc.shape
    gw = C_W // C_G
    gm = jnp.asarray(np.kron(np.eye(C_G), np.full((gw, gw), 1.0 / gw)), BF16)
    dw32 = jnp.concatenate([dw, jnp.zeros((32 - CONV_K, C_W), F32)], axis=0)
    return pl.pallas_call(
        _conv_kernel,
        grid=(b,),
        in_specs=[pl.BlockSpec((1, n, 2 * C_W), lambda i: (i, 0, 0)),
                  _const_spec((32, C_W)), _const_spec((1, C_W)), _const_spec((1, C_W)),
                  _const_spec((1, C_W)), _const_spec((C_W, C_W))],
        out_specs=pl.BlockSpec((1, n, C_W), lambda i: (i, 0, 0)),
        out_shape=jax.ShapeDtypeStruct((b, n, C_W), BF16),
        scratch_shapes=[pltpu.VMEM((n + 2 * CONV_PAD, C_W), F32),
                        pltpu.VMEM((7, min(CONV_TILE, n) + 2 * CONV_PAD - 8, C_W), F32)],
        compiler_params=_params("parallel"),
        name="conv_module",
    )(cc, dw32, db.reshape(1, C_W), ln_g.reshape(1, C_W), ln_b.reshape(1, C_W), gm)


def _top2_route(logits):
    row = lax.broadcasted_iota(jnp.int32, logits.shape, 0).astype(F32)
    lg = jnp.where(row < N_EXPERTS, logits, -jnp.inf)
    v1 = jnp.max(lg, axis=0, keepdims=True)
    i1 = jnp.min(jnp.where(lg == v1, row, 16.0), axis=0, keepdims=True)
    lg2 = jnp.where(row == i1, -jnp.inf, lg)
    v2 = jnp.max(lg2, axis=0, keepdims=True)
    i2 = jnp.min(jnp.where(lg2 == v2, row, 16.0), axis=0, keepdims=True)
    e2 = jnp.exp(v2 - v1)
    return row, i1, i2, 1.0 / (1.0 + e2), e2 / (1.0 + e2)


def _pack_halves(y):
    w = y.shape[1] // 2
    lo = pltpu.bitcast(y[:, :w].astype(BF16).astype(F32), jnp.uint32)
    hi = pltpu.bitcast(y[:, w:].astype(BF16).astype(F32), jnp.uint32)
    return (hi & jnp.uint32(0xFFFF0000)) | (lo >> 16)


def _unpack_halves(p):
    lo = pltpu.bitcast(p << 16, F32)
    hi = pltpu.bitcast(p & jnp.uint32(0xFFFF0000), F32)
    return jnp.concatenate([lo, hi], axis=1)


def _merge_heads(hf_ref, hb_ref, o_ref, yf_ref, yc_ref, x_ref, mod_ref, hg_ref, g2_ref, wo_ref):
    hsum = hf_ref[0].astype(F32) + hb_ref[0].astype(F32)
    heads = [_rms(hsum[:, h * M_DH:(h + 1) * M_DH]) for h in range(M_H)]
    ym = jax.nn.sigmoid(o_ref[0].astype(F32)) * (jnp.concatenate(heads, axis=1) * hg_ref[...])
    proj = (_dot(ym.astype(BF16), wo_ref[0:M_W, :])
            + _dot(yf_ref[0], wo_ref[M_W:M_W + F_W, :])
            + _dot(yc_ref[0], wo_ref[M_W + F_W:, :]))
    xn = x_ref[0] + mod_ref[0, 2:3, :] * proj
    hx = (_rms(xn) * g2_ref[...]) * (1.0 + mod_ref[0, 4:5, :]) + mod_ref[0, 3:4, :]
    return xn, hx


def _out_ffn_kernel(hf_ref, hb_ref, o_ref, yf_ref, yc_ref, x_ref, mod_ref, hg_ref, g2_ref, wo_ref,
                    wgu_ref, wd_ref, xo_ref):
    xn, hx = _merge_heads(hf_ref, hb_ref, o_ref, yf_ref, yc_ref, x_ref, mod_ref, hg_ref, g2_ref, wo_ref)
    xo_ref[0] = xn + mod_ref[0, 5:6, :] * _swiglu_tile(hx.astype(BF16), wgu_ref, wd_ref)


def _out_kernel(hf_ref, hb_ref, o_ref, yf_ref, yc_ref, x_ref, mod_ref, hg_ref, g2_ref, wo_ref,
                wrt_ref, brt_ref, tri_ref, xo_ref, hx_ref, route_ref, cnt_ref, carry_ref):
    xn, hx = _merge_heads(hf_ref, hb_ref, o_ref, yf_ref, yc_ref, x_ref, mod_ref, hg_ref, g2_ref, wo_ref)
    hh, hl = _split2(hx)
    wh, wl = _split2(wrt_ref[...])
    logits = _dot_nt(wh, hh) + _dot_nt(wl, hh) + _dot_nt(wh, hl) + brt_ref[:, 0:1]
    row, i1, i2, w1, w2 = _top2_route(logits)

    @pl.when((pl.program_id(0) == 0) & (pl.program_id(1) == 0))
    def _():
        carry_ref[...] = jnp.zeros_like(carry_ref)

    hit = jnp.where((row == i1) | (row == i2), 1.0, 0.0)
    before = _dot(hit.astype(BF16), tri_ref[...]) + carry_ref[:, 0:1]
    rank1 = jnp.sum(jnp.where(row == i1, before, 0.0), axis=0, keepdims=True)
    rank2 = jnp.sum(jnp.where(row == i2, before, 0.0), axis=0, keepdims=True)
    total = carry_ref[...] + jnp.sum(hit, axis=1, keepdims=True)
    xo_ref[0] = xn
    hx_ref[0] = _pack_halves(hx)
    zero = jnp.zeros_like(w1)
    route_ref[0] = jnp.concatenate([i1, i2, w1, w2, rank1, rank2, zero, zero], axis=0)
    carry_ref[...] = total
    cnt_ref[...] = total.astype(jnp.int32)


def _merge_specs(hf, hb, qvo, yf, yc, x, modv, head_g, norm2_g, w_out):
    b, n, d = x.shape
    tm = min(ROW_TILE, n)
    row = lambda w: pl.BlockSpec((1, tm, w), lambda i, j: (i, j, 0))
    in_specs = [row(M_W), row(M_W),
                pl.BlockSpec((1, tm, M_W), lambda i, j: (i, j, 2)),
                row(F_W), row(C_W), row(d),
                pl.BlockSpec((1, 8, d), lambda i, j: (i, 0, 0)),
                _const_spec((1, M_W)), _const_spec((1, d)), _const_spec((d, d))]
    args = [hf, hb, qvo, yf, yc, x, modv, head_g.reshape(1, M_W), norm2_g.reshape(1, d), w_out]
    return tm, row, in_specs, args


def _out_ffn(hf, hb, qvo, yf, yc, x, modv, head_g, norm2_g, w_out, wgu, wd):
    b, n, d = x.shape
    tm, row, in_specs, args = _merge_specs(hf, hb, qvo, yf, yc, x, modv, head_g, norm2_g, w_out)
    return pl.pallas_call(
        _out_ffn_kernel,
        grid=(b, n // tm),
        in_specs=in_specs + [_const_spec(wgu.shape), _const_spec(wd.shape)],
        out_specs=row(d),
        out_shape=jax.ShapeDtypeStruct((b, n, d), F32),
        compiler_params=_params("parallel", "arbitrary"),
        name="out_ffn",
    )(*args, wgu, wd)


def _out_route(hf, hb, qvo, yf, yc, x, modv, head_g, norm2_g, w_out, wrt, brt):
    b, n, d = x.shape
    tm, row, in_specs, args = _merge_specs(hf, hb, qvo, yf, yc, x, modv, head_g, norm2_g, w_out)
    tri = jnp.asarray(np.triu(np.ones((tm, tm), np.float32), 1), BF16)
    return pl.pallas_call(
        _out_kernel,
        grid=(b, n // tm),
        in_specs=in_specs + [_const_spec((16, d)), _const_spec((16, 128)), _const_spec((tm, tm))],
        out_specs=(row(d), row(d // 2), pl.BlockSpec((1, 8, tm), lambda i, j: (i, 0, j)),
                   _const_spec((16, 128))),
        out_shape=(jax.ShapeDtypeStruct((b, n, d), F32), jax.ShapeDtypeStruct((b, n, d // 2), jnp.uint32),
                   jax.ShapeDtypeStruct((b, 8, n), F32), jax.ShapeDtypeStruct((16, 128), jnp.int32)),
        scratch_shapes=[pltpu.VMEM((16, 128), F32)],
        compiler_params=_params("arbitrary", "arbitrary"),
        name="out_route",
    )(*args, wrt, brt, tri)


def _swiglu_tile(hx, wgu_ref, wd_ref):
    acc = jnp.zeros((hx.shape[0], D_MODEL), F32)
    for j in range(FFN_HIDDEN // HID_TILE):
        cs = slice(j * HID_TILE, (j + 1) * HID_TILE)
        us = slice(FFN_HIDDEN + j * HID_TILE, FFN_HIDDEN + (j + 1) * HID_TILE)
        a = _silu(_dot(hx, wgu_ref[:, cs])) * _dot(hx, wgu_ref[:, us])
        acc = acc + _dot(a.astype(BF16), wd_ref[cs, :])
    return acc


def _sc_worker_rows(n_rows):
    assert n_rows % (SC_WORKERS * SC_CHUNK) == 0
    return n_rows // SC_WORKERS


def _sc_scatter_rows(x, dest, n_out):
    t, w = x.shape
    per_w = _sc_worker_rows(t)
    mesh = plsc.VectorSubcoreMesh(core_axis_name="c", subcore_axis_name="s")

    @functools.partial(
        pl.kernel, mesh=mesh,
        out_type=jax.ShapeDtypeStruct((n_out, w), x.dtype),
        scratch_types=[pltpu.VMEM((1, SC_CHUNK), jnp.int32),
                       pltpu.VMEM((SC_CHUNK, w), x.dtype),
                       pltpu.SemaphoreType.DMA],
    )
    def scatter(x_hbm, dest_hbm, out_hbm, idx_v, rows_v, sem):
        base = (lax.axis_index("s") * SC_CORES + lax.axis_index("c")) * per_w

        @pl.loop(0, per_w // SC_CHUNK)
        def _(g):
            off = base + g * SC_CHUNK
            pltpu.sync_copy(x_hbm.at[pl.ds(off, SC_CHUNK)], rows_v)
            for k in range(2):
                pltpu.sync_copy(dest_hbm.at[pl.ds(k, 1), pl.ds(off, SC_CHUNK)], idx_v)
                pltpu.async_copy(rows_v, out_hbm.at[idx_v.at[0]], sem).wait()

    return scatter(x, dest)


def _sc_gather_rows(table, idx):
    r = idx.shape[0]
    w = table.shape[1]
    per_w = _sc_worker_rows(r)
    mesh = plsc.VectorSubcoreMesh(core_axis_name="c", subcore_axis_name="s")

    @functools.partial(
        pl.kernel, mesh=mesh,
        out_type=jax.ShapeDtypeStruct((r, w), table.dtype),
        scratch_types=[pltpu.VMEM((1, SC_CHUNK), jnp.int32),
                       pltpu.VMEM((SC_CHUNK, w), table.dtype),
                       pltpu.SemaphoreType.DMA],
    )
    def gather(table_hbm, idx_hbm, out_hbm, idx_v, rows_v, sem):
        base = (lax.axis_index("s") * SC_CORES + lax.axis_index("c")) * per_w

        @pl.loop(0, per_w // SC_CHUNK)
        def _(g):
            off = base + g * SC_CHUNK
            pltpu.sync_copy(idx_hbm.at[pl.ds(0, 1), pl.ds(off, SC_CHUNK)], idx_v)
            pltpu.async_copy(table_hbm.at[idx_v.at[0]], rows_v, sem).wait()
            pltpu.sync_copy(rows_v, out_hbm.at[pl.ds(off, SC_CHUNK)])

    return gather(table, idx.reshape(1, r))


def _group_kernel(te_ref, nv_ref, xs_ref, wgu_ref, wd_ref, ys_ref):
    @pl.when(pl.program_id(0) < nv_ref[0])
    def _():
        hx = _unpack_halves(xs_ref[...]).astype(BF16)
        ys_ref[...] = _pack_halves(_swiglu_tile(hx, wgu_ref.at[0], wd_ref.at[0]))


def _grouped_swiglu(xs, tile_expert, n_valid, wgu, wd):
    r, wp = xs.shape
    grid_spec = pltpu.PrefetchScalarGridSpec(
        num_scalar_prefetch=2,
        grid=(r // GROUP_TILE,),
        in_specs=[pl.BlockSpec((GROUP_TILE, wp), lambda i, te, nv: (i, 0)),
                  pl.BlockSpec((1,) + wgu.shape[1:], lambda i, te, nv: (te[i], 0, 0)),
                  pl.BlockSpec((1,) + wd.shape[1:], lambda i, te, nv: (te[i], 0, 0))],
        out_specs=pl.BlockSpec((GROUP_TILE, wp), lambda i, te, nv: (i, 0)),
    )
    return pl.pallas_call(
        _group_kernel,
        grid_spec=grid_spec,
        out_shape=jax.ShapeDtypeStruct((r, wp), jnp.uint32),
        compiler_params=_params("arbitrary"),
        name="grouped_swiglu",
    )(tile_expert, n_valid, xs, wgu, wd)


def _combine_kernel(x_ref, y_ref, route_ref, mod_ref, fg_ref, *rest):
    o_ref = rest[-1]
    r = route_ref[0]
    moe = r[:, 2:3] * _unpack_halves(y_ref[0]) + r[:, 3:4] * _unpack_halves(y_ref[1])
    xn = x_ref[0] + mod_ref[0, 5:6, :] * moe
    o_ref[0] = _rms(xn) * fg_ref[...]


def _combine_final(x, y2, route, modv, final_g, bi, prev):
    b, n, d = x.shape
    tm = min(ROW_TILE, n)
    row = lambda w: pl.BlockSpec((1, tm, w), lambda j: (bi, j, 0))
    in_specs = [row(d), pl.BlockSpec((2, tm, d // 2), lambda j: (0, j, 0)), row(8),
                pl.BlockSpec((1, 8, d), lambda j: (bi, 0, 0)), _const_spec((1, d))]
    args = [x, y2, route, modv, final_g.reshape(1, d)]
    aliases = {}
    if prev is not None:
        in_specs.append(pl.BlockSpec(memory_space=pl.ANY))
        args.append(prev)
        aliases = {len(args) - 1: 0}
    return pl.pallas_call(
        _combine_kernel,
        grid=(n // tm,),
        in_specs=in_specs,
        out_specs=row(d),
        out_shape=jax.ShapeDtypeStruct((b, n, d), F32),
        input_output_aliases=aliases,
        compiler_params=_params("arbitrary"),
        name="moe_combine",
    )(*args)


def _moe_final(hxp, x, route, counts, modv, final_g, wgu, wd):
    b, n, d = x.shape
    t = b * n
    counts = counts[:N_EXPERTS, 0]
    padded = ((counts + GROUP_TILE - 1) // GROUP_TILE) * GROUP_TILE
    ends = jnp.cumsum(padded)
    starts = ends - padded
    rows = jnp.transpose(route, (1, 0, 2)).reshape(8, t)
    dest = starts[rows[0:2].astype(jnp.int32)] + rows[4:6].astype(jnp.int32)
    route = jnp.transpose(route, (0, 2, 1))
    n_rows = 2 * t + N_EXPERTS * GROUP_TILE
    tile_start = jnp.arange(n_rows // GROUP_TILE, dtype=jnp.int32) * GROUP_TILE
    tile_expert = jnp.minimum(jnp.sum(ends[None, :] <= tile_start[:, None], axis=1), N_EXPERTS - 1).astype(jnp.int32)
    n_valid = (ends[-1:] // GROUP_TILE).astype(jnp.int32)

    xs = _sc_scatter_rows(hxp.reshape(t, d // 2), dest, n_rows)
    ys = _grouped_swiglu(xs, tile_expert, n_valid, wgu, wd)
    out = None
    for bi in range(b):
        y2 = _sc_gather_rows(ys, dest[:, bi * n:(bi + 1) * n].reshape(2 * n))
        out = _combine_final(x, y2.reshape(2, n, d // 2), route, modv, final_g, bi, out)
    return out


def _in_weights(w):
    wm = jnp.concatenate([w[:, O_Q:O_K], w[:, O_V:O_G], w[:, O_F:P_IN], w[:, O_G:O_F],
                          jnp.zeros((w.shape[0], 128 - 4 * M_H), w.dtype)], axis=1).astype(BF16)
    return wm, w[:, O_K:O_V].T.astype(BF16), w[:, O_G:O_F].T.astype(BF16)


def _zero_state(b):
    return (jnp.zeros((b, M_H, M_DH, 2 * M_DH), F32), jnp.full((b, M_H, 8, 128), -1e30, F32))


def _mod_rows(m):
    r = m.shape[0]
    m = m.reshape(r, 6, D_MODEL)
    return jnp.concatenate([m, jnp.zeros((r, 2, D_MODEL), F32)], axis=1)


def kernel(x, c, ctx, c_ctx, norm1_g, norm2_g, w_mod, b_mod, w_in, b_gate, mlstm_norm_g, conv_dw, conv_db, conv_norm_g, conv_norm_b, w_out, ffn_w_gate_up, ffn_w_down, moe_w_router, moe_b_router, moe_w_gate_up, moe_w_down, final_norm_g):
    b = x.shape[0]
    depth = w_in.shape[0]
    assert depth == 2 and b <= 7
    cond8 = jnp.concatenate([c, c_ctx[None, :], jnp.zeros((7 - b, D_MODEL), F32)], axis=0)
    xc = ctx
    mods_all = _modulation(cond8, w_mod, b_mod)
    for layer in range(depth):
        last = layer == depth - 1
        mods = mods_all[layer]
        mod_lat = _mod_rows(mods[:b])
        mod_ctx = jnp.broadcast_to(_mod_rows(mods[b:b + 1]), (b, 8, D_MODEL))
        wm, wkt, wgt = _in_weights(w_in[layer])
        bg = b_gate[layer].reshape(16, 1)
        w_o = w_out[layer].astype(BF16)

        qvo_c, kt_c, zf_c, cc_c, gt_c, gc_c = _in_proj(xc, mod_ctx, norm1_g[layer], wm, wkt, wgt, bg)
        qvo_l, kt_l, zf_l, cc_l, gt_l, gc_l = _in_proj(x, mod_lat, norm1_g[layer], wm, wkt, wgt, bg)

        hf_c, hb_c, cf, cb, mf, mb = _mlstm(qvo_c, kt_c, gt_c, gc_c, _zero_state(b), _zero_state(b))
        hf_l, hb_l, _, _, _, _ = _mlstm(qvo_l, kt_l, gt_l, gc_l, (cf, mf), (cb, mb))

        conv_args = (conv_dw[layer], conv_db[layer], conv_norm_g[layer], conv_norm_b[layer])
        yf_l = _fourier(zf_l, x.shape[1])
        yc_l = _conv(cc_l, *conv_args)
        if not last:
            wgu = ffn_w_gate_up[layer // 2].astype(BF16)
            wd = ffn_w_down[layer // 2].astype(BF16)
            x = _out_ffn(hf_l, hb_l, qvo_l, yf_l, yc_l, x, mod_lat, mlstm_norm_g[layer],
                         norm2_g[layer], w_o, wgu, wd)
            yf_c = _fourier(zf_c, xc.shape[1])
            yc_c = _conv(cc_c, *conv_args)
            xc = _out_ffn(hf_c, hb_c, qvo_c, yf_c, yc_c, xc, mod_ctx, mlstm_norm_g[layer],
                          norm2_g[layer], w_o, wgu, wd)
        else:
            idx = layer // 2
            wrt = jnp.concatenate([moe_w_router[idx].T, jnp.zeros((16 - N_EXPERTS, D_MODEL), F32)], axis=0)
            brt = jnp.broadcast_to(jnp.concatenate([moe_b_router[idx], jnp.zeros((16 - N_EXPERTS,), F32)])[:, None],
                                   (16, 128))
            x, hxp, route, counts = _out_route(hf_l, hb_l, qvo_l, yf_l, yc_l, x, mod_lat,
                                               mlstm_norm_g[layer], norm2_g[layer], w_o, wrt, brt)
            x = _moe_final(hxp, x, route, counts, mod_lat, final_norm_g,
                           moe_w_gate_up[idx].astype(BF16), moe_w_down[idx].astype(BF16))
    return x
```

```python
import functools
import math

import numpy as np
import jax
import jax.numpy as jnp
from jax import lax
from jax.experimental import pallas as pl
from jax.experimental.pallas import tpu as pltpu
from jax.experimental.pallas import tpu_sc as plsc

F32 = jnp.float32
BF16 = jnp.bfloat16

D_MODEL = 1024
M_W = 512
M_H = 4
M_DH = 128
F_W = 256
F_G = 4
C_W = 256
C_G = 4
CONV_K = 31
FFN_HIDDEN = 2816
N_EXPERTS = 8
EPS = 1e-6
LOG2E = 1.4426950408889634

O_Q = 0
O_K = O_Q + M_W
O_V = O_K + M_W
O_O = O_V + M_W
O_G = O_O + M_W
O_F = O_G + 4 * M_H
O_CU = O_F + F_W
O_CG = O_CU + C_W
P_IN = O_CG + C_W

LC = 256
MLSTM_SUB = 4
DFT_N1 = 128
DFT_DIRECT_MAX = 512
DFT_GROUP = 16
ROW_TILE = 512
IN_TILE = 512
HID_TILE = 256
CONV_TILE = 256
CONV_PAD = 16
GROUP_TILE = 512
SC_CORES = 2
SC_SUBCORES = 16
SC_WORKERS = SC_CORES * SC_SUBCORES
SC_CHUNK = 128
VMEM_LIMIT = 56 * 1024 * 1024


def _params(*sem):
    return pltpu.CompilerParams(dimension_semantics=sem, vmem_limit_bytes=VMEM_LIMIT)


def _const_spec(shape):
    zeros = (0,) * len(shape)
    return pl.BlockSpec(shape, lambda *_: zeros, pipeline_mode=pl.Buffered(1))


def _dot(a, b):
    return jnp.dot(a, b, preferred_element_type=F32)


def _dot_nt(a, b):
    return lax.dot_general(a, b, (((1,), (1,)), ((), ())), preferred_element_type=F32)


def _split2(a):
    hi = a.astype(BF16)
    lo = (a - hi.astype(F32)).astype(BF16)
    return hi, lo


def _split3(a):
    x1 = a.astype(BF16)
    r1 = a - x1.astype(F32)
    x2 = r1.astype(BF16)
    x3 = (r1 - x2.astype(F32)).astype(BF16)
    return x1, x2, x3


def _dot_precise(a, b):
    ah, al = _split2(a)
    bh, bl = _split2(b)
    return _dot(ah, bh) + _dot(al, bh) + _dot(ah, bl)


def _silu(x):
    return x * jax.nn.sigmoid(x)


def _log_sigmoid(x):
    return -(jnp.maximum(-x, 0.0) + jnp.log1p(jnp.exp(-jnp.abs(x))))


def _rms(x):
    return x * lax.rsqrt(jnp.mean(x * x, axis=-1, keepdims=True) + EPS)


def _mod_kernel(c_ref, w_ref, b_ref, o_ref):
    o_ref[0] = _dot_precise(_silu(c_ref[...]), w_ref[0]) + b_ref[0]


def _modulation(cond8, w_mod, b_mod):
    d = cond8.shape[1]
    depth, _, n_out = w_mod.shape
    tn = 1536
    return pl.pallas_call(
        _mod_kernel,
        grid=(depth, n_out // tn),
        in_specs=[_const_spec((8, d)),
                  pl.BlockSpec((1, d, tn), lambda l, j: (l, 0, j)),
                  pl.BlockSpec((1, 1, tn), lambda l, j: (l, 0, j))],
        out_specs=pl.BlockSpec((1, 8, tn), lambda l, j: (l, 0, j)),
        out_shape=jax.ShapeDtypeStruct((depth, 8, n_out), F32),
        compiler_params=_params("arbitrary", "arbitrary"),
        name="modulation",
    )(cond8, w_mod, b_mod.reshape(depth, 1, n_out))


def _in_kernel(x_ref, mod_ref, g_ref, wm_ref, wkt_ref, wgt_ref, bg_ref, bgr_ref,
               qvo_ref, kt_ref, zf_ref, cc_ref, gt_ref, gc_ref, *scratch):
    tm = x_ref.shape[1]
    h = _rms(x_ref[0]) * g_ref[...]
    h = h * (1.0 + mod_ref[0, 1:2, :]) + mod_ref[0, 0:1, :]
    hb = h.astype(BF16)
    p = _dot(hb, wm_ref[...])
    kt = _dot_nt(wkt_ref[...], hb) * (M_DH ** -0.5)
    gt = _dot_nt(wgt_ref[...], hb) + bg_ref[...]
    o_f, o_c, o_g = 3 * M_W, 3 * M_W + F_W, 3 * M_W + F_W + 2 * C_W
    qvo_ref[0] = p[:, :o_f].astype(BF16)
    if scratch:
        zs_ref, = scratch
        rows = zf_ref.shape[1]
        n2cnt = tm // rows
        zs_ref[...] = p[:, o_f:o_c].reshape(rows, n2cnt, F_W)
        swapped = pltpu.einshape("ajc->jac", zs_ref[...])
        for j in range(n2cnt):
            zf_ref[0, :, j * F_W:(j + 1) * F_W] = swapped[j]
    else:
        zf_ref[0] = p[:, o_f:o_c].astype(BF16)
    cc_ref[0] = p[:, o_c:o_g].astype(BF16)
    kt_ref[0] = kt.astype(BF16)
    gc_ref[0] = p[:, o_g:] + bgr_ref[...]
    for j in range(tm // LC):
        gt_ref[0, j] = gt[:, j * LC:(j + 1) * LC]


def _in_proj(x, modv, norm_g, wm, wkt, wgt, bg):
    b, n, d = x.shape
    tm = min(IN_TILE, n)
    nm = wm.shape[1]
    bg_row = jnp.concatenate([bg.reshape(1, 16), jnp.zeros((1, 128 - 16), F32)], axis=1)
    if n > DFT_DIRECT_MAX:
        n2cnt = n // DFT_N1
        assert tm % n2cnt == 0 and (tm // n2cnt) % 8 == 0
        zf_sds = jax.ShapeDtypeStruct((b, DFT_N1, n2cnt * F_W), F32)
        zf_spec = pl.BlockSpec((1, tm // n2cnt, n2cnt * F_W), lambda i, j: (i, j, 0))
        scratch = [pltpu.VMEM((tm // n2cnt, n2cnt, F_W), F32)]
    else:
        zf_sds = jax.ShapeDtypeStruct((b, n, F_W), BF16)
        zf_spec = pl.BlockSpec((1, tm, F_W), lambda i, j: (i, j, 0))
        scratch = []
    out_shape = (
        jax.ShapeDtypeStruct((b, n, 3 * M_W), BF16),
        jax.ShapeDtypeStruct((b, M_W, n), BF16),
        zf_sds,
        jax.ShapeDtypeStruct((b, n, 2 * C_W), BF16),
        jax.ShapeDtypeStruct((b, n // LC, 16, LC), F32),
        jax.ShapeDtypeStruct((b, n, 128), F32),
    )
    return pl.pallas_call(
        _in_kernel,
        grid=(b, n // tm),
        in_specs=[pl.BlockSpec((1, tm, d), lambda i, j: (i, j, 0)),
                  pl.BlockSpec((1, 8, d), lambda i, j: (i, 0, 0)),
                  _const_spec((1, d)),
                  _const_spec((d, nm)),
                  _const_spec((M_W, d)),
                  _const_spec((16, d)),
                  _const_spec((16, 1)),
                  _const_spec((1, 128))],
        out_specs=(pl.BlockSpec((1, tm, 3 * M_W), lambda i, j: (i, j, 0)),
                   pl.BlockSpec((1, M_W, tm), lambda i, j: (i, 0, j)),
                   zf_spec,
                   pl.BlockSpec((1, tm, 2 * C_W), lambda i, j: (i, j, 0)),
                   pl.BlockSpec((1, tm // LC, 16, LC), lambda i, j: (i, j, 0, 0)),
                   pl.BlockSpec((1, tm, 128), lambda i, j: (i, j, 0))),
        out_shape=out_shape,
        scratch_shapes=scratch,
        compiler_params=_params("parallel", "arbitrary"),
        name="in_proj",
    )(x, modv, norm_g.reshape(1, d), wm, wkt, wgt, bg, bg_row)


def _mlstm_kernel(qf_ref, vf_ref, ktf_ref, gtf_ref, gcf_ref, qb_ref, vb_ref, ktb_ref, gtb_ref, gcb_ref,
                  cf0_ref, cb0_ref, mf0_ref, mb0_ref, u3_ref, lo3_ref, ut3_ref, lot3_ref,
                  hf_ref, hb_ref, cf_ref, cb_ref, mf_ref, mb_ref):
    @pl.when(pl.program_id(1) == 0)
    def _():
        cf_ref[...] = cf0_ref[...]
        cb_ref[...] = cb0_ref[...]
        mf_ref[...] = mf0_ref[...]
        mb_ref[...] = mb0_ref[...]

    sub = qf_ref.shape[1] // LC
    t_idx = lax.broadcasted_iota(jnp.int32, (LC, LC), 0)
    s_idx = lax.broadcasted_iota(jnp.int32, (LC, LC), 1)
    ones = jnp.ones((LC, M_DH), BF16)
    state = [[(c_ref[0, h], m_ref[0, h, 0:1, 0:1]) for h in range(M_H)] for c_ref, m_ref in
             ((cf_ref, mf_ref), (cb_ref, mb_ref))]
    h_out = []
    for k in range(sub):
        kf, kb = k, sub - 1 - k
        gf = gtf_ref[0, kf]
        gb = gtb_ref[0, kb]
        rows_f = _dot(jnp.concatenate(_split3(_log_sigmoid(gf)), axis=1), u3_ref[...])
        rows_b = _dot(jnp.concatenate(_split3(_log_sigmoid(gb)), axis=1), lo3_ref[...])
        cols_f = _dot(ut3_ref[...], jnp.concatenate(
            _split3(_log_sigmoid(gcf_ref[0, kf * LC:(kf + 1) * LC, :])), axis=0))
        cols_b = _dot(lot3_ref[...], jnp.concatenate(
            _split3(_log_sigmoid(gcb_ref[0, kb * LC:(kb + 1) * LC, :])), axis=0))
        dirs = (
            (0, qf_ref, vf_ref, ktf_ref, kf, gf, rows_f, cols_f, s_idx <= t_idx, hf_ref, 0, 4, LC - 1),
            (1, qb_ref, vb_ref, ktb_ref, kb, gb, rows_b, cols_b, s_idx >= t_idx, hb_ref, 8, 12, 0),
        )
        for d, q_ref, v_ref, kt_ref, kc, g, rows, cols, mask, h_ref, i_off, f_off, last in dirs:
            rs = slice(kc * LC, (kc + 1) * LC)
            for h in range(M_H):
                hs = slice(h * M_DH, (h + 1) * M_DH)
                q = q_ref[0, rs, hs]
                kt = kt_ref[0, hs, rs]
                vaug = jnp.concatenate([v_ref[0, rs, hs], ones], axis=1)
                i_row = g[i_off + h:i_off + h + 1, :]
                b_row = rows[f_off + h:f_off + h + 1, :]
                b_col = jnp.broadcast_to(cols[:, f_off + h:f_off + h + 1], (LC, M_DH))
                b_last = b_row[:, last:last + 1]
                c0, m0 = state[d][h]

                am = jnp.where(mask, (i_row - b_row) * LOG2E, -jnp.inf)
                g2 = jnp.maximum(m0 * LOG2E, jnp.max(am, axis=-1, keepdims=True))
                g2 = jnp.broadcast_to(g2, (LC, M_DH))
                e = jnp.exp2(am - jnp.concatenate([g2, g2], axis=1))
                s = (_dot(q, kt) * e).astype(BF16)
                qi = (q.astype(F32) * jnp.exp2(m0 * LOG2E - g2)).astype(BF16)
                na = _dot(s, vaug) + _dot(qi, c0.astype(BF16))
                floor = jnp.exp2(-(b_col * LOG2E + g2))
                h_new = (na[:, :M_DH] / jnp.maximum(jnp.abs(na[:, M_DH:]), floor)).astype(h_ref.dtype)
                h_out.append((h_ref, rs, hs, h_new))

                gs = b_last - b_row + i_row
                m_new = jnp.maximum(b_last + m0, jnp.max(gs, axis=-1, keepdims=True))
                kw = (kt.astype(F32) * jnp.exp(gs - m_new)).astype(BF16)
                state[d][h] = (jnp.exp(b_last + m0 - m_new) * c0 + _dot(kw, vaug), m_new)

    for h_ref, rs, hs, h_new in h_out:
        h_ref[0, rs, hs] = h_new
    for d, (c_ref, m_ref) in enumerate(((cf_ref, mf_ref), (cb_ref, mb_ref))):
        for h in range(M_H):
            c_ref[0, h] = state[d][h][0]
            m_ref[0, h] = jnp.broadcast_to(state[d][h][1], (8, 128))


def _tri_constants():
    s = np.arange(LC)[:, None]
    t = np.arange(LC)[None, :]
    u = (s <= t).astype(np.float32)
    lo = (s >= t).astype(np.float32)
    u3 = np.concatenate([u, u, u], axis=0)
    lo3 = np.concatenate([lo, lo, lo], axis=0)
    return (jnp.asarray(u3, BF16), jnp.asarray(lo3, BF16),
            jnp.asarray(u3.T.copy(), BF16), jnp.asarray(lo3.T.copy(), BF16))


def _mlstm(qvo, kt, gt, gc, state_f, state_b):
    b, n, _ = qvo.shape
    sub = min(MLSTM_SUB, n // LC)
    rows = sub * LC
    nc = n // rows
    cf0, mf0 = state_f
    cb0, mb0 = state_b
    fwd = lambda i, c: (i, c, 0)
    bwd = lambda i, c: (i, nc - 1 - c, 0)
    st_c = pl.BlockSpec((1, M_H, M_DH, 2 * M_DH), lambda i, c: (i, 0, 0, 0))
    st_m = pl.BlockSpec((1, M_H, 8, 128), lambda i, c: (i, 0, 0, 0))
    h_sds = jax.ShapeDtypeStruct((b, n, M_W), BF16)
    return pl.pallas_call(
        _mlstm_kernel,
        grid=(b, nc),
        in_specs=[pl.BlockSpec((1, rows, M_W), fwd),
                  pl.BlockSpec((1, rows, M_W), lambda i, c: (i, c, 1)),
                  pl.BlockSpec((1, M_W, rows), lambda i, c: (i, 0, c)),
                  pl.BlockSpec((1, sub, 16, LC), lambda i, c: (i, c, 0, 0)),
                  pl.BlockSpec((1, rows, 128), fwd),
                  pl.BlockSpec((1, rows, M_W), bwd),
                  pl.BlockSpec((1, rows, M_W), lambda i, c: (i, nc - 1 - c, 1)),
                  pl.BlockSpec((1, M_W, rows), lambda i, c: (i, 0, nc - 1 - c)),
                  pl.BlockSpec((1, sub, 16, LC), lambda i, c: (i, nc - 1 - c, 0, 0)),
                  pl.BlockSpec((1, rows, 128), bwd),
                  st_c, st_c, st_m, st_m,
                  _const_spec((3 * LC, LC)), _const_spec((3 * LC, LC)),
                  _const_spec((LC, 3 * LC)), _const_spec((LC, 3 * LC))],
        out_specs=(pl.BlockSpec((1, rows, M_W), fwd), pl.BlockSpec((1, rows, M_W), bwd),
                   st_c, st_c, st_m, st_m),
        out_shape=(h_sds, h_sds,
                   jax.ShapeDtypeStruct(cf0.shape, F32), jax.ShapeDtypeStruct(cb0.shape, F32),
                   jax.ShapeDtypeStruct(mf0.shape, F32), jax.ShapeDtypeStruct(mb0.shape, F32)),
        compiler_params=_params("parallel", "arbitrary"),
        name="mlstm",
    )(qvo, qvo, kt, gt, gc, qvo, qvo, kt, gt, gc, cf0, cb0, mf0, mb0, *_tri_constants())


def _channel_dft_matrix():
    gw = F_W // F_G
    j = np.arange(gw)[:, None]
    l = np.arange(gw)[None, :]
    ang = 2.0 * np.pi * j * l / gw
    eye = np.eye(F_G)
    bc = np.kron(eye, np.cos(ang)) / math.sqrt(gw)
    bs = np.kron(eye, np.sin(ang)) / math.sqrt(gw)
    return jnp.asarray(np.concatenate([bc, -bs], axis=1), BF16)


def _dft_cos_sin(n):
    k = np.arange(n)[:, None]
    m = np.arange(n)[None, :]
    ang = 2.0 * np.pi * ((k * m) % n) / n
    return np.cos(ang) / math.sqrt(n), np.sin(ang) / math.sqrt(n)


def _fourier1_kernel(z_ref, bcs_ref, f1_ref, o_ref, ts_ref):
    n2cnt = o_ref.shape[1] // 2

    def group(g, carry):
        for r in range(DFT_GROUP):
            j = g * DFT_GROUP + r
            off = pl.multiple_of(j * F_W, F_W)
            ab = _dot(z_ref[0, :, pl.ds(off, F_W)].astype(BF16), bcs_ref[...])
            u = jnp.concatenate([ab[:, :F_W], ab[:, F_W:]], axis=0).astype(BF16)
            t = _dot(f1_ref[j], u)
            ts_ref[0, r] = t[:DFT_N1]
            ts_ref[1, r] = t[DFT_N1:]
        for part in range(2):
            row0 = pl.multiple_of(part * n2cnt + g * DFT_GROUP, DFT_GROUP)
            swapped = pltpu.einshape("jkc->kjc", ts_ref[part])
            for k1 in range(DFT_N1):
                o_ref[0, pl.ds(row0, DFT_GROUP), k1 * F_W:(k1 + 1) * F_W] = swapped[k1].astype(BF16)
        return carry

    lax.fori_loop(0, n2cnt // DFT_GROUP, group, 0)


def _fourier2_kernel(t_ref, f2_ref, o_ref, s_ref):
    n2cnt = f2_ref.shape[0]
    width = t_ref.shape[2]
    tw = DFT_GROUP * F_W
    for c in range(width // tw):
        y = _dot(f2_ref[...], t_ref[0, :, c * tw:(c + 1) * tw])
        for kk in range(DFT_GROUP):
            s_ref[kk] = y[:, kk * F_W:(kk + 1) * F_W]
        swapped = pltpu.einshape("kjc->jkc", s_ref[...])
        for k2 in range(n2cnt):
            r0 = k2 * DFT_N1 + c * DFT_GROUP
            o_ref[0, r0:r0 + DFT_GROUP, :] = swapped[k2].astype(o_ref.dtype)


def _fourier_direct_kernel(z_ref, bcs_ref, f_ref, o_ref):
    ab = _dot(z_ref[0], bcs_ref[...])
    u = jnp.concatenate([ab[:, :F_W], ab[:, F_W:]], axis=0).astype(BF16)
    o_ref[0] = _dot(f_ref[...], u).astype(o_ref.dtype)


def _fourier(zf, n):
    b = zf.shape[0]
    bcs = _channel_dft_matrix()
    if n <= DFT_DIRECT_MAX:
        c, s = _dft_cos_sin(n)
        fmat = jnp.asarray(np.concatenate([c, s], axis=1), BF16)
        return pl.pallas_call(
            _fourier_direct_kernel,
            grid=(b,),
            in_specs=[pl.BlockSpec((1, n, F_W), lambda i: (i, 0, 0)),
                      _const_spec((F_W, 2 * F_W)), _const_spec((n, 2 * n))],
            out_specs=pl.BlockSpec((1, n, F_W), lambda i: (i, 0, 0)),
            out_shape=jax.ShapeDtypeStruct((b, n, F_W), BF16),
            compiler_params=_params("parallel"),
            name="fourier_direct",
        )(zf, bcs, fmat)

    n1 = DFT_N1
    n2 = n // n1
    assert n2 % DFT_GROUP == 0
    jj = np.arange(n2)[:, None, None]
    k1 = np.arange(n1)[None, :, None]
    aa = np.arange(n1)[None, None, :]
    ang = 2.0 * np.pi * (((jj * k1) % n) / n + ((k1 * aa) % n1) / n1)
    c1, s1 = np.cos(ang) / math.sqrt(n1), np.sin(ang) / math.sqrt(n1)
    f1 = jnp.asarray(np.concatenate([np.concatenate([c1, s1], axis=2),
                                     np.concatenate([-s1, c1], axis=2)], axis=1), BF16)
    c2, s2 = _dft_cos_sin(n2)
    f2 = jnp.asarray(np.concatenate([c2, s2], axis=1), BF16)
    width = n1 * F_W

    t = pl.pallas_call(
        _fourier1_kernel,
        grid=(b,),
        in_specs=[pl.BlockSpec((1, n1, n2 * F_W), lambda i: (i, 0, 0)),
                  _const_spec((F_W, 2 * F_W)), _const_spec((n2, 2 * n1, 2 * n1))],
        out_specs=pl.BlockSpec((1, 2 * n2, width), lambda i: (i, 0, 0)),
        out_shape=jax.ShapeDtypeStruct((b, 2 * n2, width), BF16),
        scratch_shapes=[pltpu.VMEM((2, DFT_GROUP, n1, F_W), F32)],
        compiler_params=_params("parallel"),
        name="fourier_stage1",
    )(zf, bcs, f1)
    return pl.pallas_call(
        _fourier2_kernel,
        grid=(b,),
        in_specs=[pl.BlockSpec((1, 2 * n2, width), lambda i: (i, 0, 0)),
                  _const_spec((n2, 2 * n2))],
        out_specs=pl.BlockSpec((1, n, F_W), lambda i: (i, 0, 0)),
        out_shape=jax.ShapeDtypeStruct((b, n, F_W), BF16),
        scratch_shapes=[pltpu.VMEM((DFT_GROUP, n2, F_W), F32)],
        compiler_params=_params("parallel"),
        name="fourier_stage2",
    )(t, f2)


def _group_mean(y, gm_ref):
    hi, lo = _split2(y)
    return _dot(hi, gm_ref[...]) + _dot(lo, gm_ref[...])


def _conv_kernel(cc_ref, dw_ref, db_ref, lg_ref, lb_ref, gm_ref, o_ref, a_ref, ph_ref):
    n = cc_ref.shape[1]
    rt = min(CONV_TILE, n)
    span = rt + 2 * CONV_PAD - 8
    a_ref[0:CONV_PAD, :] = jnp.zeros((CONV_PAD, C_W), F32)
    a_ref[n + CONV_PAD:n + 2 * CONV_PAD, :] = jnp.zeros((CONV_PAD, C_W), F32)

    def glu(i, carry):
        r = pl.multiple_of(i * rt, rt)
        blk = cc_ref[0, pl.ds(r, rt), :].astype(F32)
        a_ref[pl.ds(r + CONV_PAD, rt), :] = blk[:, :C_W] * jax.nn.sigmoid(blk[:, C_W:])
        return carry

    lax.fori_loop(0, n // rt, glu, 0)

    def conv(i, carry):
        r = pl.multiple_of(i * rt, rt)
        win = a_ref[pl.ds(r, rt + 2 * CONV_PAD), :]
        for s in range(1, 8):
            ph_ref[s - 1] = win[s:s + span]
        acc = jnp.broadcast_to(db_ref[...], (rt, C_W))
        for j in range(CONV_K):
            off = CONV_PAD - CONV_K // 2 + j
            base = 8 * (off // 8)
            if off % 8 == 0:
                tap = a_ref[pl.ds(r + base, rt), :]
            else:
                tap = ph_ref[off % 8 - 1, base:base + rt, :]
            acc = acc + tap * dw_ref[j:j + 1, :]
        d = acc - _group_mean(acc, gm_ref)
        var = _group_mean(d * d, gm_ref)
        yn = d * lax.rsqrt(var + EPS) * lg_ref[...] + lb_ref[...]
        o_ref[0, pl.ds(r, rt), :] = _silu(yn).astype(o_ref.dtype)
        return carry

    lax.fori_loop(0, n // rt, conv, 0)


def _conv(cc, dw, db, ln_g, ln_b):
    b, n, _ = cc.shape
    gw = C_W // C_G
    gm = jnp.asarray(np.kron(np.eye(C_G), np.full((gw, gw), 1.0 / gw)), BF16)
    dw32 = jnp.concatenate([dw, jnp.zeros((32 - CONV_K, C_W), F32)], axis=0)
    return pl.pallas_call(
        _conv_kernel,
        grid=(b,),
        in_specs=[pl.BlockSpec((1, n, 2 * C_W), lambda i: (i, 0, 0)),
                  _const_spec((32, C_W)), _const_spec((1, C_W)), _const_spec((1, C_W)),
                  _const_spec((1, C_W)), _const_spec((C_W, C_W))],
        out_specs=pl.BlockSpec((1, n, C_W), lambda i: (i, 0, 0)),
        out_shape=jax.ShapeDtypeStruct((b, n, C_W), BF16),
        scratch_shapes=[pltpu.VMEM((n + 2 * CONV_PAD, C_W), F32),
                        pltpu.VMEM((7, min(CONV_TILE, n) + 2 * CONV_PAD - 8, C_W), F32)],
        compiler_params=_params("parallel"),
        name="conv_module",
    )(cc, dw32, db.reshape(1, C_W), ln_g.reshape(1, C_W), ln_b.reshape(1, C_W), gm)


def _top2_route(logits):
    row = lax.broadcasted_iota(jnp.int32, logits.shape, 0).astype(F32)
    lg = jnp.where(row < N_EXPERTS, logits, -jnp.inf)
    v1 = jnp.max(lg, axis=0, keepdims=True)
    i1 = jnp.min(jnp.where(lg == v1, row, 16.0), axis=0, keepdims=True)
    lg2 = jnp.where(row == i1, -jnp.inf, lg)
    v2 = jnp.max(lg2, axis=0, keepdims=True)
    i2 = jnp.min(jnp.where(lg2 == v2, row, 16.0), axis=0, keepdims=True)
    e2 = jnp.exp(v2 - v1)
    return row, i1, i2, 1.0 / (1.0 + e2), e2 / (1.0 + e2)


def _pack_halves(y):
    w = y.shape[1] // 2
    lo = pltpu.bitcast(y[:, :w].astype(BF16).astype(F32), jnp.uint32)
    hi = pltpu.bitcast(y[:, w:].astype(BF16).astype(F32), jnp.uint32)
    return (hi & jnp.uint32(0xFFFF0000)) | (lo >> 16)


def _unpack_halves(p):
    lo = pltpu.bitcast(p << 16, F32)
    hi = pltpu.bitcast(p & jnp.uint32(0xFFFF0000), F32)
    return jnp.concatenate([lo, hi], axis=1)


def _merge_heads(hf_ref, hb_ref, o_ref, yf_ref, yc_ref, x_ref, mod_ref, hg_ref, g2_ref, wo_ref):
    hsum = hf_ref[0].astype(F32) + hb_ref[0].astype(F32)
    heads = [_rms(hsum[:, h * M_DH:(h + 1) * M_DH]) for h in range(M_H)]
    ym = jax.nn.sigmoid(o_ref[0].astype(F32)) * (jnp.concatenate(heads, axis=1) * hg_ref[...])
    proj = (_dot(ym.astype(BF16), wo_ref[0:M_W, :])
            + _dot(yf_ref[0], wo_ref[M_W:M_W + F_W, :])
            + _dot(yc_ref[0], wo_ref[M_W + F_W:, :]))
    xn = x_ref[0] + mod_ref[0, 2:3, :] * proj
    hx = (_rms(xn) * g2_ref[...]) * (1.0 + mod_ref[0, 4:5, :]) + mod_ref[0, 3:4, :]
    return xn, hx


def _out_ffn_kernel(hf_ref, hb_ref, o_ref, yf_ref, yc_ref, x_ref, mod_ref, hg_ref, g2_ref, wo_ref,
                    wgu_ref, wd_ref, xo_ref):
    xn, hx = _merge_heads(hf_ref, hb_ref, o_ref, yf_ref, yc_ref, x_ref, mod_ref, hg_ref, g2_ref, wo_ref)
    xo_ref[0] = xn + mod_ref[0, 5:6, :] * _swiglu_tile(hx.astype(BF16), wgu_ref, wd_ref)


def _out_kernel(hf_ref, hb_ref, o_ref, yf_ref, yc_ref, x_ref, mod_ref, hg_ref, g2_ref, wo_ref,
                wrt_ref, brt_ref, tri_ref, xo_ref, hx_ref, route_ref, cnt_ref, carry_ref):
    xn, hx = _merge_heads(hf_ref, hb_ref, o_ref, yf_ref, yc_ref, x_ref, mod_ref, hg_ref, g2_ref, wo_ref)
    hh, hl = _split2(hx)
    wh, wl = _split2(wrt_ref[...])
    logits = _dot_nt(wh, hh) + _dot_nt(wl, hh) + _dot_nt(wh, hl) + brt_ref[:, 0:1]
    row, i1, i2, w1, w2 = _top2_route(logits)

    @pl.when((pl.program_id(0) == 0) & (pl.program_id(1) == 0))
    def _():
        carry_ref[...] = jnp.zeros_like(carry_ref)

    hit = jnp.where((row == i1) | (row == i2), 1.0, 0.0)
    before = _dot(hit.astype(BF16), tri_ref[...]) + carry_ref[:, 0:1]
    rank1 = jnp.sum(jnp.where(row == i1, before, 0.0), axis=0, keepdims=True)
    rank2 = jnp.sum(jnp.where(row == i2, before, 0.0), axis=0, keepdims=True)
    total = carry_ref[...] + jnp.sum(hit, axis=1, keepdims=True)
    xo_ref[0] = xn
    hx_ref[0] = _pack_halves(hx)
    zero = jnp.zeros_like(w1)
    route_ref[0] = jnp.concatenate([i1, i2, w1, w2, rank1, rank2, zero, zero], axis=0)
    carry_ref[...] = total
    cnt_ref[...] = total.astype(jnp.int32)


def _merge_specs(hf, hb, qvo, yf, yc, x, modv, head_g, norm2_g, w_out):
    b, n, d = x.shape
    tm = min(ROW_TILE, n)
    row = lambda w: pl.BlockSpec((1, tm, w), lambda i, j: (i, j, 0))
    in_specs = [row(M_W), row(M_W),
                pl.BlockSpec((1, tm, M_W), lambda i, j: (i, j, 2)),
                row(F_W), row(C_W), row(d),
                pl.BlockSpec((1, 8, d), lambda i, j: (i, 0, 0)),
                _const_spec((1, M_W)), _const_spec((1, d)), _const_spec((d, d))]
    args = [hf, hb, qvo, yf, yc, x, modv, head_g.reshape(1, M_W), norm2_g.reshape(1, d), w_out]
    return tm, row, in_specs, args


def _out_ffn(hf, hb, qvo, yf, yc, x, modv, head_g, norm2_g, w_out, wgu, wd):
    b, n, d = x.shape
    tm, row, in_specs, args = _merge_specs(hf, hb, qvo, yf, yc, x, modv, head_g, norm2_g, w_out)
    return pl.pallas_call(
        _out_ffn_kernel,
        grid=(b, n // tm),
        in_specs=in_specs + [_const_spec(wgu.shape), _const_spec(wd.shape)],
        out_specs=row(d),
        out_shape=jax.ShapeDtypeStruct((b, n, d), F32),
        compiler_params=_params("parallel", "arbitrary"),
        name="out_ffn",
    )(*args, wgu, wd)


def _out_route(hf, hb, qvo, yf, yc, x, modv, head_g, norm2_g, w_out, wrt, brt):
    b, n, d = x.shape
    tm, row, in_specs, args = _merge_specs(hf, hb, qvo, yf, yc, x, modv, head_g, norm2_g, w_out)
    tri = jnp.asarray(np.triu(np.ones((tm, tm), np.float32), 1), BF16)
    return pl.pallas_call(
        _out_kernel,
        grid=(b, n // tm),
        in_specs=in_specs + [_const_spec((16, d)), _const_spec((16, 128)), _const_spec((tm, tm))],
        out_specs=(row(d), row(d // 2), pl.BlockSpec((1, 8, tm), lambda i, j: (i, 0, j)),
                   _const_spec((16, 128))),
        out_shape=(jax.ShapeDtypeStruct((b, n, d), F32), jax.ShapeDtypeStruct((b, n, d // 2), jnp.uint32),
                   jax.ShapeDtypeStruct((b, 8, n), F32), jax.ShapeDtypeStruct((16, 128), jnp.int32)),
        scratch_shapes=[pltpu.VMEM((16, 128), F32)],
        compiler_params=_params("arbitrary", "arbitrary"),
        name="out_route",
    )(*args, wrt, brt, tri)


def _swiglu_tile(hx, wgu_ref, wd_ref):
    acc = jnp.zeros((hx.shape[0], D_MODEL), F32)
    for j in range(FFN_HIDDEN // HID_TILE):
        cs = slice(j * HID_TILE, (j + 1) * HID_TILE)
        us = slice(FFN_HIDDEN + j * HID_TILE, FFN_HIDDEN + (j + 1) * HID_TILE)
        a = _silu(_dot(hx, wgu_ref[:, cs])) * _dot(hx, wgu_ref[:, us])
        acc = acc + _dot(a.astype(BF16), wd_ref[cs, :])
    return acc


def _sc_worker_rows(n_rows):
    assert n_rows % (SC_WORKERS * SC_CHUNK) == 0
    return n_rows // SC_WORKERS


def _sc_scatter_rows(x, dest, n_out):
    t, w = x.shape
    per_w = _sc_worker_rows(t)
    mesh = plsc.VectorSubcoreMesh(core_axis_name="c", subcore_axis_name="s")

    @functools.partial(
        pl.kernel, mesh=mesh,
        out_type=jax.ShapeDtypeStruct((n_out, w), x.dtype),
        scratch_types=[pltpu.VMEM((1, SC_CHUNK), jnp.int32),
                       pltpu.VMEM((SC_CHUNK, w), x.dtype),
                       pltpu.SemaphoreType.DMA],
    )
    def scatter(x_hbm, dest_hbm, out_hbm, idx_v, rows_v, sem):
        base = (lax.axis_index("s") * SC_CORES + lax.axis_index("c")) * per_w

        @pl.loop(0, per_w // SC_CHUNK)
        def _(g):
            off = base + g * SC_CHUNK
            pltpu.sync_copy(x_hbm.at[pl.ds(off, SC_CHUNK)], rows_v)
            for k in range(2):
                pltpu.sync_copy(dest_hbm.at[pl.ds(k, 1), pl.ds(off, SC_CHUNK)], idx_v)
                pltpu.async_copy(rows_v, out_hbm.at[idx_v.at[0]], sem).wait()

    return scatter(x, dest)


def _sc_gather_rows(table, idx):
    r = idx.shape[0]
    w = table.shape[1]
    per_w = _sc_worker_rows(r)
    mesh = plsc.VectorSubcoreMesh(core_axis_name="c", subcore_axis_name="s")

    @functools.partial(
        pl.kernel, mesh=mesh,
        out_type=jax.ShapeDtypeStruct((r, w), table.dtype),
        scratch_types=[pltpu.VMEM((1, SC_CHUNK), jnp.int32),
                       pltpu.VMEM((SC_CHUNK, w), table.dtype),
                       pltpu.SemaphoreType.DMA],
    )
    def gather(table_hbm, idx_hbm, out_hbm, idx_v, rows_v, sem):
        base = (lax.axis_index("s") * SC_CORES + lax.axis_index("c")) * per_w

        @pl.loop(0, per_w // SC_CHUNK)
        def _(g):
            off = base + g * SC_CHUNK
            pltpu.sync_copy(idx_hbm.at[pl.ds(0, 1), pl.ds(off, SC_CHUNK)], idx_v)
            pltpu.async_copy(table_hbm.at[idx_v.at[0]], rows_v, sem).wait()
            pltpu.sync_copy(rows_v, out_hbm.at[pl.ds(off, SC_CHUNK)])

    return gather(table, idx.reshape(1, r))


def _group_kernel(te_ref, nv_ref, xs_ref, wgu_ref, wd_ref, ys_ref):
    @pl.when(pl.program_id(0) < nv_ref[0])
    def _():
        hx = _unpack_halves(xs_ref[...]).astype(BF16)
        ys_ref[...] = _pack_halves(_swiglu_tile(hx, wgu_ref.at[0], wd_ref.at[0]))


def _grouped_swiglu(xs, tile_expert, n_valid, wgu, wd):
    r, wp = xs.shape
    grid_spec = pltpu.PrefetchScalarGridSpec(
        num_scalar_prefetch=2,
        grid=(r // GROUP_TILE,),
        in_specs=[pl.BlockSpec((GROUP_TILE, wp), lambda i, te, nv: (i, 0)),
                  pl.BlockSpec((1,) + wgu.shape[1:], lambda i, te, nv: (te[i], 0, 0)),
                  pl.BlockSpec((1,) + wd.shape[1:], lambda i, te, nv: (te[i], 0, 0))],
        out_specs=pl.BlockSpec((GROUP_TILE, wp), lambda i, te, nv: (i, 0)),
    )
    return pl.pallas_call(
        _group_kernel,
        grid_spec=grid_spec,
        out_shape=jax.ShapeDtypeStruct((r, wp), jnp.uint32),
        compiler_params=_params("arbitrary"),
        name="grouped_swiglu",
    )(tile_expert, n_valid, xs, wgu, wd)


def _combine_kernel(x_ref, y_ref, route_ref, mod_ref, fg_ref, *rest):
    o_ref = rest[-1]
    r = route_ref[0]
    moe = r[:, 2:3] * _unpack_halves(y_ref[0]) + r[:, 3:4] * _unpack_halves(y_ref[1])
    xn = x_ref[0] + mod_ref[0, 5:6, :] * moe
    o_ref[0] = _rms(xn) * fg_ref[...]


def _combine_final(x, y2, route, modv, final_g, bi, prev):
    b, n, d = x.shape
    tm = min(ROW_TILE, n)
    row = lambda w: pl.BlockSpec((1, tm, w), lambda j: (bi, j, 0))
    in_specs = [row(d), pl.BlockSpec((2, tm, d // 2), lambda j: (0, j, 0)), row(8),
                pl.BlockSpec((1, 8, d), lambda j: (bi, 0, 0)), _const_spec((1, d))]
    args = [x, y2, route, modv, final_g.reshape(1, d)]
    aliases = {}
    if prev is not None:
        in_specs.append(pl.BlockSpec(memory_space=pl.ANY))
        args.append(prev)
        aliases = {len(args) - 1: 0}
    return pl.pallas_call(
        _combine_kernel,
        grid=(n // tm,),
        in_specs=in_specs,
        out_specs=row(d),
        out_shape=jax.ShapeDtypeStruct((b, n, d), F32),
        input_output_aliases=aliases,
        compiler_params=_params("arbitrary"),
        name="moe_combine",
    )(*args)


def _moe_final(hxp, x, route, counts, modv, final_g, wgu, wd):
    b, n, d = x.shape
    t = b * n
    counts = counts[:N_EXPERTS, 0]
    padded = ((counts + GROUP_TILE - 1) // GROUP_TILE) * GROUP_TILE
    ends = jnp.cumsum(padded)
    starts = ends - padded
    rows = jnp.transpose(route, (1, 0, 2)).reshape(8, t)
    experts = rows[0:2].astype(jnp.int32)
    dest = rows[4:6].astype(jnp.int32)
    for e in range(N_EXPERTS):
        dest = dest + jnp.where(experts == e, starts[e], 0)
    route = jnp.transpose(route, (0, 2, 1))
    n_rows = 2 * t + N_EXPERTS * GROUP_TILE
    tile_start = jnp.arange(n_rows // GROUP_TILE, dtype=jnp.int32) * GROUP_TILE
    tile_expert = jnp.minimum(jnp.sum(ends[None, :] <= tile_start[:, None], axis=1), N_EXPERTS - 1).astype(jnp.int32)
    n_valid = (ends[-1:] // GROUP_TILE).astype(jnp.int32)

    xs = _sc_scatter_rows(hxp.reshape(t, d // 2), dest, n_rows)
    ys = _grouped_swiglu(xs, tile_expert, n_valid, wgu, wd)
    out = None
    for bi in range(b):
        y2 = _sc_gather_rows(ys, dest[:, bi * n:(bi + 1) * n].reshape(2 * n))
        out = _combine_final(x, y2.reshape(2, n, d // 2), route, modv, final_g, bi, out)
    return out


def _in_weights(w):
    wm = jnp.concatenate([w[:, O_Q:O_K], w[:, O_V:O_G], w[:, O_F:P_IN], w[:, O_G:O_F],
                          jnp.zeros((w.shape[0], 128 - 4 * M_H), w.dtype)], axis=1).astype(BF16)
    return wm, w[:, O_K:O_V].T.astype(BF16), w[:, O_G:O_F].T.astype(BF16)


def _zero_state(b):
    return (jnp.zeros((b, M_H, M_DH, 2 * M_DH), F32), jnp.full((b, M_H, 8, 128), -1e30, F32))


def _mod_rows(m):
    r = m.shape[0]
    m = m.reshape(r, 6, D_MODEL)
    return jnp.concatenate([m, jnp.zeros((r, 2, D_MODEL), F32)], axis=1)


def kernel(x, c, ctx, c_ctx, norm1_g, norm2_g, w_mod, b_mod, w_in, b_gate, mlstm_norm_g, conv_dw, conv_db, conv_norm_g, conv_norm_b, w_out, ffn_w_gate_up, ffn_w_down, moe_w_router, moe_b_router, moe_w_gate_up, moe_w_down, final_norm_g):
    b = x.shape[0]
    depth = w_in.shape[0]
    assert depth == 2 and b <= 7
    cond8 = jnp.concatenate([c, c_ctx[None, :], jnp.zeros((7 - b, D_MODEL), F32)], axis=0)
    xc = ctx
    mods_all = _modulation(cond8, w_mod, b_mod)
    for layer in range(depth):
        last = layer == depth - 1
        mods = mods_all[layer]
        mod_lat = _mod_rows(mods[:b])
        mod_ctx = jnp.broadcast_to(_mod_rows(mods[b:b + 1]), (b, 8, D_MODEL))
        wm, wkt, wgt = _in_weights(w_in[layer])
        bg = b_gate[layer].reshape(16, 1)
        w_o = w_out[layer].astype(BF16)

        qvo_c, kt_c, zf_c, cc_c, gt_c, gc_c = _in_proj(xc, mod_ctx, norm1_g[layer], wm, wkt, wgt, bg)
        qvo_l, kt_l, zf_l, cc_l, gt_l, gc_l = _in_proj(x, mod_lat, norm1_g[layer], wm, wkt, wgt, bg)

        hf_c, hb_c, cf, cb, mf, mb = _mlstm(qvo_c, kt_c, gt_c, gc_c, _zero_state(b), _zero_state(b))
        hf_l, hb_l, _, _, _, _ = _mlstm(qvo_l, kt_l, gt_l, gc_l, (cf, mf), (cb, mb))

        conv_args = (conv_dw[layer], conv_db[layer], conv_norm_g[layer], conv_norm_b[layer])
        yf_l = _fourier(zf_l, x.shape[1])
        yc_l = _conv(cc_l, *conv_args)
        if not last:
            wgu = ffn_w_gate_up[layer // 2].astype(BF16)
            wd = ffn_w_down[layer // 2].astype(BF16)
            x = _out_ffn(hf_l, hb_l, qvo_l, yf_l, yc_l, x, mod_lat, mlstm_norm_g[layer],
                         norm2_g[layer], w_o, wgu, wd)
            yf_c = _fourier(zf_c, xc.shape[1])
            yc_c = _conv(cc_c, *conv_args)
            xc = _out_ffn(hf_c, hb_c, qvo_c, yf_c, yc_c, xc, mod_ctx, mlstm_norm_g[layer],
                          norm2_g[layer], w_o, wgu, wd)
        else:
            idx = layer // 2
            wrt = jnp.concatenate([moe_w_router[idx].T, jnp.zeros((16 - N_EXPERTS, D_MODEL), F32)], axis=0)
            brt = jnp.broadcast_to(jnp.concatenate([moe_b_router[idx], jnp.zeros((16 - N_EXPERTS,), F32)])[:, None],
                                   (16, 128))
            x, hxp, route, counts = _out_route(hf_l, hb_l, qvo_l, yf_l, yc_l, x, mod_lat,
                                               mlstm_norm_g[layer], norm2_g[layer], w_o, wrt, brt)
            x = _moe_final(hxp, x, route, counts, mod_lat, final_norm_g,
                           moe_w_gate_up[idx].astype(BF16), moe_w_down[idx].astype(BF16))
    return x
```

```python
import functools
import math

import numpy as np
import jax
import jax.numpy as jnp
from jax import lax
from jax.experimental import pallas as pl
from jax.experimental.pallas import tpu as pltpu
from jax.experimental.pallas import tpu_sc as plsc

F32 = jnp.float32
BF16 = jnp.bfloat16

D_MODEL = 1024
M_W = 512
M_H = 4
M_DH = 128
F_W = 256
F_G = 4
C_W = 256
C_G = 4
CONV_K = 31
FFN_HIDDEN = 2816
N_EXPERTS = 8
EPS = 1e-6
LOG2E = 1.4426950408889634

O_Q = 0
O_K = O_Q + M_W
O_V = O_K + M_W
O_O = O_V + M_W
O_G = O_O + M_W
O_F = O_G + 4 * M_H
O_CU = O_F + F_W
O_CG = O_CU + C_W
P_IN = O_CG + C_W

LC = 256
MLSTM_SUB = 8
DFT_N1 = 128
DFT_DIRECT_MAX = 512
DFT_GROUP = 16
ROW_TILE = 512
IN_TILE = 512
HID_TILE = 256
CONV_TILE = 256
CONV_PAD = 16
GROUP_TILE = 512
SC_CORES = 2
SC_SUBCORES = 16
SC_WORKERS = SC_CORES * SC_SUBCORES
SC_CHUNK = 128
VMEM_LIMIT = 56 * 1024 * 1024


def _params(*sem):
    return pltpu.CompilerParams(dimension_semantics=sem, vmem_limit_bytes=VMEM_LIMIT)


def _const_spec(shape):
    zeros = (0,) * len(shape)
    return pl.BlockSpec(shape, lambda *_: zeros, pipeline_mode=pl.Buffered(1))


def _dot(a, b):
    return jnp.dot(a, b, preferred_element_type=F32)


def _dot_nt(a, b):
    return lax.dot_general(a, b, (((1,), (1,)), ((), ())), preferred_element_type=F32)


def _split2(a):
    hi = a.astype(BF16)
    lo = (a - hi.astype(F32)).astype(BF16)
    return hi, lo


def _split3(a):
    x1 = a.astype(BF16)
    r1 = a - x1.astype(F32)
    x2 = r1.astype(BF16)
    x3 = (r1 - x2.astype(F32)).astype(BF16)
    return x1, x2, x3


def _dot_precise(a, b):
    ah, al = _split2(a)
    bh, bl = _split2(b)
    return _dot(ah, bh) + _dot(al, bh) + _dot(ah, bl)


def _silu(x):
    return x * jax.nn.sigmoid(x)


def _log_sigmoid(x):
    return -(jnp.maximum(-x, 0.0) + jnp.log1p(jnp.exp(-jnp.abs(x))))


def _rms(x):
    return x * lax.rsqrt(jnp.mean(x * x, axis=-1, keepdims=True) + EPS)


def _mod_kernel(c_ref, w_ref, b_ref, o_ref):
    o_ref[0] = _dot_precise(_silu(c_ref[...]), w_ref[0]) + b_ref[0]


def _modulation(cond8, w_mod, b_mod):
    d = cond8.shape[1]
    depth, _, n_out = w_mod.shape
    tn = 1536
    return pl.pallas_call(
        _mod_kernel,
        grid=(depth, n_out // tn),
        in_specs=[_const_spec((8, d)),
                  pl.BlockSpec((1, d, tn), lambda l, j: (l, 0, j)),
                  pl.BlockSpec((1, 1, tn), lambda l, j: (l, 0, j))],
        out_specs=pl.BlockSpec((1, 8, tn), lambda l, j: (l, 0, j)),
        out_shape=jax.ShapeDtypeStruct((depth, 8, n_out), F32),
        compiler_params=_params("arbitrary", "arbitrary"),
        name="modulation",
    )(cond8, w_mod, b_mod.reshape(depth, 1, n_out))


def _in_kernel(x_ref, mod_ref, g_ref, wm_ref, wkt_ref, wgt_ref, bg_ref,
               qvo_ref, kt_ref, zf_ref, cc_ref, gt_ref, gc_ref, *scratch):
    tm = x_ref.shape[1]
    h = _rms(x_ref[0]) * g_ref[...]
    h = h * (1.0 + mod_ref[0, 1:2, :]) + mod_ref[0, 0:1, :]
    hb = h.astype(BF16)
    p = _dot(hb, wm_ref[...])
    kt = _dot_nt(wkt_ref[...], hb) * (M_DH ** -0.5)
    gt = _dot_nt(wgt_ref[...], hb) + bg_ref[...]
    o_f, o_c, o_g = 3 * M_W, 3 * M_W + F_W, 3 * M_W + F_W + 2 * C_W
    qvo_ref[0] = p[:, :o_f].astype(BF16)
    if scratch:
        zs_ref, = scratch
        rows = zf_ref.shape[1]
        n2cnt = tm // rows
        zs_ref[...] = p[:, o_f:o_c].reshape(rows, n2cnt, F_W)
        swapped = pltpu.einshape("ajc->jac", zs_ref[...])
        for j in range(n2cnt):
            zf_ref[0, :, j * F_W:(j + 1) * F_W] = swapped[j]
    else:
        zf_ref[0] = p[:, o_f:o_c].astype(BF16)
    cc_ref[0] = p[:, o_c:o_g].astype(BF16)
    kt_ref[0] = kt.astype(BF16)
    gc_ref[0] = jnp.concatenate([gt, jnp.zeros((128 - 16, tm), F32)], axis=0).T
    for j in range(tm // LC):
        gt_ref[0, j] = gt[:, j * LC:(j + 1) * LC]


def _in_proj(x, modv, norm_g, wm, wkt, wgt, bg):
    b, n, d = x.shape
    tm = min(IN_TILE, n)
    nm = wm.shape[1]
    if n > DFT_DIRECT_MAX:
        n2cnt = n // DFT_N1
        assert tm % n2cnt == 0 and (tm // n2cnt) % 8 == 0
        zf_sds = jax.ShapeDtypeStruct((b, DFT_N1, n2cnt * F_W), F32)
        zf_spec = pl.BlockSpec((1, tm // n2cnt, n2cnt * F_W), lambda i, j: (i, j, 0))
        scratch = [pltpu.VMEM((tm // n2cnt, n2cnt, F_W), F32)]
    else:
        zf_sds = jax.ShapeDtypeStruct((b, n, F_W), BF16)
        zf_spec = pl.BlockSpec((1, tm, F_W), lambda i, j: (i, j, 0))
        scratch = []
    out_shape = (
        jax.ShapeDtypeStruct((b, n, 3 * M_W), BF16),
        jax.ShapeDtypeStruct((b, M_W, n), BF16),
        zf_sds,
        jax.ShapeDtypeStruct((b, n, 2 * C_W), BF16),
        jax.ShapeDtypeStruct((b, n // LC, 16, LC), F32),
        jax.ShapeDtypeStruct((b, n, 128), F32),
    )
    return pl.pallas_call(
        _in_kernel,
        grid=(b, n // tm),
        in_specs=[pl.BlockSpec((1, tm, d), lambda i, j: (i, j, 0)),
                  pl.BlockSpec((1, 8, d), lambda i, j: (i, 0, 0)),
                  _const_spec((1, d)),
                  _const_spec((d, nm)),
                  _const_spec((M_W, d)),
                  _const_spec((16, d)),
                  _const_spec((16, 1))],
        out_specs=(pl.BlockSpec((1, tm, 3 * M_W), lambda i, j: (i, j, 0)),
                   pl.BlockSpec((1, M_W, tm), lambda i, j: (i, 0, j)),
                   zf_spec,
                   pl.BlockSpec((1, tm, 2 * C_W), lambda i, j: (i, j, 0)),
                   pl.BlockSpec((1, tm // LC, 16, LC), lambda i, j: (i, j, 0, 0)),
                   pl.BlockSpec((1, tm, 128), lambda i, j: (i, j, 0))),
        out_shape=out_shape,
        scratch_shapes=scratch,
        compiler_params=_params("parallel", "arbitrary"),
        name="in_proj",
    )(x, modv, norm_g.reshape(1, d), wm, wkt, wgt, bg)


def _mlstm_kernel(qf_ref, vf_ref, ktf_ref, gtf_ref, gcf_ref, qb_ref, vb_ref, ktb_ref, gtb_ref, gcb_ref,
                  cf0_ref, cb0_ref, mf0_ref, mb0_ref, u3_ref, lo3_ref, ut3_ref, lot3_ref,
                  hf_ref, hb_ref, cf_ref, cb_ref, mf_ref, mb_ref):
    @pl.when(pl.program_id(1) == 0)
    def _():
        cf_ref[...] = cf0_ref[...]
        cb_ref[...] = cb0_ref[...]
        mf_ref[...] = mf0_ref[...]
        mb_ref[...] = mb0_ref[...]

    sub = qf_ref.shape[1] // LC
    t_idx = lax.broadcasted_iota(jnp.int32, (LC, LC), 0)
    s_idx = lax.broadcasted_iota(jnp.int32, (LC, LC), 1)
    ones = jnp.ones((LC, M_DH), BF16)
    state = [[(c_ref[0, h], m_ref[0, h, 0:1, 0:1]) for h in range(M_H)] for c_ref, m_ref in
             ((cf_ref, mf_ref), (cb_ref, mb_ref))]
    h_out = []
    for k in range(sub):
        kf, kb = k, sub - 1 - k
        gf = gtf_ref[0, kf]
        gb = gtb_ref[0, kb]
        rows_f = _dot(jnp.concatenate(_split3(_log_sigmoid(gf)), axis=1), u3_ref[...])
        rows_b = _dot(jnp.concatenate(_split3(_log_sigmoid(gb)), axis=1), lo3_ref[...])
        cols_f = _dot(ut3_ref[...], jnp.concatenate(
            _split3(_log_sigmoid(gcf_ref[0, kf * LC:(kf + 1) * LC, :])), axis=0))
        cols_b = _dot(lot3_ref[...], jnp.concatenate(
            _split3(_log_sigmoid(gcb_ref[0, kb * LC:(kb + 1) * LC, :])), axis=0))
        dirs = (
            (0, qf_ref, vf_ref, ktf_ref, kf, gf, rows_f, cols_f, s_idx <= t_idx, hf_ref, 0, 4, LC - 1),
            (1, qb_ref, vb_ref, ktb_ref, kb, gb, rows_b, cols_b, s_idx >= t_idx, hb_ref, 8, 12, 0),
        )
        for d, q_ref, v_ref, kt_ref, kc, g, rows, cols, mask, h_ref, i_off, f_off, last in dirs:
            rs = slice(kc * LC, (kc + 1) * LC)
            for h in range(M_H):
                hs = slice(h * M_DH, (h + 1) * M_DH)
                q = q_ref[0, rs, hs]
                kt = kt_ref[0, hs, rs]
                vaug = jnp.concatenate([v_ref[0, rs, hs], ones], axis=1)
                i_row = g[i_off + h:i_off + h + 1, :]
                b_row = rows[f_off + h:f_off + h + 1, :]
                b_col = jnp.broadcast_to(cols[:, f_off + h:f_off + h + 1], (LC, M_DH))
                b_last = b_row[:, last:last + 1]
                c0, m0 = state[d][h]

                am = jnp.where(mask, (i_row - b_row) * LOG2E, -jnp.inf)
                g2 = jnp.maximum(m0 * LOG2E, jnp.max(am, axis=-1, keepdims=True))
                g2 = jnp.broadcast_to(g2, (LC, M_DH))
                e = jnp.exp2(am - jnp.concatenate([g2, g2], axis=1))
                s = (_dot(q, kt) * e).astype(BF16)
                qi = (q.astype(F32) * jnp.exp2(m0 * LOG2E - g2)).astype(BF16)
                na = _dot(s, vaug) + _dot(qi, c0.astype(BF16))
                floor = jnp.exp2(-(b_col * LOG2E + g2))
                h_new = (na[:, :M_DH] / jnp.maximum(jnp.abs(na[:, M_DH:]), floor)).astype(h_ref.dtype)
                h_out.append((h_ref, rs, hs, h_new))

                gs = b_last - b_row + i_row
                m_new = jnp.maximum(b_last + m0, jnp.max(gs, axis=-1, keepdims=True))
                kw = (kt.astype(F32) * jnp.exp(gs - m_new)).astype(BF16)
                state[d][h] = (jnp.exp(b_last + m0 - m_new) * c0 + _dot(kw, vaug), m_new)

    for h_ref, rs, hs, h_new in h_out:
        h_ref[0, rs, hs] = h_new
    for d, (c_ref, m_ref) in enumerate(((cf_ref, mf_ref), (cb_ref, mb_ref))):
        for h in range(M_H):
            c_ref[0, h] = state[d][h][0]
            m_ref[0, h] = jnp.broadcast_to(state[d][h][1], (8, 128))


def _tri_constants():
    s = np.arange(LC)[:, None]
    t = np.arange(LC)[None, :]
    u = (s <= t).astype(np.float32)
    lo = (s >= t).astype(np.float32)
    u3 = np.concatenate([u, u, u], axis=0)
    lo3 = np.concatenate([lo, lo, lo], axis=0)
    return (jnp.asarray(u3, BF16), jnp.asarray(lo3, BF16),
            jnp.asarray(u3.T.copy(), BF16), jnp.asarray(lo3.T.copy(), BF16))


def _mlstm(qvo, kt, gt, gc, state_f, state_b):
    b, n, _ = qvo.shape
    sub = min(MLSTM_SUB, n // LC)
    rows = sub * LC
    nc = n // rows
    cf0, mf0 = state_f
    cb0, mb0 = state_b
    fwd = lambda i, c: (i, c, 0)
    bwd = lambda i, c: (i, nc - 1 - c, 0)
    st_c = pl.BlockSpec((1, M_H, M_DH, 2 * M_DH), lambda i, c: (i, 0, 0, 0))
    st_m = pl.BlockSpec((1, M_H, 8, 128), lambda i, c: (i, 0, 0, 0))
    h_sds = jax.ShapeDtypeStruct((b, n, M_W), BF16)
    return pl.pallas_call(
        _mlstm_kernel,
        grid=(b, nc),
        in_specs=[pl.BlockSpec((1, rows, M_W), fwd),
                  pl.BlockSpec((1, rows, M_W), lambda i, c: (i, c, 1)),
                  pl.BlockSpec((1, M_W, rows), lambda i, c: (i, 0, c)),
                  pl.BlockSpec((1, sub, 16, LC), lambda i, c: (i, c, 0, 0)),
                  pl.BlockSpec((1, rows, 128), fwd),
                  pl.BlockSpec((1, rows, M_W), bwd),
                  pl.BlockSpec((1, rows, M_W), lambda i, c: (i, nc - 1 - c, 1)),
                  pl.BlockSpec((1, M_W, rows), lambda i, c: (i, 0, nc - 1 - c)),
                  pl.BlockSpec((1, sub, 16, LC), lambda i, c: (i, nc - 1 - c, 0, 0)),
                  pl.BlockSpec((1, rows, 128), bwd),
                  st_c, st_c, st_m, st_m,
                  _const_spec((3 * LC, LC)), _const_spec((3 * LC, LC)),
                  _const_spec((LC, 3 * LC)), _const_spec((LC, 3 * LC))],
        out_specs=(pl.BlockSpec((1, rows, M_W), fwd), pl.BlockSpec((1, rows, M_W), bwd),
                   st_c, st_c, st_m, st_m),
        out_shape=(h_sds, h_sds,
                   jax.ShapeDtypeStruct(cf0.shape, F32), jax.ShapeDtypeStruct(cb0.shape, F32),
                   jax.ShapeDtypeStruct(mf0.shape, F32), jax.ShapeDtypeStruct(mb0.shape, F32)),
        compiler_params=_params("parallel", "arbitrary"),
        name="mlstm",
    )(qvo, qvo, kt, gt, gc, qvo, qvo, kt, gt, gc, cf0, cb0, mf0, mb0, *_tri_constants())


def _channel_dft_matrix():
    gw = F_W // F_G
    j = np.arange(gw)[:, None]
    l = np.arange(gw)[None, :]
    ang = 2.0 * np.pi * j * l / gw
    eye = np.eye(F_G)
    bc = np.kron(eye, np.cos(ang)) / math.sqrt(gw)
    bs = np.kron(eye, np.sin(ang)) / math.sqrt(gw)
    return jnp.asarray(np.concatenate([bc, -bs], axis=1), BF16)


def _dft_cos_sin(n):
    k = np.arange(n)[:, None]
    m = np.arange(n)[None, :]
    ang = 2.0 * np.pi * ((k * m) % n) / n
    return np.cos(ang) / math.sqrt(n), np.sin(ang) / math.sqrt(n)


def _fourier1_kernel(z_ref, bcs_ref, f1_ref, o_ref, ts_ref):
    n2cnt = o_ref.shape[1] // 2

    def group(g, carry):
        for r in range(DFT_GROUP):
            j = g * DFT_GROUP + r
            off = pl.multiple_of(j * F_W, F_W)
            ab = _dot(z_ref[0, :, pl.ds(off, F_W)].astype(BF16), bcs_ref[...])
            u = jnp.concatenate([ab[:, :F_W], ab[:, F_W:]], axis=0).astype(BF16)
            t = _dot(f1_ref[j], u)
            ts_ref[0, r] = t[:DFT_N1]
            ts_ref[1, r] = t[DFT_N1:]
        for part in range(2):
            row0 = pl.multiple_of(part * n2cnt + g * DFT_GROUP, DFT_GROUP)
            swapped = pltpu.einshape("jkc->kjc", ts_ref[part])
            for k1 in range(DFT_N1):
                o_ref[0, pl.ds(row0, DFT_GROUP), k1 * F_W:(k1 + 1) * F_W] = swapped[k1].astype(BF16)
        return carry

    lax.fori_loop(0, n2cnt // DFT_GROUP, group, 0)


def _fourier2_kernel(t_ref, f2_ref, o_ref, s_ref):
    n2cnt = f2_ref.shape[0]
    width = t_ref.shape[2]
    tw = DFT_GROUP * F_W
    for c in range(width // tw):
        y = _dot(f2_ref[...], t_ref[0, :, c * tw:(c + 1) * tw])
        for kk in range(DFT_GROUP):
            s_ref[kk] = y[:, kk * F_W:(kk + 1) * F_W]
        swapped = pltpu.einshape("kjc->jkc", s_ref[...])
        for k2 in range(n2cnt):
            r0 = k2 * DFT_N1 + c * DFT_GROUP
            o_ref[0, r0:r0 + DFT_GROUP, :] = swapped[k2].astype(o_ref.dtype)


def _fourier_direct_kernel(z_ref, bcs_ref, f_ref, o_ref):
    ab = _dot(z_ref[0], bcs_ref[...])
    u = jnp.concatenate([ab[:, :F_W], ab[:, F_W:]], axis=0).astype(BF16)
    o_ref[0] = _dot(f_ref[...], u).astype(o_ref.dtype)


def _fourier(zf, n):
    b = zf.shape[0]
    bcs = _channel_dft_matrix()
    if n <= DFT_DIRECT_MAX:
        c, s = _dft_cos_sin(n)
        fmat = jnp.asarray(np.concatenate([c, s], axis=1), BF16)
        return pl.pallas_call(
            _fourier_direct_kernel,
            grid=(b,),
            in_specs=[pl.BlockSpec((1, n, F_W), lambda i: (i, 0, 0)),
                      _const_spec((F_W, 2 * F_W)), _const_spec((n, 2 * n))],
            out_specs=pl.BlockSpec((1, n, F_W), lambda i: (i, 0, 0)),
            out_shape=jax.ShapeDtypeStruct((b, n, F_W), BF16),
            compiler_params=_params("parallel"),
            name="fourier_direct",
        )(zf, bcs, fmat)

    n1 = DFT_N1
    n2 = n // n1
    assert n2 % DFT_GROUP == 0
    jj = np.arange(n2)[:, None, None]
    k1 = np.arange(n1)[None, :, None]
    aa = np.arange(n1)[None, None, :]
    ang = 2.0 * np.pi * (((jj * k1) % n) / n + ((k1 * aa) % n1) / n1)
    c1, s1 = np.cos(ang) / math.sqrt(n1), np.sin(ang) / math.sqrt(n1)
    f1 = jnp.asarray(np.concatenate([np.concatenate([c1, s1], axis=2),
                                     np.concatenate([-s1, c1], axis=2)], axis=1), BF16)
    c2, s2 = _dft_cos_sin(n2)
    f2 = jnp.asarray(np.concatenate([c2, s2], axis=1), BF16)
    width = n1 * F_W

    t = pl.pallas_call(
        _fourier1_kernel,
        grid=(b,),
        in_specs=[pl.BlockSpec((1, n1, n2 * F_W), lambda i: (i, 0, 0)),
                  _const_spec((F_W, 2 * F_W)), _const_spec((n2, 2 * n1, 2 * n1))],
        out_specs=pl.BlockSpec((1, 2 * n2, width), lambda i: (i, 0, 0)),
        out_shape=jax.ShapeDtypeStruct((b, 2 * n2, width), BF16),
        scratch_shapes=[pltpu.VMEM((2, DFT_GROUP, n1, F_W), F32)],
        compiler_params=_params("parallel"),
        name="fourier_stage1",
    )(zf, bcs, f1)
    return pl.pallas_call(
        _fourier2_kernel,
        grid=(b,),
        in_specs=[pl.BlockSpec((1, 2 * n2, width), lambda i: (i, 0, 0)),
                  _const_spec((n2, 2 * n2))],
        out_specs=pl.BlockSpec((1, n, F_W), lambda i: (i, 0, 0)),
        out_shape=jax.ShapeDtypeStruct((b, n, F_W), BF16),
        scratch_shapes=[pltpu.VMEM((DFT_GROUP, n2, F_W), F32)],
        compiler_params=_params("parallel"),
        name="fourier_stage2",
    )(t, f2)


def _group_mean(y, gm_ref):
    hi, lo = _split2(y)
    return _dot(hi, gm_ref[...]) + _dot(lo, gm_ref[...])


def _conv_kernel(cc_ref, dw_ref, db_ref, lg_ref, lb_ref, gm_ref, o_ref, a_ref, ph_ref):
    n = cc_ref.shape[1]
    rt = min(CONV_TILE, n)
    span = rt + 2 * CONV_PAD - 8
    a_ref[0:CONV_PAD, :] = jnp.zeros((CONV_PAD, C_W), F32)
    a_ref[n + CONV_PAD:n + 2 * CONV_PAD, :] = jnp.zeros((CONV_PAD, C_W), F32)

    def glu(i, carry):
        r = pl.multiple_of(i * rt, rt)
        blk = cc_ref[0, pl.ds(r, rt), :].astype(F32)
        a_ref[pl.ds(r + CONV_PAD, rt), :] = blk[:, :C_W] * jax.nn.sigmoid(blk[:, C_W:])
        return carry

    lax.fori_loop(0, n // rt, glu, 0)

    def conv(i, carry):
        r = pl.multiple_of(i * rt, rt)
        win = a_ref[pl.ds(r, rt + 2 * CONV_PAD), :]
        for s in range(1, 8):
            ph_ref[s - 1] = win[s:s + span]
        acc = jnp.broadcast_to(db_ref[...], (rt, C_W))
        for j in range(CONV_K):
            off = CONV_PAD - CONV_K // 2 + j
            base = 8 * (off // 8)
            if off % 8 == 0:
                tap = a_ref[pl.ds(r + base, rt), :]
            else:
                tap = ph_ref[off % 8 - 1, base:base + rt, :]
            acc = acc + tap * dw_ref[j:j + 1, :]
        d = acc - _group_mean(acc, gm_ref)
        var = _dot((d * d).astype(BF16), gm_ref[...])
        yn = d * lax.rsqrt(var + EPS) * lg_ref[...] + lb_ref[...]
        o_ref[0, pl.ds(r, rt), :] = _silu(yn).astype(o_ref.dtype)
        return carry

    lax.fori_loop(0, n // rt, conv, 0)


def _conv(cc, dw, db, ln_g, ln_b):
    b, n, _ = cc.shape
    gw = C_W // C_G
    gm = jnp.asarray(np.kron(np.eye(C_G), np.full((gw, gw), 1.0 / gw)), BF16)
    dw32 = jnp.concatenate([dw, jnp.zeros((32 - CONV_K, C_W), F32)], axis=0)
    return pl.pallas_call(
        _conv_kernel,
        grid=(b,),
        in_specs=[pl.BlockSpec((1, n, 2 * C_W), lambda i: (i, 0, 0)),
                  _const_spec((32, C_W)), _const_spec((1, C_W)), _const_spec((1, C_W)),
                  _const_spec((1, C_W)), _const_spec((C_W, C_W))],
        out_specs=pl.BlockSpec((1, n, C_W), lambda i: (i, 0, 0)),
        out_shape=jax.ShapeDtypeStruct((b, n, C_W), BF16),
        scratch_shapes=[pltpu.VMEM((n + 2 * CONV_PAD, C_W), F32),
                        pltpu.VMEM((7, min(CONV_TILE, n) + 2 * CONV_PAD - 8, C_W), F32)],
        compiler_params=_params("parallel"),
        name="conv_module",
    )(cc, dw32, db.reshape(1, C_W), ln_g.reshape(1, C_W), ln_b.reshape(1, C_W), gm)


def _top2_route(logits):
    row = lax.broadcasted_iota(jnp.int32, logits.shape, 0).astype(F32)
    lg = jnp.where(row < N_EXPERTS, logits, -jnp.inf)
    v1 = jnp.max(lg, axis=0, keepdims=True)
    i1 = jnp.min(jnp.where(lg == v1, row, 16.0), axis=0, keepdims=True)
    lg2 = jnp.where(row == i1, -jnp.inf, lg)
    v2 = jnp.max(lg2, axis=0, keepdims=True)
    i2 = jnp.min(jnp.where(lg2 == v2, row, 16.0), axis=0, keepdims=True)
    e2 = jnp.exp(v2 - v1)
    return row, i1, i2, 1.0 / (1.0 + e2), e2 / (1.0 + e2)


def _pack_halves(y):
    w = y.shape[1] // 2
    lo = pltpu.bitcast(y[:, :w].astype(BF16).astype(F32), jnp.uint32)
    hi = pltpu.bitcast(y[:, w:].astype(BF16).astype(F32), jnp.uint32)
    return (hi & jnp.uint32(0xFFFF0000)) | (lo >> 16)


def _unpack_halves(p):
    lo = pltpu.bitcast(p << 16, F32)
    hi = pltpu.bitcast(p & jnp.uint32(0xFFFF0000), F32)
    return jnp.concatenate([lo, hi], axis=1)


def _merge_heads(hf_ref, hb_ref, o_ref, yf_ref, yc_ref, x_ref, mod_ref, hg_ref, g2_ref, wo_ref):
    hsum = hf_ref[0].astype(F32) + hb_ref[0].astype(F32)
    heads = [_rms(hsum[:, h * M_DH:(h + 1) * M_DH]) for h in range(M_H)]
    ym = jax.nn.sigmoid(o_ref[0].astype(F32)) * (jnp.concatenate(heads, axis=1) * hg_ref[...])
    proj = (_dot(ym.astype(BF16), wo_ref[0:M_W, :])
            + _dot(yf_ref[0], wo_ref[M_W:M_W + F_W, :])
            + _dot(yc_ref[0], wo_ref[M_W + F_W:, :]))
    xn = x_ref[0] + mod_ref[0, 2:3, :] * proj
    hx = (_rms(xn) * g2_ref[...]) * (1.0 + mod_ref[0, 4:5, :]) + mod_ref[0, 3:4, :]
    return xn, hx


def _out_ffn_kernel(hf_ref, hb_ref, o_ref, yf_ref, yc_ref, x_ref, mod_ref, hg_ref, g2_ref, wo_ref,
                    wgu_ref, wd_ref, xo_ref):
    xn, hx = _merge_heads(hf_ref, hb_ref, o_ref, yf_ref, yc_ref, x_ref, mod_ref, hg_ref, g2_ref, wo_ref)
    xo_ref[0] = xn + mod_ref[0, 5:6, :] * _swiglu_tile(hx.astype(BF16), wgu_ref, wd_ref)


def _out_kernel(hf_ref, hb_ref, o_ref, yf_ref, yc_ref, x_ref, mod_ref, hg_ref, g2_ref, wo_ref,
                wrt_ref, brt_ref, tri_ref, xo_ref, hx_ref, route_ref, cnt_ref, carry_ref):
    xn, hx = _merge_heads(hf_ref, hb_ref, o_ref, yf_ref, yc_ref, x_ref, mod_ref, hg_ref, g2_ref, wo_ref)
    hh, hl = _split2(hx)
    wh, wl = _split2(wrt_ref[...])
    logits = _dot_nt(wh, hh) + _dot_nt(wl, hh) + _dot_nt(wh, hl) + brt_ref[:, 0:1]
    row, i1, i2, w1, w2 = _top2_route(logits)

    @pl.when((pl.program_id(0) == 0) & (pl.program_id(1) == 0))
    def _():
        carry_ref[...] = jnp.zeros_like(carry_ref)

    hit = jnp.where((row == i1) | (row == i2), 1.0, 0.0)
    before = _dot(hit.astype(BF16), tri_ref[...]) + carry_ref[:, 0:1]
    rank1 = jnp.sum(jnp.where(row == i1, before, 0.0), axis=0, keepdims=True)
    rank2 = jnp.sum(jnp.where(row == i2, before, 0.0), axis=0, keepdims=True)
    total = carry_ref[...] + jnp.sum(hit, axis=1, keepdims=True)
    xo_ref[0] = xn
    hx_ref[0] = _pack_halves(hx)
    zero = jnp.zeros_like(w1)
    route_ref[0] = jnp.concatenate([i1, i2, w1, w2, rank1, rank2, zero, zero], axis=0)
    carry_ref[...] = total
    cnt_ref[...] = total.astype(jnp.int32)


def _merge_specs(hf, hb, qvo, yf, yc, x, modv, head_g, norm2_g, w_out):
    b, n, d = x.shape
    tm = min(ROW_TILE, n)
    row = lambda w: pl.BlockSpec((1, tm, w), lambda i, j: (i, j, 0))
    in_specs = [row(M_W), row(M_W),
                pl.BlockSpec((1, tm, M_W), lambda i, j: (i, j, 2)),
                row(F_W), row(C_W), row(d),
                pl.BlockSpec((1, 8, d), lambda i, j: (i, 0, 0)),
                _const_spec((1, M_W)), _const_spec((1, d)), _const_spec((d, d))]
    args = [hf, hb, qvo, yf, yc, x, modv, head_g.reshape(1, M_W), norm2_g.reshape(1, d), w_out]
    return tm, row, in_specs, args


def _out_ffn(hf, hb, qvo, yf, yc, x, modv, head_g, norm2_g, w_out, wgu, wd):
    b, n, d = x.shape
    tm, row, in_specs, args = _merge_specs(hf, hb, qvo, yf, yc, x, modv, head_g, norm2_g, w_out)
    return pl.pallas_call(
        _out_ffn_kernel,
        grid=(b, n // tm),
        in_specs=in_specs + [_const_spec(wgu.shape), _const_spec(wd.shape)],
        out_specs=row(d),
        out_shape=jax.ShapeDtypeStruct((b, n, d), F32),
        compiler_params=_params("parallel", "arbitrary"),
        name="out_ffn",
    )(*args, wgu, wd)


def _out_route(hf, hb, qvo, yf, yc, x, modv, head_g, norm2_g, w_out, wrt, brt):
    b, n, d = x.shape
    tm, row, in_specs, args = _merge_specs(hf, hb, qvo, yf, yc, x, modv, head_g, norm2_g, w_out)
    tri = jnp.asarray(np.triu(np.ones((tm, tm), np.float32), 1), BF16)
    return pl.pallas_call(
        _out_kernel,
        grid=(b, n // tm),
        in_specs=in_specs + [_const_spec((16, d)), _const_spec((16, 128)), _const_spec((tm, tm))],
        out_specs=(row(d), row(d // 2), pl.BlockSpec((1, 8, tm), lambda i, j: (i, 0, j)),
                   _const_spec((16, 128))),
        out_shape=(jax.ShapeDtypeStruct((b, n, d), F32), jax.ShapeDtypeStruct((b, n, d // 2), jnp.uint32),
                   jax.ShapeDtypeStruct((b, 8, n), F32), jax.ShapeDtypeStruct((16, 128), jnp.int32)),
        scratch_shapes=[pltpu.VMEM((16, 128), F32)],
        compiler_params=_params("arbitrary", "arbitrary"),
        name="out_route",
    )(*args, wrt, brt, tri)


def _swiglu_tile(hx, wgu_ref, wd_ref):
    acc = jnp.zeros((hx.shape[0], D_MODEL), F32)
    for j in range(FFN_HIDDEN // HID_TILE):
        cs = slice(j * HID_TILE, (j + 1) * HID_TILE)
        us = slice(FFN_HIDDEN + j * HID_TILE, FFN_HIDDEN + (j + 1) * HID_TILE)
        a = _silu(_dot(hx, wgu_ref[:, cs])) * _dot(hx, wgu_ref[:, us])
        acc = acc + _dot(a.astype(BF16), wd_ref[cs, :])
    return acc


def _sc_worker_rows(n_rows):
    assert n_rows % (SC_WORKERS * SC_CHUNK) == 0
    return n_rows // SC_WORKERS


def _sc_scatter_rows(x, dest, n_out):
    t, w = x.shape
    per_w = _sc_worker_rows(t)
    mesh = plsc.VectorSubcoreMesh(core_axis_name="c", subcore_axis_name="s")

    @functools.partial(
        pl.kernel, mesh=mesh,
        out_type=jax.ShapeDtypeStruct((n_out, w), x.dtype),
        scratch_types=[pltpu.VMEM((1, SC_CHUNK), jnp.int32),
                       pltpu.VMEM((SC_CHUNK, w), x.dtype),
                       pltpu.SemaphoreType.DMA],
    )
    def scatter(x_hbm, dest_hbm, out_hbm, idx_v, rows_v, sem):
        base = (lax.axis_index("s") * SC_CORES + lax.axis_index("c")) * per_w

        @pl.loop(0, per_w // SC_CHUNK)
        def _(g):
            off = base + g * SC_CHUNK
            pltpu.sync_copy(x_hbm.at[pl.ds(off, SC_CHUNK)], rows_v)
            for k in range(2):
                pltpu.sync_copy(dest_hbm.at[pl.ds(k, 1), pl.ds(off, SC_CHUNK)], idx_v)
                pltpu.async_copy(rows_v, out_hbm.at[idx_v.at[0]], sem).wait()

    return scatter(x, dest)


def _sc_gather_rows(table, idx):
    r = idx.shape[0]
    w = table.shape[1]
    per_w = _sc_worker_rows(r)
    mesh = plsc.VectorSubcoreMesh(core_axis_name="c", subcore_axis_name="s")

    @functools.partial(
        pl.kernel, mesh=mesh,
        out_type=jax.ShapeDtypeStruct((r, w), table.dtype),
        scratch_types=[pltpu.VMEM((1, SC_CHUNK), jnp.int32),
                       pltpu.VMEM((SC_CHUNK, w), table.dtype),
                       pltpu.SemaphoreType.DMA],
    )
    def gather(table_hbm, idx_hbm, out_hbm, idx_v, rows_v, sem):
        base = (lax.axis_index("s") * SC_CORES + lax.axis_index("c")) * per_w

        @pl.loop(0, per_w // SC_CHUNK)
        def _(g):
            off = base + g * SC_CHUNK
            pltpu.sync_copy(idx_hbm.at[pl.ds(0, 1), pl.ds(off, SC_CHUNK)], idx_v)
            pltpu.async_copy(table_hbm.at[idx_v.at[0]], rows_v, sem).wait()
            pltpu.sync_copy(rows_v, out_hbm.at[pl.ds(off, SC_CHUNK)])

    return gather(table, idx.reshape(1, r))


def _group_kernel(te_ref, nv_ref, xs_ref, wgu_ref, wd_ref, ys_ref):
    @pl.when(pl.program_id(0) < nv_ref[0])
    def _():
        hx = _unpack_halves(xs_ref[...]).astype(BF16)
        ys_ref[...] = _pack_halves(_swiglu_tile(hx, wgu_ref.at[0], wd_ref.at[0]))


def _grouped_swiglu(xs, tile_expert, n_valid, wgu, wd):
    r, wp = xs.shape
    grid_spec = pltpu.PrefetchScalarGridSpec(
        num_scalar_prefetch=2,
        grid=(r // GROUP_TILE,),
        in_specs=[pl.BlockSpec((GROUP_TILE, wp), lambda i, te, nv: (i, 0)),
                  pl.BlockSpec((1,) + wgu.shape[1:], lambda i, te, nv: (te[i], 0, 0)),
                  pl.BlockSpec((1,) + wd.shape[1:], lambda i, te, nv: (te[i], 0, 0))],
        out_specs=pl.BlockSpec((GROUP_TILE, wp), lambda i, te, nv: (i, 0)),
    )
    return pl.pallas_call(
        _group_kernel,
        grid_spec=grid_spec,
        out_shape=jax.ShapeDtypeStruct((r, wp), jnp.uint32),
        compiler_params=_params("arbitrary"),
        name="grouped_swiglu",
    )(tile_expert, n_valid, xs, wgu, wd)


def _combine_kernel(x_ref, y_ref, route_ref, mod_ref, fg_ref, *rest):
    o_ref = rest[-1]
    r = route_ref[0]
    moe = r[:, 2:3] * _unpack_halves(y_ref[0]) + r[:, 3:4] * _unpack_halves(y_ref[1])
    xn = x_ref[0] + mod_ref[0, 5:6, :] * moe
    o_ref[0] = _rms(xn) * fg_ref[...]


def _combine_final(x, y2, route, modv, final_g, bi, prev):
    b, n, d = x.shape
    tm = min(2 * ROW_TILE, n)
    row = lambda w: pl.BlockSpec((1, tm, w), lambda j: (bi, j, 0))
    in_specs = [row(d), pl.BlockSpec((2, tm, d // 2), lambda j: (0, j, 0)), row(8),
                pl.BlockSpec((1, 8, d), lambda j: (bi, 0, 0)), _const_spec((1, d))]
    args = [x, y2, route, modv, final_g.reshape(1, d)]
    aliases = {}
    if prev is not None:
        in_specs.append(pl.BlockSpec(memory_space=pl.ANY))
        args.append(prev)
        aliases = {len(args) - 1: 0}
    return pl.pallas_call(
        _combine_kernel,
        grid=(n // tm,),
        in_specs=in_specs,
        out_specs=row(d),
        out_shape=jax.ShapeDtypeStruct((b, n, d), F32),
        input_output_aliases=aliases,
        compiler_params=_params("arbitrary"),
        name="moe_combine",
    )(*args)


def _moe_final(hxp, x, route, counts, modv, final_g, wgu, wd):
    b, n, d = x.shape
    t = b * n
    counts = counts[:N_EXPERTS, 0]
    padded = ((counts + GROUP_TILE - 1) // GROUP_TILE) * GROUP_TILE
    ends = jnp.cumsum(padded)
    starts = ends - padded
    rows = jnp.transpose(route, (1, 0, 2)).reshape(8, t)
    experts = rows[0:2].astype(jnp.int32)
    dest = rows[4:6].astype(jnp.int32)
    for e in range(N_EXPERTS):
        dest = dest + jnp.where(experts == e, starts[e], 0)
    route = jnp.transpose(route, (0, 2, 1))
    n_rows = 2 * t + N_EXPERTS * GROUP_TILE
    tile_start = jnp.arange(n_rows // GROUP_TILE, dtype=jnp.int32) * GROUP_TILE
    tile_expert = jnp.minimum(jnp.sum(ends[None, :] <= tile_start[:, None], axis=1), N_EXPERTS - 1).astype(jnp.int32)
    n_valid = (ends[-1:] // GROUP_TILE).astype(jnp.int32)

    xs = _sc_scatter_rows(hxp.reshape(t, d // 2), dest, n_rows)
    ys = _grouped_swiglu(xs, tile_expert, n_valid, wgu, wd)
    out = None
    for bi in range(b):
        y2 = _sc_gather_rows(ys, dest[:, bi * n:(bi + 1) * n].reshape(2 * n))
        out = _combine_final(x, y2.reshape(2, n, d // 2), route, modv, final_g, bi, out)
    return out


def _in_weights(w):
    wm = jnp.concatenate([w[:, O_Q:O_K], w[:, O_V:O_G], w[:, O_F:P_IN]], axis=1).astype(BF16)
    return wm, w[:, O_K:O_V].T.astype(BF16), w[:, O_G:O_F].T.astype(BF16)


def _zero_state(b):
    return (jnp.zeros((b, M_H, M_DH, 2 * M_DH), F32), jnp.full((b, M_H, 8, 128), -1e30, F32))


def _mod_rows(m):
    r = m.shape[0]
    m = m.reshape(r, 6, D_MODEL)
    return jnp.concatenate([m, jnp.zeros((r, 2, D_MODEL), F32)], axis=1)


def kernel(x, c, ctx, c_ctx, norm1_g, norm2_g, w_mod, b_mod, w_in, b_gate, mlstm_norm_g, conv_dw, conv_db, conv_norm_g, conv_norm_b, w_out, ffn_w_gate_up, ffn_w_down, moe_w_router, moe_b_router, moe_w_gate_up, moe_w_down, final_norm_g):
    b = x.shape[0]
    depth = w_in.shape[0]
    assert depth == 2 and b <= 7
    cond8 = jnp.concatenate([c, c_ctx[None, :], jnp.zeros((7 - b, D_MODEL), F32)], axis=0)
    xc = ctx
    mods_all = _modulation(cond8, w_mod, b_mod)
    for layer in range(depth):
        last = layer == depth - 1
        mods = mods_all[layer]
        mod_lat = _mod_rows(mods[:b])
        mod_ctx = jnp.broadcast_to(_mod_rows(mods[b:b + 1]), (b, 8, D_MODEL))
        wm, wkt, wgt = _in_weights(w_in[layer])
        bg = b_gate[layer].reshape(16, 1)
        w_o = w_out[layer].astype(BF16)

        qvo_c, kt_c, zf_c, cc_c, gt_c, gc_c = _in_proj(xc, mod_ctx, norm1_g[layer], wm, wkt, wgt, bg)
        qvo_l, kt_l, zf_l, cc_l, gt_l, gc_l = _in_proj(x, mod_lat, norm1_g[layer], wm, wkt, wgt, bg)

        hf_c, hb_c, cf, cb, mf, mb = _mlstm(qvo_c, kt_c, gt_c, gc_c, _zero_state(b), _zero_state(b))
        hf_l, hb_l, _, _, _, _ = _mlstm(qvo_l, kt_l, gt_l, gc_l, (cf, mf), (cb, mb))

        conv_args = (conv_dw[layer], conv_db[layer], conv_norm_g[layer], conv_norm_b[layer])
        yf_l = _fourier(zf_l, x.shape[1])
        yc_l = _conv(cc_l, *conv_args)
        if not last:
            wgu = ffn_w_gate_up[layer // 2].astype(BF16)
            wd = ffn_w_down[layer // 2].astype(BF16)
            x = _out_ffn(hf_l, hb_l, qvo_l, yf_l, yc_l, x, mod_lat, mlstm_norm_g[layer],
                         norm2_g[layer], w_o, wgu, wd)
            yf_c = _fourier(zf_c, xc.shape[1])
            yc_c = _conv(cc_c, *conv_args)
            xc = _out_ffn(hf_c, hb_c, qvo_c, yf_c, yc_c, xc, mod_ctx, mlstm_norm_g[layer],
                          norm2_g[layer], w_o, wgu, wd)
        else:
            idx = layer // 2
            wrt = jnp.concatenate([moe_w_router[idx].T, jnp.zeros((16 - N_EXPERTS, D_MODEL), F32)], axis=0)
            brt = jnp.broadcast_to(jnp.concatenate([moe_b_router[idx], jnp.zeros((16 - N_EXPERTS,), F32)])[:, None],
                                   (16, 128))
            x, hxp, route, counts = _out_route(hf_l, hb_l, qvo_l, yf_l, yc_l, x, mod_lat,
                                               mlstm_norm_g[layer], norm2_g[layer], w_o, wrt, brt)
            x = _moe_final(hxp, x, route, counts, mod_lat, final_norm_g,
                           moe_w_gate_up[idx].astype(BF16), moe_w_down[idx].astype(BF16))
    return x
```

```python
import functools
import math

import numpy as np
import jax
import jax.numpy as jnp
from jax import lax
from jax.experimental import pallas as pl
from jax.experimental.pallas import tpu as pltpu
from jax.experimental.pallas import tpu_sc as plsc

F32 = jnp.float32
BF16 = jnp.bfloat16

D_MODEL = 1024
M_W = 512
M_H = 4
M_DH = 128
F_W = 256
F_G = 4
C_W = 256
C_G = 4
CONV_K = 31
FFN_HIDDEN = 2816
N_EXPERTS = 8
EPS = 1e-6
LOG2E = 1.4426950408889634

O_Q = 0
O_K = O_Q + M_W
O_V = O_K + M_W
O_O = O_V + M_W
O_G = O_O + M_W
O_F = O_G + 4 * M_H
O_CU = O_F + F_W
O_CG = O_CU + C_W
P_IN = O_CG + C_W

LC = 256
MLSTM_SUB = 8
DFT_N1 = 128
DFT_DIRECT_MAX = 512
DFT_GROUP = 16
ROW_TILE = 512
IN_TILE = 1024
HID_TILE = 256
CONV_TILE = 256
CONV_PAD = 16
GROUP_TILE = 512
SC_CORES = 2
SC_SUBCORES = 16
SC_WORKERS = SC_CORES * SC_SUBCORES
SC_CHUNK = 128
VMEM_LIMIT = 56 * 1024 * 1024


def _params(*sem):
    return pltpu.CompilerParams(dimension_semantics=sem, vmem_limit_bytes=VMEM_LIMIT)


def _const_spec(shape):
    zeros = (0,) * len(shape)
    return pl.BlockSpec(shape, lambda *_: zeros, pipeline_mode=pl.Buffered(1))


def _dot(a, b):
    return jnp.dot(a, b, preferred_element_type=F32)


def _dot_nt(a, b):
    return lax.dot_general(a, b, (((1,), (1,)), ((), ())), preferred_element_type=F32)


def _split2(a):
    hi = a.astype(BF16)
    lo = (a - hi.astype(F32)).astype(BF16)
    return hi, lo


def _split3(a):
    x1 = a.astype(BF16)
    r1 = a - x1.astype(F32)
    x2 = r1.astype(BF16)
    x3 = (r1 - x2.astype(F32)).astype(BF16)
    return x1, x2, x3


def _dot_precise(a, b):
    ah, al = _split2(a)
    bh, bl = _split2(b)
    return _dot(ah, bh) + _dot(al, bh) + _dot(ah, bl)


def _silu(x):
    return x * jax.nn.sigmoid(x)


def _log_sigmoid(x):
    return -(jnp.maximum(-x, 0.0) + jnp.log1p(jnp.exp(-jnp.abs(x))))


def _rms(x):
    return x * lax.rsqrt(jnp.mean(x * x, axis=-1, keepdims=True) + EPS)


def _mod_kernel(c_ref, w_ref, b_ref, o_ref):
    o_ref[0] = _dot_precise(_silu(c_ref[...]), w_ref[0]) + b_ref[0]


def _modulation(cond8, w_mod, b_mod):
    d = cond8.shape[1]
    depth, _, n_out = w_mod.shape
    tn = 1536
    return pl.pallas_call(
        _mod_kernel,
        grid=(depth, n_out // tn),
        in_specs=[_const_spec((8, d)),
                  pl.BlockSpec((1, d, tn), lambda l, j: (l, 0, j)),
                  pl.BlockSpec((1, 1, tn), lambda l, j: (l, 0, j))],
        out_specs=pl.BlockSpec((1, 8, tn), lambda l, j: (l, 0, j)),
        out_shape=jax.ShapeDtypeStruct((depth, 8, n_out), F32),
        compiler_params=_params("arbitrary", "arbitrary"),
        name="modulation",
    )(cond8, w_mod, b_mod.reshape(depth, 1, n_out))


def _in_kernel(x_ref, mod_ref, g_ref, wm_ref, wkt_ref, wgt_ref, bg_ref,
               qvo_ref, kt_ref, zf_ref, cc_ref, gt_ref, gc_ref, *scratch):
    tm = x_ref.shape[1]
    h = _rms(x_ref[0]) * g_ref[...]
    h = h * (1.0 + mod_ref[0, 1:2, :]) + mod_ref[0, 0:1, :]
    hb = h.astype(BF16)
    p = _dot(hb, wm_ref[...])
    kt = _dot_nt(wkt_ref[...], hb) * (M_DH ** -0.5)
    gt = _dot_nt(wgt_ref[...], hb) + bg_ref[...]
    o_f, o_c, o_g = 3 * M_W, 3 * M_W + F_W, 3 * M_W + F_W + 2 * C_W
    qvo_ref[0] = p[:, :o_f].astype(BF16)
    if scratch:
        zs_ref, = scratch
        rows = zf_ref.shape[1]
        n2cnt = tm // rows
        zs_ref[...] = p[:, o_f:o_c].reshape(rows, n2cnt, F_W)
        swapped = pltpu.einshape("ajc->jac", zs_ref[...])
        for j in range(n2cnt):
            zf_ref[0, :, j * F_W:(j + 1) * F_W] = swapped[j]
    else:
        zf_ref[0] = p[:, o_f:o_c].astype(BF16)
    cc_ref[0] = p[:, o_c:o_g].astype(BF16)
    kt_ref[0] = kt.astype(BF16)
    gc_ref[0] = jnp.concatenate([gt, jnp.zeros((128 - 16, tm), F32)], axis=0).T
    for j in range(tm // LC):
        gt_ref[0, j] = gt[:, j * LC:(j + 1) * LC]


def _in_proj(x, modv, norm_g, wm, wkt, wgt, bg):
    b, n, d = x.shape
    tm = min(IN_TILE, n)
    nm = wm.shape[1]
    if n > DFT_DIRECT_MAX:
        n2cnt = n // DFT_N1
        assert tm % n2cnt == 0 and (tm // n2cnt) % 8 == 0
        zf_sds = jax.ShapeDtypeStruct((b, DFT_N1, n2cnt * F_W), F32)
        zf_spec = pl.BlockSpec((1, tm // n2cnt, n2cnt * F_W), lambda i, j: (i, j, 0))
        scratch = [pltpu.VMEM((tm // n2cnt, n2cnt, F_W), F32)]
    else:
        zf_sds = jax.ShapeDtypeStruct((b, n, F_W), BF16)
        zf_spec = pl.BlockSpec((1, tm, F_W), lambda i, j: (i, j, 0))
        scratch = []
    out_shape = (
        jax.ShapeDtypeStruct((b, n, 3 * M_W), BF16),
        jax.ShapeDtypeStruct((b, M_W, n), BF16),
        zf_sds,
        jax.ShapeDtypeStruct((b, n, 2 * C_W), BF16),
        jax.ShapeDtypeStruct((b, n // LC, 16, LC), F32),
        jax.ShapeDtypeStruct((b, n, 128), F32),
    )
    return pl.pallas_call(
        _in_kernel,
        grid=(b, n // tm),
        in_specs=[pl.BlockSpec((1, tm, d), lambda i, j: (i, j, 0)),
                  pl.BlockSpec((1, 8, d), lambda i, j: (i, 0, 0)),
                  _const_spec((1, d)),
                  _const_spec((d, nm)),
                  _const_spec((M_W, d)),
                  _const_spec((16, d)),
                  _const_spec((16, 1))],
        out_specs=(pl.BlockSpec((1, tm, 3 * M_W), lambda i, j: (i, j, 0)),
                   pl.BlockSpec((1, M_W, tm), lambda i, j: (i, 0, j)),
                   zf_spec,
                   pl.BlockSpec((1, tm, 2 * C_W), lambda i, j: (i, j, 0)),
                   pl.BlockSpec((1, tm // LC, 16, LC), lambda i, j: (i, j, 0, 0)),
                   pl.BlockSpec((1, tm, 128), lambda i, j: (i, j, 0))),
        out_shape=out_shape,
        scratch_shapes=scratch,
        compiler_params=_params("parallel", "arbitrary"),
        name="in_proj",
    )(x, modv, norm_g.reshape(1, d), wm, wkt, wgt, bg)


def _mlstm_kernel(qf_ref, vf_ref, ktf_ref, gtf_ref, gcf_ref, qb_ref, vb_ref, ktb_ref, gtb_ref, gcb_ref,
                  cf0_ref, cb0_ref, mf0_ref, mb0_ref, u3_ref, lo3_ref, ut3_ref, lot3_ref,
                  hf_ref, hb_ref, cf_ref, cb_ref, mf_ref, mb_ref):
    @pl.when(pl.program_id(1) == 0)
    def _():
        cf_ref[...] = cf0_ref[...]
        cb_ref[...] = cb0_ref[...]
        mf_ref[...] = mf0_ref[...]
        mb_ref[...] = mb0_ref[...]

    sub = qf_ref.shape[1] // LC
    t_idx = lax.broadcasted_iota(jnp.int32, (LC, LC), 0)
    s_idx = lax.broadcasted_iota(jnp.int32, (LC, LC), 1)
    ones = jnp.ones((LC, M_DH), BF16)
    state = [[(c_ref[0, h], m_ref[0, h, 0:1, 0:1]) for h in range(M_H)] for c_ref, m_ref in
             ((cf_ref, mf_ref), (cb_ref, mb_ref))]
    h_out = []
    for k in range(sub):
        kf, kb = k, sub - 1 - k
        gf = gtf_ref[0, kf]
        gb = gtb_ref[0, kb]
        rows_f = _dot(jnp.concatenate(_split3(_log_sigmoid(gf)), axis=1), u3_ref[...])
        rows_b = _dot(jnp.concatenate(_split3(_log_sigmoid(gb)), axis=1), lo3_ref[...])
        cols_f = _dot(ut3_ref[...], jnp.concatenate(
            _split3(_log_sigmoid(gcf_ref[0, kf * LC:(kf + 1) * LC, :])), axis=0))
        cols_b = _dot(lot3_ref[...], jnp.concatenate(
            _split3(_log_sigmoid(gcb_ref[0, kb * LC:(kb + 1) * LC, :])), axis=0))
        dirs = (
            (0, qf_ref, vf_ref, ktf_ref, kf, gf, rows_f, cols_f, s_idx <= t_idx, hf_ref, 0, 4, LC - 1),
            (1, qb_ref, vb_ref, ktb_ref, kb, gb, rows_b, cols_b, s_idx >= t_idx, hb_ref, 8, 12, 0),
        )
        for d, q_ref, v_ref, kt_ref, kc, g, rows, cols, mask, h_ref, i_off, f_off, last in dirs:
            rs = slice(kc * LC, (kc + 1) * LC)
            for h in range(M_H):
                hs = slice(h * M_DH, (h + 1) * M_DH)
                q = q_ref[0, rs, hs]
                kt = kt_ref[0, hs, rs]
                vaug = jnp.concatenate([v_ref[0, rs, hs], ones], axis=1)
                i_row = g[i_off + h:i_off + h + 1, :]
                b_row = rows[f_off + h:f_off + h + 1, :]
                b_col = jnp.broadcast_to(cols[:, f_off + h:f_off + h + 1], (LC, M_DH))
                b_last = b_row[:, last:last + 1]
                c0, m0 = state[d][h]

                am = jnp.where(mask, (i_row - b_row) * LOG2E, -jnp.inf)
                g2 = jnp.maximum(m0 * LOG2E, jnp.max(am, axis=-1, keepdims=True))
                g2 = jnp.broadcast_to(g2, (LC, M_DH))
                e = jnp.exp2(am - jnp.concatenate([g2, g2], axis=1))
                s = (_dot(q, kt) * e).astype(BF16)
                qi = (q.astype(F32) * jnp.exp2(m0 * LOG2E - g2)).astype(BF16)
                na = _dot(s, vaug) + _dot(qi, c0.astype(BF16))
                floor = jnp.exp2(-(b_col * LOG2E + g2))
                h_new = (na[:, :M_DH] / jnp.maximum(jnp.abs(na[:, M_DH:]), floor)).astype(h_ref.dtype)
                h_out.append((h_ref, rs, hs, h_new))

                gs = b_last - b_row + i_row
                m_new = jnp.maximum(b_last + m0, jnp.max(gs, axis=-1, keepdims=True))
                kw = (kt.astype(F32) * jnp.exp(gs - m_new)).astype(BF16)
                state[d][h] = (jnp.exp(b_last + m0 - m_new) * c0 + _dot(kw, vaug), m_new)

    for h_ref, rs, hs, h_new in h_out:
        h_ref[0, rs, hs] = h_new
    for d, (c_ref, m_ref) in enumerate(((cf_ref, mf_ref), (cb_ref, mb_ref))):
        for h in range(M_H):
            c_ref[0, h] = state[d][h][0]
            m_ref[0, h] = jnp.broadcast_to(state[d][h][1], (8, 128))


def _tri_constants():
    s = np.arange(LC)[:, None]
    t = np.arange(LC)[None, :]
    u = (s <= t).astype(np.float32)
    lo = (s >= t).astype(np.float32)
    u3 = np.concatenate([u, u, u], axis=0)
    lo3 = np.concatenate([lo, lo, lo], axis=0)
    return (jnp.asarray(u3, BF16), jnp.asarray(lo3, BF16),
            jnp.asarray(u3.T.copy(), BF16), jnp.asarray(lo3.T.copy(), BF16))


def _mlstm(qvo, kt, gt, gc, state_f, state_b):
    b, n, _ = qvo.shape
    sub = min(MLSTM_SUB, n // LC)
    rows = sub * LC
    nc = n // rows
    cf0, mf0 = state_f
    cb0, mb0 = state_b
    fwd = lambda i, c: (i, c, 0)
    bwd = lambda i, c: (i, nc - 1 - c, 0)
    st_c = pl.BlockSpec((1, M_H, M_DH, 2 * M_DH), lambda i, c: (i, 0, 0, 0))
    st_m = pl.BlockSpec((1, M_H, 8, 128), lambda i, c: (i, 0, 0, 0))
    h_sds = jax.ShapeDtypeStruct((b, n, M_W), BF16)
    return pl.pallas_call(
        _mlstm_kernel,
        grid=(b, nc),
        in_specs=[pl.BlockSpec((1, rows, M_W), fwd),
                  pl.BlockSpec((1, rows, M_W), lambda i, c: (i, c, 1)),
                  pl.BlockSpec((1, M_W, rows), lambda i, c: (i, 0, c)),
                  pl.BlockSpec((1, sub, 16, LC), lambda i, c: (i, c, 0, 0)),
                  pl.BlockSpec((1, rows, 128), fwd),
                  pl.BlockSpec((1, rows, M_W), bwd),
                  pl.BlockSpec((1, rows, M_W), lambda i, c: (i, nc - 1 - c, 1)),
                  pl.BlockSpec((1, M_W, rows), lambda i, c: (i, 0, nc - 1 - c)),
                  pl.BlockSpec((1, sub, 16, LC), lambda i, c: (i, nc - 1 - c, 0, 0)),
                  pl.BlockSpec((1, rows, 128), bwd),
                  st_c, st_c, st_m, st_m,
                  _const_spec((3 * LC, LC)), _const_spec((3 * LC, LC)),
                  _const_spec((LC, 3 * LC)), _const_spec((LC, 3 * LC))],
        out_specs=(pl.BlockSpec((1, rows, M_W), fwd), pl.BlockSpec((1, rows, M_W), bwd),
                   st_c, st_c, st_m, st_m),
        out_shape=(h_sds, h_sds,
                   jax.ShapeDtypeStruct(cf0.shape, F32), jax.ShapeDtypeStruct(cb0.shape, F32),
                   jax.ShapeDtypeStruct(mf0.shape, F32), jax.ShapeDtypeStruct(mb0.shape, F32)),
        compiler_params=_params("parallel", "arbitrary"),
        name="mlstm",
    )(qvo, qvo, kt, gt, gc, qvo, qvo, kt, gt, gc, cf0, cb0, mf0, mb0, *_tri_constants())


def _channel_dft_matrix():
    gw = F_W // F_G
    j = np.arange(gw)[:, None]
    l = np.arange(gw)[None, :]
    ang = 2.0 * np.pi * j * l / gw
    eye = np.eye(F_G)
    bc = np.kron(eye, np.cos(ang)) / math.sqrt(gw)
    bs = np.kron(eye, np.sin(ang)) / math.sqrt(gw)
    return jnp.asarray(np.concatenate([bc, -bs], axis=1), BF16)


def _dft_cos_sin(n):
    k = np.arange(n)[:, None]
    m = np.arange(n)[None, :]
    ang = 2.0 * np.pi * ((k * m) % n) / n
    return np.cos(ang) / math.sqrt(n), np.sin(ang) / math.sqrt(n)


def _fourier1_kernel(z_ref, bcs_ref, f1_ref, o_ref, ts_ref):
    n2cnt = o_ref.shape[1] // 2

    def group(g, carry):
        for r in range(DFT_GROUP):
            j = g * DFT_GROUP + r
            off = pl.multiple_of(j * F_W, F_W)
            ab = _dot(z_ref[0, :, pl.ds(off, F_W)].astype(BF16), bcs_ref[...])
            u = jnp.concatenate([ab[:, :F_W], ab[:, F_W:]], axis=0).astype(BF16)
            t = _dot(f1_ref[j], u)
            ts_ref[0, r] = t[:DFT_N1]
            ts_ref[1, r] = t[DFT_N1:]
        for part in range(2):
            row0 = pl.multiple_of(part * n2cnt + g * DFT_GROUP, DFT_GROUP)
            swapped = pltpu.einshape("jkc->kjc", ts_ref[part])
            for k1 in range(DFT_N1):
                o_ref[0, pl.ds(row0, DFT_GROUP), k1 * F_W:(k1 + 1) * F_W] = swapped[k1].astype(BF16)
        return carry

    lax.fori_loop(0, n2cnt // DFT_GROUP, group, 0)


def _fourier2_kernel(t_ref, f2_ref, o_ref, s_ref):
    n2cnt = f2_ref.shape[0]
    width = t_ref.shape[2]
    tw = DFT_GROUP * F_W
    for c in range(width // tw):
        y = _dot(f2_ref[...], t_ref[0, :, c * tw:(c + 1) * tw])
        for kk in range(DFT_GROUP):
            s_ref[kk] = y[:, kk * F_W:(kk + 1) * F_W]
        swapped = pltpu.einshape("kjc->jkc", s_ref[...])
        for k2 in range(n2cnt):
            r0 = k2 * DFT_N1 + c * DFT_GROUP
            o_ref[0, r0:r0 + DFT_GROUP, :] = swapped[k2].astype(o_ref.dtype)


def _fourier_direct_kernel(z_ref, bcs_ref, f_ref, o_ref):
    ab = _dot(z_ref[0], bcs_ref[...])
    u = jnp.concatenate([ab[:, :F_W], ab[:, F_W:]], axis=0).astype(BF16)
    o_ref[0] = _dot(f_ref[...], u).astype(o_ref.dtype)


def _fourier(zf, n):
    b = zf.shape[0]
    bcs = _channel_dft_matrix()
    if n <= DFT_DIRECT_MAX:
        c, s = _dft_cos_sin(n)
        fmat = jnp.asarray(np.concatenate([c, s], axis=1), BF16)
        return pl.pallas_call(
            _fourier_direct_kernel,
            grid=(b,),
            in_specs=[pl.BlockSpec((1, n, F_W), lambda i: (i, 0, 0)),
                      _const_spec((F_W, 2 * F_W)), _const_spec((n, 2 * n))],
            out_specs=pl.BlockSpec((1, n, F_W), lambda i: (i, 0, 0)),
            out_shape=jax.ShapeDtypeStruct((b, n, F_W), BF16),
            compiler_params=_params("parallel"),
            name="fourier_direct",
        )(zf, bcs, fmat)

    n1 = DFT_N1
    n2 = n // n1
    assert n2 % DFT_GROUP == 0
    jj = np.arange(n2)[:, None, None]
    k1 = np.arange(n1)[None, :, None]
    aa = np.arange(n1)[None, None, :]
    ang = 2.0 * np.pi * (((jj * k1) % n) / n + ((k1 * aa) % n1) / n1)
    c1, s1 = np.cos(ang) / math.sqrt(n1), np.sin(ang) / math.sqrt(n1)
    f1 = jnp.asarray(np.concatenate([np.concatenate([c1, s1], axis=2),
                                     np.concatenate([-s1, c1], axis=2)], axis=1), BF16)
    c2, s2 = _dft_cos_sin(n2)
    f2 = jnp.asarray(np.concatenate([c2, s2], axis=1), BF16)
    width = n1 * F_W

    t = pl.pallas_call(
        _fourier1_kernel,
        grid=(b,),
        in_specs=[pl.BlockSpec((1, n1, n2 * F_W), lambda i: (i, 0, 0)),
                  _const_spec((F_W, 2 * F_W)), _const_spec((n2, 2 * n1, 2 * n1))],
        out_specs=pl.BlockSpec((1, 2 * n2, width), lambda i: (i, 0, 0)),
        out_shape=jax.ShapeDtypeStruct((b, 2 * n2, width), BF16),
        scratch_shapes=[pltpu.VMEM((2, DFT_GROUP, n1, F_W), F32)],
        compiler_params=_params("parallel"),
        name="fourier_stage1",
    )(zf, bcs, f1)
    return pl.pallas_call(
        _fourier2_kernel,
        grid=(b,),
        in_specs=[pl.BlockSpec((1, 2 * n2, width), lambda i: (i, 0, 0)),
                  _const_spec((n2, 2 * n2))],
        out_specs=pl.BlockSpec((1, n, F_W), lambda i: (i, 0, 0)),
        out_shape=jax.ShapeDtypeStruct((b, n, F_W), BF16),
        scratch_shapes=[pltpu.VMEM((DFT_GROUP, n2, F_W), F32)],
        compiler_params=_params("parallel"),
        name="fourier_stage2",
    )(t, f2)


def _group_mean(y, gm_ref):
    hi, lo = _split2(y)
    return _dot(hi, gm_ref[...]) + _dot(lo, gm_ref[...])


def _conv_kernel(cc_ref, dw_ref, db_ref, lg_ref, lb_ref, gm_ref, o_ref, a_ref, ph_ref):
    n = cc_ref.shape[1]
    rt = min(CONV_TILE, n)
    span = rt + 2 * CONV_PAD - 8
    a_ref[0:CONV_PAD, :] = jnp.zeros((CONV_PAD, C_W), F32)
    a_ref[n + CONV_PAD:n + 2 * CONV_PAD, :] = jnp.zeros((CONV_PAD, C_W), F32)

    def glu(i, carry):
        r = pl.multiple_of(i * rt, rt)
        blk = cc_ref[0, pl.ds(r, rt), :].astype(F32)
        a_ref[pl.ds(r + CONV_PAD, rt), :] = blk[:, :C_W] * jax.nn.sigmoid(blk[:, C_W:])
        return carry

    lax.fori_loop(0, n // rt, glu, 0)

    def conv(i, carry):
        r = pl.multiple_of(i * rt, rt)
        win = a_ref[pl.ds(r, rt + 2 * CONV_PAD), :]
        for s in range(1, 8):
            ph_ref[s - 1] = win[s:s + span]
        acc = jnp.broadcast_to(db_ref[...], (rt, C_W))
        for j in range(CONV_K):
            off = CONV_PAD - CONV_K // 2 + j
            base = 8 * (off // 8)
            if off % 8 == 0:
                tap = a_ref[pl.ds(r + base, rt), :]
            else:
                tap = ph_ref[off % 8 - 1, base:base + rt, :]
            acc = acc + tap * dw_ref[j:j + 1, :]
        d = acc - _group_mean(acc, gm_ref)
        var = _dot((d * d).astype(BF16), gm_ref[...])
        yn = d * lax.rsqrt(var + EPS) * lg_ref[...] + lb_ref[...]
        o_ref[0, pl.ds(r, rt), :] = _silu(yn).astype(o_ref.dtype)
        return carry

    lax.fori_loop(0, n // rt, conv, 0)


def _conv(cc, dw, db, ln_g, ln_b):
    b, n, _ = cc.shape
    gw = C_W // C_G
    gm = jnp.asarray(np.kron(np.eye(C_G), np.full((gw, gw), 1.0 / gw)), BF16)
    dw32 = jnp.concatenate([dw, jnp.zeros((32 - CONV_K, C_W), F32)], axis=0)
    return pl.pallas_call(
        _conv_kernel,
        grid=(b,),
        in_specs=[pl.BlockSpec((1, n, 2 * C_W), lambda i: (i, 0, 0)),
                  _const_spec((32, C_W)), _const_spec((1, C_W)), _const_spec((1, C_W)),
                  _const_spec((1, C_W)), _const_spec((C_W, C_W))],
        out_specs=pl.BlockSpec((1, n, C_W), lambda i: (i, 0, 0)),
        out_shape=jax.ShapeDtypeStruct((b, n, C_W), BF16),
        scratch_shapes=[pltpu.VMEM((n + 2 * CONV_PAD, C_W), F32),
                        pltpu.VMEM((7, min(CONV_TILE, n) + 2 * CONV_PAD - 8, C_W), F32)],
        compiler_params=_params("parallel"),
        name="conv_module",
    )(cc, dw32, db.reshape(1, C_W), ln_g.reshape(1, C_W), ln_b.reshape(1, C_W), gm)


def _top2_route(logits):
    row = lax.broadcasted_iota(jnp.int32, logits.shape, 0).astype(F32)
    lg = jnp.where(row < N_EXPERTS, logits, -jnp.inf)
    v1 = jnp.max(lg, axis=0, keepdims=True)
    i1 = jnp.min(jnp.where(lg == v1, row, 16.0), axis=0, keepdims=True)
    lg2 = jnp.where(row == i1, -jnp.inf, lg)
    v2 = jnp.max(lg2, axis=0, keepdims=True)
    i2 = jnp.min(jnp.where(lg2 == v2, row, 16.0), axis=0, keepdims=True)
    e2 = jnp.exp(v2 - v1)
    return row, i1, i2, 1.0 / (1.0 + e2), e2 / (1.0 + e2)


def _pack_halves(y):
    w = y.shape[1] // 2
    lo = pltpu.bitcast(y[:, :w].astype(BF16).astype(F32), jnp.uint32)
    hi = pltpu.bitcast(y[:, w:].astype(BF16).astype(F32), jnp.uint32)
    return (hi & jnp.uint32(0xFFFF0000)) | (lo >> 16)


def _unpack_halves(p):
    lo = pltpu.bitcast(p << 16, F32)
    hi = pltpu.bitcast(p & jnp.uint32(0xFFFF0000), F32)
    return jnp.concatenate([lo, hi], axis=1)


def _merge_heads(hf_ref, hb_ref, o_ref, yf_ref, yc_ref, x_ref, mod_ref, hg_ref, g2_ref, wo_ref):
    hsum = hf_ref[0].astype(F32) + hb_ref[0].astype(F32)
    heads = [_rms(hsum[:, h * M_DH:(h + 1) * M_DH]) for h in range(M_H)]
    ym = jax.nn.sigmoid(o_ref[0].astype(F32)) * (jnp.concatenate(heads, axis=1) * hg_ref[...])
    proj = (_dot(ym.astype(BF16), wo_ref[0:M_W, :])
            + _dot(yf_ref[0], wo_ref[M_W:M_W + F_W, :])
            + _dot(yc_ref[0], wo_ref[M_W + F_W:, :]))
    xn = x_ref[0] + mod_ref[0, 2:3, :] * proj
    hx = (_rms(xn) * g2_ref[...]) * (1.0 + mod_ref[0, 4:5, :]) + mod_ref[0, 3:4, :]
    return xn, hx


def _out_ffn_kernel(hf_ref, hb_ref, o_ref, yf_ref, yc_ref, x_ref, mod_ref, hg_ref, g2_ref, wo_ref,
                    wgu_ref, wd_ref, xo_ref):
    xn, hx = _merge_heads(hf_ref, hb_ref, o_ref, yf_ref, yc_ref, x_ref, mod_ref, hg_ref, g2_ref, wo_ref)
    xo_ref[0] = xn + mod_ref[0, 5:6, :] * _swiglu_tile(hx.astype(BF16), wgu_ref, wd_ref)


def _out_kernel(hf_ref, hb_ref, o_ref, yf_ref, yc_ref, x_ref, mod_ref, hg_ref, g2_ref, wo_ref,
                wrt_ref, brt_ref, tri_ref, xo_ref, hx_ref, route_ref, cnt_ref, carry_ref):
    xn, hx = _merge_heads(hf_ref, hb_ref, o_ref, yf_ref, yc_ref, x_ref, mod_ref, hg_ref, g2_ref, wo_ref)
    hh, hl = _split2(hx)
    wh, wl = _split2(wrt_ref[...])
    logits = _dot_nt(wh, hh) + _dot_nt(wl, hh) + _dot_nt(wh, hl) + brt_ref[:, 0:1]
    row, i1, i2, w1, w2 = _top2_route(logits)

    @pl.when((pl.program_id(0) == 0) & (pl.program_id(1) == 0))
    def _():
        carry_ref[...] = jnp.zeros_like(carry_ref)

    hit = jnp.where((row == i1) | (row == i2), 1.0, 0.0)
    before = _dot(hit.astype(BF16), tri_ref[...]) + carry_ref[:, 0:1]
    rank1 = jnp.sum(jnp.where(row == i1, before, 0.0), axis=0, keepdims=True)
    rank2 = jnp.sum(jnp.where(row == i2, before, 0.0), axis=0, keepdims=True)
    total = carry_ref[...] + jnp.sum(hit, axis=1, keepdims=True)
    xo_ref[0] = xn
    hx_ref[0] = _pack_halves(hx)
    zero = jnp.zeros_like(w1)
    route_ref[0] = jnp.concatenate([i1, i2, w1, w2, rank1, rank2, zero, zero], axis=0)
    carry_ref[...] = total
    cnt_ref[...] = total.astype(jnp.int32)


def _merge_specs(hf, hb, qvo, yf, yc, x, modv, head_g, norm2_g, w_out, tile=ROW_TILE):
    b, n, d = x.shape
    tm = min(tile, n)
    row = lambda w: pl.BlockSpec((1, tm, w), lambda i, j: (i, j, 0))
    in_specs = [row(M_W), row(M_W),
                pl.BlockSpec((1, tm, M_W), lambda i, j: (i, j, 2)),
                row(F_W), row(C_W), row(d),
                pl.BlockSpec((1, 8, d), lambda i, j: (i, 0, 0)),
                _const_spec((1, M_W)), _const_spec((1, d)), _const_spec((d, d))]
    args = [hf, hb, qvo, yf, yc, x, modv, head_g.reshape(1, M_W), norm2_g.reshape(1, d), w_out]
    return tm, row, in_specs, args


def _out_ffn(hf, hb, qvo, yf, yc, x, modv, head_g, norm2_g, w_out, wgu, wd):
    b, n, d = x.shape
    tm, row, in_specs, args = _merge_specs(hf, hb, qvo, yf, yc, x, modv, head_g, norm2_g, w_out)
    return pl.pallas_call(
        _out_ffn_kernel,
        grid=(b, n // tm),
        in_specs=in_specs + [_const_spec(wgu.shape), _const_spec(wd.shape)],
        out_specs=row(d),
        out_shape=jax.ShapeDtypeStruct((b, n, d), F32),
        compiler_params=_params("parallel", "arbitrary"),
        name="out_ffn",
    )(*args, wgu, wd)


def _out_route(hf, hb, qvo, yf, yc, x, modv, head_g, norm2_g, w_out, wrt, brt):
    b, n, d = x.shape
    tm, row, in_specs, args = _merge_specs(hf, hb, qvo, yf, yc, x, modv, head_g, norm2_g, w_out,
                                           tile=2 * ROW_TILE)
    tri = jnp.asarray(np.triu(np.ones((tm, tm), np.float32), 1), BF16)
    return pl.pallas_call(
        _out_kernel,
        grid=(b, n // tm),
        in_specs=in_specs + [_const_spec((16, d)), _const_spec((16, 128)), _const_spec((tm, tm))],
        out_specs=(row(d), row(d // 2), pl.BlockSpec((1, 8, tm), lambda i, j: (i, 0, j)),
                   _const_spec((16, 128))),
        out_shape=(jax.ShapeDtypeStruct((b, n, d), F32), jax.ShapeDtypeStruct((b, n, d // 2), jnp.uint32),
                   jax.ShapeDtypeStruct((b, 8, n), F32), jax.ShapeDtypeStruct((16, 128), jnp.int32)),
        scratch_shapes=[pltpu.VMEM((16, 128), F32)],
        compiler_params=_params("arbitrary", "arbitrary"),
        name="out_route",
    )(*args, wrt, brt, tri)


def _swiglu_tile(hx, wgu_ref, wd_ref):
    acc = jnp.zeros((hx.shape[0], D_MODEL), F32)
    for j in range(FFN_HIDDEN // HID_TILE):
        cs = slice(j * HID_TILE, (j + 1) * HID_TILE)
        us = slice(FFN_HIDDEN + j * HID_TILE, FFN_HIDDEN + (j + 1) * HID_TILE)
        a = _silu(_dot(hx, wgu_ref[:, cs])) * _dot(hx, wgu_ref[:, us])
        acc = acc + _dot(a.astype(BF16), wd_ref[cs, :])
    return acc


def _sc_worker_rows(n_rows):
    assert n_rows % (SC_WORKERS * SC_CHUNK) == 0
    return n_rows // SC_WORKERS


def _sc_scatter_rows(x, dest, n_out):
    t, w = x.shape
    per_w = _sc_worker_rows(t)
    mesh = plsc.VectorSubcoreMesh(core_axis_name="c", subcore_axis_name="s")

    @functools.partial(
        pl.kernel, mesh=mesh,
        out_type=jax.ShapeDtypeStruct((n_out, w), x.dtype),
        scratch_types=[pltpu.VMEM((1, SC_CHUNK), jnp.int32),
                       pltpu.VMEM((SC_CHUNK, w), x.dtype),
                       pltpu.SemaphoreType.DMA],
    )
    def scatter(x_hbm, dest_hbm, out_hbm, idx_v, rows_v, sem):
        base = (lax.axis_index("s") * SC_CORES + lax.axis_index("c")) * per_w

        @pl.loop(0, per_w // SC_CHUNK)
        def _(g):
            off = base + g * SC_CHUNK
            pltpu.sync_copy(x_hbm.at[pl.ds(off, SC_CHUNK)], rows_v)
            for k in range(2):
                pltpu.sync_copy(dest_hbm.at[pl.ds(k, 1), pl.ds(off, SC_CHUNK)], idx_v)
                pltpu.async_copy(rows_v, out_hbm.at[idx_v.at[0]], sem).wait()

    return scatter(x, dest)


def _sc_gather_rows(table, idx):
    r = idx.shape[0]
    w = table.shape[1]
    per_w = _sc_worker_rows(r)
    mesh = plsc.VectorSubcoreMesh(core_axis_name="c", subcore_axis_name="s")

    @functools.partial(
        pl.kernel, mesh=mesh,
        out_type=jax.ShapeDtypeStruct((r, w), table.dtype),
        scratch_types=[pltpu.VMEM((1, SC_CHUNK), jnp.int32),
                       pltpu.VMEM((SC_CHUNK, w), table.dtype),
                       pltpu.SemaphoreType.DMA],
    )
    def gather(table_hbm, idx_hbm, out_hbm, idx_v, rows_v, sem):
        base = (lax.axis_index("s") * SC_CORES + lax.axis_index("c")) * per_w

        @pl.loop(0, per_w // SC_CHUNK)
        def _(g):
            off = base + g * SC_CHUNK
            pltpu.sync_copy(idx_hbm.at[pl.ds(0, 1), pl.ds(off, SC_CHUNK)], idx_v)
            pltpu.async_copy(table_hbm.at[idx_v.at[0]], rows_v, sem).wait()
            pltpu.sync_copy(rows_v, out_hbm.at[pl.ds(off, SC_CHUNK)])

    return gather(table, idx.reshape(1, r))


def _group_kernel(te_ref, nv_ref, xs_ref, wgu_ref, wd_ref, ys_ref):
    @pl.when(pl.program_id(0) < nv_ref[0])
    def _():
        hx = _unpack_halves(xs_ref[...]).astype(BF16)
        ys_ref[...] = _pack_halves(_swiglu_tile(hx, wgu_ref.at[0], wd_ref.at[0]))


def _grouped_swiglu(xs, tile_expert, n_valid, wgu, wd):
    r, wp = xs.shape
    grid_spec = pltpu.PrefetchScalarGridSpec(
        num_scalar_prefetch=2,
        grid=(r // GROUP_TILE,),
        in_specs=[pl.BlockSpec((GROUP_TILE, wp), lambda i, te, nv: (i, 0)),
                  pl.BlockSpec((1,) + wgu.shape[1:], lambda i, te, nv: (te[i], 0, 0)),
                  pl.BlockSpec((1,) + wd.shape[1:], lambda i, te, nv: (te[i], 0, 0))],
        out_specs=pl.BlockSpec((GROUP_TILE, wp), lambda i, te, nv: (i, 0)),
    )
    return pl.pallas_call(
        _group_kernel,
        grid_spec=grid_spec,
        out_shape=jax.ShapeDtypeStruct((r, wp), jnp.uint32),
        compiler_params=_params("arbitrary"),
        name="grouped_swiglu",
    )(tile_expert, n_valid, xs, wgu, wd)


def _combine_kernel(x_ref, y_ref, route_ref, mod_ref, fg_ref, *rest):
    o_ref = rest[-1]
    r = route_ref[0]
    moe = r[:, 2:3] * _unpack_halves(y_ref[0]) + r[:, 3:4] * _unpack_halves(y_ref[1])
    xn = x_ref[0] + mod_ref[0, 5:6, :] * moe
    o_ref[0] = _rms(xn) * fg_ref[...]


def _combine_final(x, y2, route, modv, final_g, bi, prev):
    b, n, d = x.shape
    tm = min(2 * ROW_TILE, n)
    row = lambda w: pl.BlockSpec((1, tm, w), lambda j: (bi, j, 0))
    in_specs = [row(d), pl.BlockSpec((2, tm, d // 2), lambda j: (0, j, 0)), row(8),
                pl.BlockSpec((1, 8, d), lambda j: (bi, 0, 0)), _const_spec((1, d))]
    args = [x, y2, route, modv, final_g.reshape(1, d)]
    aliases = {}
    if prev is not None:
        in_specs.append(pl.BlockSpec(memory_space=pl.ANY))
        args.append(prev)
        aliases = {len(args) - 1: 0}
    return pl.pallas_call(
        _combine_kernel,
        grid=(n // tm,),
        in_specs=in_specs,
        out_specs=row(d),
        out_shape=jax.ShapeDtypeStruct((b, n, d), F32),
        input_output_aliases=aliases,
        compiler_params=_params("arbitrary"),
        name="moe_combine",
    )(*args)


def _moe_final(hxp, x, route, counts, modv, final_g, wgu, wd):
    b, n, d = x.shape
    t = b * n
    counts = counts[:N_EXPERTS, 0]
    padded = ((counts + GROUP_TILE - 1) // GROUP_TILE) * GROUP_TILE
    ends = jnp.cumsum(padded)
    starts = ends - padded
    rows = jnp.transpose(route, (1, 0, 2)).reshape(8, t)
    experts = rows[0:2].astype(jnp.int32)
    dest = rows[4:6].astype(jnp.int32)
    for e in range(N_EXPERTS):
        dest = dest + jnp.where(experts == e, starts[e], 0)
    route = jnp.transpose(route, (0, 2, 1))
    n_rows = 2 * t + N_EXPERTS * GROUP_TILE
    tile_start = jnp.arange(n_rows // GROUP_TILE, dtype=jnp.int32) * GROUP_TILE
    tile_expert = jnp.minimum(jnp.sum(ends[None, :] <= tile_start[:, None], axis=1), N_EXPERTS - 1).astype(jnp.int32)
    n_valid = (ends[-1:] // GROUP_TILE).astype(jnp.int32)

    xs = _sc_scatter_rows(hxp.reshape(t, d // 2), dest, n_rows)
    ys = _grouped_swiglu(xs, tile_expert, n_valid, wgu, wd)
    out = None
    for bi in range(b):
        y2 = _sc_gather_rows(ys, dest[:, bi * n:(bi + 1) * n].reshape(2 * n))
        out = _combine_final(x, y2.reshape(2, n, d // 2), route, modv, final_g, bi, out)
    return out


def _in_weights(w):
    wm = jnp.concatenate([w[:, O_Q:O_K], w[:, O_V:O_G], w[:, O_F:P_IN]], axis=1).astype(BF16)
    return wm, w[:, O_K:O_V].T.astype(BF16), w[:, O_G:O_F].T.astype(BF16)


def _zero_state(b):
    return (jnp.zeros((b, M_H, M_DH, 2 * M_DH), F32), jnp.full((b, M_H, 8, 128), -1e30, F32))


def _mod_rows(m):
    r = m.shape[0]
    m = m.reshape(r, 6, D_MODEL)
    return jnp.concatenate([m, jnp.zeros((r, 2, D_MODEL), F32)], axis=1)


def kernel(x, c, ctx, c_ctx, norm1_g, norm2_g, w_mod, b_mod, w_in, b_gate, mlstm_norm_g, conv_dw, conv_db, conv_norm_g, conv_norm_b, w_out, ffn_w_gate_up, ffn_w_down, moe_w_router, moe_b_router, moe_w_gate_up, moe_w_down, final_norm_g):
    b = x.shape[0]
    depth = w_in.shape[0]
    assert depth == 2 and b <= 7
    cond8 = jnp.concatenate([c, c_ctx[None, :], jnp.zeros((7 - b, D_MODEL), F32)], axis=0)
    xc = ctx
    mods_all = _modulation(cond8, w_mod, b_mod)
    for layer in range(depth):
        last = layer == depth - 1
        mods = mods_all[layer]
        mod_lat = _mod_rows(mods[:b])
        mod_ctx = jnp.broadcast_to(_mod_rows(mods[b:b + 1]), (b, 8, D_MODEL))
        wm, wkt, wgt = _in_weights(w_in[layer])
        bg = b_gate[layer].reshape(16, 1)
        w_o = w_out[layer].astype(BF16)

        qvo_c, kt_c, zf_c, cc_c, gt_c, gc_c = _in_proj(xc, mod_ctx, norm1_g[layer], wm, wkt, wgt, bg)
        qvo_l, kt_l, zf_l, cc_l, gt_l, gc_l = _in_proj(x, mod_lat, norm1_g[layer], wm, wkt, wgt, bg)

        hf_c, hb_c, cf, cb, mf, mb = _mlstm(qvo_c, kt_c, gt_c, gc_c, _zero_state(b), _zero_state(b))
        hf_l, hb_l, _, _, _, _ = _mlstm(qvo_l, kt_l, gt_l, gc_l, (cf, mf), (cb, mb))

        conv_args = (conv_dw[layer], conv_db[layer], conv_norm_g[layer], conv_norm_b[layer])
        yf_l = _fourier(zf_l, x.shape[1])
        yc_l = _conv(cc_l, *conv_args)
        if not last:
            wgu = ffn_w_gate_up[layer // 2].astype(BF16)
            wd = ffn_w_down[layer // 2].astype(BF16)
            x = _out_ffn(hf_l, hb_l, qvo_l, yf_l, yc_l, x, mod_lat, mlstm_norm_g[layer],
                         norm2_g[layer], w_o, wgu, wd)
            yf_c = _fourier(zf_c, xc.shape[1])
            yc_c = _conv(cc_c, *conv_args)
            xc = _out_ffn(hf_c, hb_c, qvo_c, yf_c, yc_c, xc, mod_ctx, mlstm_norm_g[layer],
                          norm2_g[layer], w_o, wgu, wd)
        else:
            idx = layer // 2
            wrt = jnp.concatenate([moe_w_router[idx].T, jnp.zeros((16 - N_EXPERTS, D_MODEL), F32)], axis=0)
            brt = jnp.broadcast_to(jnp.concatenate([moe_b_router[idx], jnp.zeros((16 - N_EXPERTS,), F32)])[:, None],
                                   (16, 128))
            x, hxp, route, counts = _out_route(hf_l, hb_l, qvo_l, yf_l, yc_l, x, mod_lat,
                                               mlstm_norm_g[layer], norm2_g[layer], w_o, wrt, brt)
            x = _moe_final(hxp, x, route, counts, mod_lat, final_norm_g,
                           moe_w_gate_up[idx].astype(BF16), moe_w_down[idx].astype(BF16))
    return x
```

```python
import functools
import math

import numpy as np
import jax
import jax.numpy as jnp
from jax import lax
from jax.experimental import pallas as pl
from jax.experimental.pallas import tpu as pltpu
from jax.experimental.pallas import tpu_sc as plsc

F32 = jnp.float32
BF16 = jnp.bfloat16

D_MODEL = 1024
M_W = 512
M_H = 4
M_DH = 128
F_W = 256
F_G = 4
C_W = 256
C_G = 4
CONV_K = 31
FFN_HIDDEN = 2816
N_EXPERTS = 8
EPS = 1e-6
LOG2E = 1.4426950408889634

O_Q = 0
O_K = O_Q + M_W
O_V = O_K + M_W
O_O = O_V + M_W
O_G = O_O + M_W
O_F = O_G + 4 * M_H
O_CU = O_F + F_W
O_CG = O_CU + C_W
P_IN = O_CG + C_W

LC = 256
MLSTM_SUB = 8
DFT_N1 = 128
DFT_DIRECT_MAX = 512
DFT_GROUP = 16
ROW_TILE = 512
IN_TILE = 1024
HID_TILE = 256
CONV_TILE = 256
CONV_PAD = 16
GROUP_TILE = 512
SC_CORES = 2
SC_SUBCORES = 16
SC_WORKERS = SC_CORES * SC_SUBCORES
SC_CHUNK = 128
LANES = 128
SUBLANES = 8
N_GATES = 4 * M_H
VMEM_LIMIT = 56 * 1024 * 1024


def _params(*sem):
    return pltpu.CompilerParams(dimension_semantics=sem, vmem_limit_bytes=VMEM_LIMIT)


def _const_spec(shape):
    zeros = (0,) * len(shape)
    return pl.BlockSpec(shape, lambda *_: zeros, pipeline_mode=pl.Buffered(1))


def _dot(a, b):
    return jnp.dot(a, b, preferred_element_type=F32)


def _dot_nt(a, b):
    return lax.dot_general(a, b, (((1,), (1,)), ((), ())), preferred_element_type=F32)


def _split2(a):
    hi = a.astype(BF16)
    lo = (a - hi.astype(F32)).astype(BF16)
    return hi, lo


def _split3(a):
    x1 = a.astype(BF16)
    r1 = a - x1.astype(F32)
    x2 = r1.astype(BF16)
    x3 = (r1 - x2.astype(F32)).astype(BF16)
    return x1, x2, x3


def _dot_precise(a, b):
    ah, al = _split2(a)
    bh, bl = _split2(b)
    return _dot(ah, bh) + _dot(al, bh) + _dot(ah, bl)


def _silu(x):
    return x * jax.nn.sigmoid(x)


def _log_sigmoid(x):
    return -(jnp.maximum(-x, 0.0) + jnp.log1p(jnp.exp(-jnp.abs(x))))


def _rms(x):
    return x * lax.rsqrt(jnp.mean(x * x, axis=-1, keepdims=True) + EPS)


def _mod_kernel(c_ref, w_ref, b_ref, o_ref):
    o_ref[0] = _dot_precise(_silu(c_ref[...]), w_ref[0]) + b_ref[0]


def _modulation(cond8, w_mod, b_mod):
    d = cond8.shape[1]
    depth, _, n_out = w_mod.shape
    tn = 1536
    return pl.pallas_call(
        _mod_kernel,
        grid=(depth, n_out // tn),
        in_specs=[_const_spec((8, d)),
                  pl.BlockSpec((1, d, tn), lambda l, j: (l, 0, j)),
                  pl.BlockSpec((1, 1, tn), lambda l, j: (l, 0, j))],
        out_specs=pl.BlockSpec((1, 8, tn), lambda l, j: (l, 0, j)),
        out_shape=jax.ShapeDtypeStruct((depth, 8, n_out), F32),
        compiler_params=_params("arbitrary", "arbitrary"),
        name="modulation",
    )(cond8, w_mod, b_mod.reshape(depth, 1, n_out))


def _in_kernel(x_ref, mod_ref, g_ref, wm_ref, wkt_ref, wgt_ref, bg_ref,
               qvo_ref, kt_ref, zf_ref, cc_ref, gt_ref, gc_ref, *scratch):
    tm = x_ref.shape[1]
    h = _rms(x_ref[0]) * g_ref[...]
    h = h * (1.0 + mod_ref[0, 1:2, :]) + mod_ref[0, 0:1, :]
    hb = h.astype(BF16)
    p = _dot(hb, wm_ref[...])
    kt = _dot_nt(wkt_ref[...], hb) * (M_DH ** -0.5)
    gt = _dot_nt(wgt_ref[...], hb) + bg_ref[...]
    o_f, o_c, o_g = 3 * M_W, 3 * M_W + F_W, 3 * M_W + F_W + 2 * C_W
    qvo_ref[0] = p[:, :o_f].astype(BF16)
    if scratch:
        zs_ref, = scratch
        rows = zf_ref.shape[1]
        n2cnt = tm // rows
        zs_ref[...] = p[:, o_f:o_c].reshape(rows, n2cnt, F_W)
        swapped = pltpu.einshape("ajc->jac", zs_ref[...])
        for j in range(n2cnt):
            zf_ref[0, :, j * F_W:(j + 1) * F_W] = swapped[j]
    else:
        zf_ref[0] = p[:, o_f:o_c].astype(BF16)
    cc_ref[0] = p[:, o_c:o_g].astype(BF16)
    kt_ref[0] = kt.astype(BF16)
    gc_ref[0] = jnp.concatenate([gt, jnp.zeros((LANES - N_GATES, tm), F32)], axis=0).T
    for j in range(tm // LC):
        gt_ref[0, j] = gt[:, j * LC:(j + 1) * LC]


def _in_proj(x, modv, norm_g, wm, wkt, wgt, bg):
    b, n, d = x.shape
    tm = min(IN_TILE, n)
    nm = wm.shape[1]
    if n > DFT_DIRECT_MAX:
        n2cnt = n // DFT_N1
        assert tm % n2cnt == 0 and (tm // n2cnt) % 8 == 0
        zf_sds = jax.ShapeDtypeStruct((b, DFT_N1, n2cnt * F_W), F32)
        zf_spec = pl.BlockSpec((1, tm // n2cnt, n2cnt * F_W), lambda i, j: (i, j, 0))
        scratch = [pltpu.VMEM((tm // n2cnt, n2cnt, F_W), F32)]
    else:
        zf_sds = jax.ShapeDtypeStruct((b, n, F_W), BF16)
        zf_spec = pl.BlockSpec((1, tm, F_W), lambda i, j: (i, j, 0))
        scratch = []
    out_shape = (
        jax.ShapeDtypeStruct((b, n, 3 * M_W), BF16),
        jax.ShapeDtypeStruct((b, M_W, n), BF16),
        zf_sds,
        jax.ShapeDtypeStruct((b, n, 2 * C_W), BF16),
        jax.ShapeDtypeStruct((b, n // LC, 16, LC), F32),
        jax.ShapeDtypeStruct((b, n, LANES), F32),
    )
    return pl.pallas_call(
        _in_kernel,
        grid=(b, n // tm),
        in_specs=[pl.BlockSpec((1, tm, d), lambda i, j: (i, j, 0)),
                  pl.BlockSpec((1, 8, d), lambda i, j: (i, 0, 0)),
                  _const_spec((1, d)),
                  _const_spec((d, nm)),
                  _const_spec((M_W, d)),
                  _const_spec((16, d)),
                  _const_spec((16, 1))],
        out_specs=(pl.BlockSpec((1, tm, 3 * M_W), lambda i, j: (i, j, 0)),
                   pl.BlockSpec((1, M_W, tm), lambda i, j: (i, 0, j)),
                   zf_spec,
                   pl.BlockSpec((1, tm, 2 * C_W), lambda i, j: (i, j, 0)),
                   pl.BlockSpec((1, tm // LC, 16, LC), lambda i, j: (i, j, 0, 0)),
                   pl.BlockSpec((1, tm, LANES), lambda i, j: (i, j, 0))),
        out_shape=out_shape,
        scratch_shapes=scratch,
        compiler_params=_params("parallel", "arbitrary"),
        name="in_proj",
    )(x, modv, norm_g.reshape(1, d), wm, wkt, wgt, bg)


def _mlstm_kernel(qf_ref, vf_ref, ktf_ref, gtf_ref, gcf_ref, qb_ref, vb_ref, ktb_ref, gtb_ref, gcb_ref,
                  cf0_ref, cb0_ref, mf0_ref, mb0_ref, u3_ref, lo3_ref, ut3_ref, lot3_ref,
                  hf_ref, hb_ref, cf_ref, cb_ref, mf_ref, mb_ref):
    @pl.when(pl.program_id(1) == 0)
    def _():
        cf_ref[...] = cf0_ref[...]
        cb_ref[...] = cb0_ref[...]
        mf_ref[...] = mf0_ref[...]
        mb_ref[...] = mb0_ref[...]

    sub = qf_ref.shape[1] // LC
    t_idx = lax.broadcasted_iota(jnp.int32, (LC, LC), 0)
    s_idx = lax.broadcasted_iota(jnp.int32, (LC, LC), 1)
    ones = jnp.ones((LC, M_DH), BF16)
    state = [[(c_ref[0, h], m_ref[0, h, 0:1, 0:1]) for h in range(M_H)] for c_ref, m_ref in
             ((cf_ref, mf_ref), (cb_ref, mb_ref))]
    h_out = []
    for k in range(sub):
        kf, kb = k, sub - 1 - k
        gf = gtf_ref[0, kf]
        gb = gtb_ref[0, kb]
        rows_f = _dot(jnp.concatenate(_split3(_log_sigmoid(gf)), axis=1), u3_ref[...])
        rows_b = _dot(jnp.concatenate(_split3(_log_sigmoid(gb)), axis=1), lo3_ref[...])
        cols_f = _dot(ut3_ref[...], jnp.concatenate(
            _split3(_log_sigmoid(gcf_ref[0, kf * LC:(kf + 1) * LC, :])), axis=0))
        cols_b = _dot(lot3_ref[...], jnp.concatenate(
            _split3(_log_sigmoid(gcb_ref[0, kb * LC:(kb + 1) * LC, :])), axis=0))
        dirs = (
            (0, qf_ref, vf_ref, ktf_ref, kf, gf, rows_f, cols_f, s_idx <= t_idx, hf_ref, 0, 4, LC - 1),
            (1, qb_ref, vb_ref, ktb_ref, kb, gb, rows_b, cols_b, s_idx >= t_idx, hb_ref, 8, 12, 0),
        )
        for d, q_ref, v_ref, kt_ref, kc, g, rows, cols, mask, h_ref, i_off, f_off, last in dirs:
            rs = slice(kc * LC, (kc + 1) * LC)
            for h in range(M_H):
                hs = slice(h * M_DH, (h + 1) * M_DH)
                q = q_ref[0, rs, hs]
                kt = kt_ref[0, hs, rs]
                vaug = jnp.concatenate([v_ref[0, rs, hs], ones], axis=1)
                i_row = g[i_off + h:i_off + h + 1, :]
                b_row = rows[f_off + h:f_off + h + 1, :]
                b_col = jnp.broadcast_to(cols[:, f_off + h:f_off + h + 1], (LC, M_DH))
                b_last = b_row[:, last:last + 1]
                c0, m0 = state[d][h]

                am = jnp.where(mask, (i_row - b_row) * LOG2E, -jnp.inf)
                g2 = jnp.maximum(m0 * LOG2E, jnp.max(am, axis=-1, keepdims=True))
                g2 = jnp.broadcast_to(g2, (LC, M_DH))
                e = jnp.exp2(am - jnp.concatenate([g2, g2], axis=1))
                s = (_dot(q, kt) * e).astype(BF16)
                qi = (q.astype(F32) * jnp.exp2(m0 * LOG2E - g2)).astype(BF16)
                na = _dot(s, vaug) + _dot(qi, c0.astype(BF16))
                floor = jnp.exp2(-(b_col * LOG2E + g2))
                h_new = (na[:, :M_DH] / jnp.maximum(jnp.abs(na[:, M_DH:]), floor)).astype(h_ref.dtype)
                h_out.append((h_ref, rs, hs, h_new))

                gs = b_last - b_row + i_row
                m_new = jnp.maximum(b_last + m0, jnp.max(gs, axis=-1, keepdims=True))
                kw = (kt.astype(F32) * jnp.exp(gs - m_new)).astype(BF16)
                state[d][h] = (jnp.exp(b_last + m0 - m_new) * c0 + _dot(kw, vaug), m_new)

    for h_ref, rs, hs, h_new in h_out:
        h_ref[0, rs, hs] = h_new
    for d, (c_ref, m_ref) in enumerate(((cf_ref, mf_ref), (cb_ref, mb_ref))):
        for h in range(M_H):
            c_ref[0, h] = state[d][h][0]
            m_ref[0, h] = jnp.broadcast_to(state[d][h][1], (SUBLANES, LANES))


def _tri_constants():
    s = np.arange(LC)[:, None]
    t = np.arange(LC)[None, :]
    u = (s <= t).astype(np.float32)
    lo = (s >= t).astype(np.float32)
    u3 = np.concatenate([u, u, u], axis=0)
    lo3 = np.concatenate([lo, lo, lo], axis=0)
    return (jnp.asarray(u3, BF16), jnp.asarray(lo3, BF16),
            jnp.asarray(u3.T.copy(), BF16), jnp.asarray(lo3.T.copy(), BF16))


def _mlstm(qvo, kt, gt, gc, state_f, state_b):
    b, n, _ = qvo.shape
    sub = min(MLSTM_SUB, n // LC)
    rows = sub * LC
    nc = n // rows
    cf0, mf0 = state_f
    cb0, mb0 = state_b
    fwd = lambda i, c: (i, c, 0)
    bwd = lambda i, c: (i, nc - 1 - c, 0)
    st_c = pl.BlockSpec((1, M_H, M_DH, 2 * M_DH), lambda i, c: (i, 0, 0, 0))
    st_m = pl.BlockSpec((1, M_H, SUBLANES, LANES), lambda i, c: (i, 0, 0, 0))
    h_sds = jax.ShapeDtypeStruct((b, n, M_W), BF16)
    return pl.pallas_call(
        _mlstm_kernel,
        grid=(b, nc),
        in_specs=[pl.BlockSpec((1, rows, M_W), fwd),
                  pl.BlockSpec((1, rows, M_W), lambda i, c: (i, c, 1)),
                  pl.BlockSpec((1, M_W, rows), lambda i, c: (i, 0, c)),
                  pl.BlockSpec((1, sub, 16, LC), lambda i, c: (i, c, 0, 0)),
                  pl.BlockSpec((1, rows, LANES), fwd),
                  pl.BlockSpec((1, rows, M_W), bwd),
                  pl.BlockSpec((1, rows, M_W), lambda i, c: (i, nc - 1 - c, 1)),
                  pl.BlockSpec((1, M_W, rows), lambda i, c: (i, 0, nc - 1 - c)),
                  pl.BlockSpec((1, sub, 16, LC), lambda i, c: (i, nc - 1 - c, 0, 0)),
                  pl.BlockSpec((1, rows, LANES), bwd),
                  st_c, st_c, st_m, st_m,
                  _const_spec((3 * LC, LC)), _const_spec((3 * LC, LC)),
                  _const_spec((LC, 3 * LC)), _const_spec((LC, 3 * LC))],
        out_specs=(pl.BlockSpec((1, rows, M_W), fwd), pl.BlockSpec((1, rows, M_W), bwd),
                   st_c, st_c, st_m, st_m),
        out_shape=(h_sds, h_sds,
                   jax.ShapeDtypeStruct(cf0.shape, F32), jax.ShapeDtypeStruct(cb0.shape, F32),
                   jax.ShapeDtypeStruct(mf0.shape, F32), jax.ShapeDtypeStruct(mb0.shape, F32)),
        compiler_params=_params("parallel", "arbitrary"),
        name="mlstm",
    )(qvo, qvo, kt, gt, gc, qvo, qvo, kt, gt, gc, cf0, cb0, mf0, mb0, *_tri_constants())


def _channel_dft_matrix():
    gw = F_W // F_G
    j = np.arange(gw)[:, None]
    l = np.arange(gw)[None, :]
    ang = 2.0 * np.pi * j * l / gw
    eye = np.eye(F_G)
    bc = np.kron(eye, np.cos(ang)) / math.sqrt(gw)
    bs = np.kron(eye, np.sin(ang)) / math.sqrt(gw)
    return jnp.asarray(np.concatenate([bc, -bs], axis=1), BF16)


def _dft_cos_sin(n):
    k = np.arange(n)[:, None]
    m = np.arange(n)[None, :]
    ang = 2.0 * np.pi * ((k * m) % n) / n
    return np.cos(ang) / math.sqrt(n), np.sin(ang) / math.sqrt(n)


def _fourier1_kernel(z_ref, bcs_ref, f1_ref, o_ref, ts_ref):
    n2cnt = o_ref.shape[1] // 2

    def group(g, carry):
        for r in range(DFT_GROUP):
            j = g * DFT_GROUP + r
            off = pl.multiple_of(j * F_W, F_W)
            ab = _dot(z_ref[0, :, pl.ds(off, F_W)].astype(BF16), bcs_ref[...])
            u = jnp.concatenate([ab[:, :F_W], ab[:, F_W:]], axis=0).astype(BF16)
            t = _dot(f1_ref[j], u)
            ts_ref[0, r] = t[:DFT_N1]
            ts_ref[1, r] = t[DFT_N1:]
        for part in range(2):
            row0 = pl.multiple_of(part * n2cnt + g * DFT_GROUP, DFT_GROUP)
            swapped = pltpu.einshape("jkc->kjc", ts_ref[part])
            for k1 in range(DFT_N1):
                o_ref[0, pl.ds(row0, DFT_GROUP), k1 * F_W:(k1 + 1) * F_W] = swapped[k1].astype(BF16)
        return carry

    lax.fori_loop(0, n2cnt // DFT_GROUP, group, 0)


def _fourier2_kernel(t_ref, f2_ref, o_ref, s_ref):
    n2cnt = f2_ref.shape[0]
    width = t_ref.shape[2]
    tw = DFT_GROUP * F_W
    for c in range(width // tw):
        y = _dot(f2_ref[...], t_ref[0, :, c * tw:(c + 1) * tw])
        for kk in range(DFT_GROUP):
            s_ref[kk] = y[:, kk * F_W:(kk + 1) * F_W]
        swapped = pltpu.einshape("kjc->jkc", s_ref[...])
        for k2 in range(n2cnt):
            r0 = k2 * DFT_N1 + c * DFT_GROUP
            o_ref[0, r0:r0 + DFT_GROUP, :] = swapped[k2].astype(o_ref.dtype)


def _fourier_direct_kernel(z_ref, bcs_ref, f_ref, o_ref):
    ab = _dot(z_ref[0], bcs_ref[...])
    u = jnp.concatenate([ab[:, :F_W], ab[:, F_W:]], axis=0).astype(BF16)
    o_ref[0] = _dot(f_ref[...], u).astype(o_ref.dtype)


def _fourier(zf, n):
    b = zf.shape[0]
    bcs = _channel_dft_matrix()
    if n <= DFT_DIRECT_MAX:
        c, s = _dft_cos_sin(n)
        fmat = jnp.asarray(np.concatenate([c, s], axis=1), BF16)
        return pl.pallas_call(
            _fourier_direct_kernel,
            grid=(b,),
            in_specs=[pl.BlockSpec((1, n, F_W), lambda i: (i, 0, 0)),
                      _const_spec((F_W, 2 * F_W)), _const_spec((n, 2 * n))],
            out_specs=pl.BlockSpec((1, n, F_W), lambda i: (i, 0, 0)),
            out_shape=jax.ShapeDtypeStruct((b, n, F_W), BF16),
            compiler_params=_params("parallel"),
            name="fourier_direct",
        )(zf, bcs, fmat)

    n1 = DFT_N1
    n2 = n // n1
    assert n2 % DFT_GROUP == 0
    jj = np.arange(n2)[:, None, None]
    k1 = np.arange(n1)[None, :, None]
    aa = np.arange(n1)[None, None, :]
    ang = 2.0 * np.pi * (((jj * k1) % n) / n + ((k1 * aa) % n1) / n1)
    c1, s1 = np.cos(ang) / math.sqrt(n1), np.sin(ang) / math.sqrt(n1)
    f1 = jnp.asarray(np.concatenate([np.concatenate([c1, s1], axis=2),
                                     np.concatenate([-s1, c1], axis=2)], axis=1), BF16)
    c2, s2 = _dft_cos_sin(n2)
    f2 = jnp.asarray(np.concatenate([c2, s2], axis=1), BF16)
    width = n1 * F_W

    t = pl.pallas_call(
        _fourier1_kernel,
        grid=(b,),
        in_specs=[pl.BlockSpec((1, n1, n2 * F_W), lambda i: (i, 0, 0)),
                  _const_spec((F_W, 2 * F_W)), _const_spec((n2, 2 * n1, 2 * n1))],
        out_specs=pl.BlockSpec((1, 2 * n2, width), lambda i: (i, 0, 0)),
        out_shape=jax.ShapeDtypeStruct((b, 2 * n2, width), BF16),
        scratch_shapes=[pltpu.VMEM((2, DFT_GROUP, n1, F_W), F32)],
        compiler_params=_params("parallel"),
        name="fourier_stage1",
    )(zf, bcs, f1)
    return pl.pallas_call(
        _fourier2_kernel,
        grid=(b,),
        in_specs=[pl.BlockSpec((1, 2 * n2, width), lambda i: (i, 0, 0)),
                  _const_spec((n2, 2 * n2))],
        out_specs=pl.BlockSpec((1, n, F_W), lambda i: (i, 0, 0)),
        out_shape=jax.ShapeDtypeStruct((b, n, F_W), BF16),
        scratch_shapes=[pltpu.VMEM((DFT_GROUP, n2, F_W), F32)],
        compiler_params=_params("parallel"),
        name="fourier_stage2",
    )(t, f2)


def _group_mean(y, gm_ref):
    hi, lo = _split2(y)
    return _dot(hi, gm_ref[...]) + _dot(lo, gm_ref[...])


def _conv_kernel(cc_ref, dw_ref, db_ref, lg_ref, lb_ref, gm_ref, o_ref, a_ref, ph_ref):
    n = cc_ref.shape[1]
    rt = min(CONV_TILE, n)
    span = rt + 2 * CONV_PAD - 8
    a_ref[0:CONV_PAD, :] = jnp.zeros((CONV_PAD, C_W), F32)
    a_ref[n + CONV_PAD:n + 2 * CONV_PAD, :] = jnp.zeros((CONV_PAD, C_W), F32)

    def glu(i, carry):
        r = pl.multiple_of(i * rt, rt)
        blk = cc_ref[0, pl.ds(r, rt), :].astype(F32)
        a_ref[pl.ds(r + CONV_PAD, rt), :] = blk[:, :C_W] * jax.nn.sigmoid(blk[:, C_W:])
        return carry

    lax.fori_loop(0, n // rt, glu, 0)

    def conv(i, carry):
        r = pl.multiple_of(i * rt, rt)
        win = a_ref[pl.ds(r, rt + 2 * CONV_PAD), :]
        for s in range(1, 8):
            ph_ref[s - 1] = win[s:s + span]
        acc = jnp.broadcast_to(db_ref[...], (rt, C_W))
        for j in range(CONV_K):
            off = CONV_PAD - CONV_K // 2 + j
            base = 8 * (off // 8)
            if off % 8 == 0:
                tap = a_ref[pl.ds(r + base, rt), :]
            else:
                tap = ph_ref[off % 8 - 1, base:base + rt, :]
            acc = acc + tap * dw_ref[j:j + 1, :]
        d = acc - _group_mean(acc, gm_ref)
        var = _dot((d * d).astype(BF16), gm_ref[...])
        yn = d * lax.rsqrt(var + EPS) * lg_ref[...] + lb_ref[...]
        o_ref[0, pl.ds(r, rt), :] = _silu(yn).astype(o_ref.dtype)
        return carry

    lax.fori_loop(0, n // rt, conv, 0)


def _conv(cc, dw, db, ln_g, ln_b):
    b, n, _ = cc.shape
    gw = C_W // C_G
    gm = jnp.asarray(np.kron(np.eye(C_G), np.full((gw, gw), 1.0 / gw)), BF16)
    dw32 = jnp.concatenate([dw, jnp.zeros((32 - CONV_K, C_W), F32)], axis=0)
    return pl.pallas_call(
        _conv_kernel,
        grid=(b,),
        in_specs=[pl.BlockSpec((1, n, 2 * C_W), lambda i: (i, 0, 0)),
                  _const_spec((32, C_W)), _const_spec((1, C_W)), _const_spec((1, C_W)),
                  _const_spec((1, C_W)), _const_spec((C_W, C_W))],
        out_specs=pl.BlockSpec((1, n, C_W), lambda i: (i, 0, 0)),
        out_shape=jax.ShapeDtypeStruct((b, n, C_W), BF16),
        scratch_shapes=[pltpu.VMEM((n + 2 * CONV_PAD, C_W), F32),
                        pltpu.VMEM((7, min(CONV_TILE, n) + 2 * CONV_PAD - 8, C_W), F32)],
        compiler_params=_params("parallel"),
        name="conv_module",
    )(cc, dw32, db.reshape(1, C_W), ln_g.reshape(1, C_W), ln_b.reshape(1, C_W), gm)


def _top2_route(logits):
    row = lax.broadcasted_iota(jnp.int32, logits.shape, 0).astype(F32)
    lg = jnp.where(row < N_EXPERTS, logits, -jnp.inf)
    v1 = jnp.max(lg, axis=0, keepdims=True)
    i1 = jnp.min(jnp.where(lg == v1, row, 16.0), axis=0, keepdims=True)
    lg2 = jnp.where(row == i1, -jnp.inf, lg)
    v2 = jnp.max(lg2, axis=0, keepdims=True)
    i2 = jnp.min(jnp.where(lg2 == v2, row, 16.0), axis=0, keepdims=True)
    e2 = jnp.exp(v2 - v1)
    return row, i1, i2, 1.0 / (1.0 + e2), e2 / (1.0 + e2)


def _pack_halves(y):
    w = y.shape[1] // 2
    lo = pltpu.bitcast(y[:, :w].astype(BF16).astype(F32), jnp.uint32)
    hi = pltpu.bitcast(y[:, w:].astype(BF16).astype(F32), jnp.uint32)
    return (hi & jnp.uint32(0xFFFF0000)) | (lo >> 16)


def _unpack_halves(p):
    lo = pltpu.bitcast(p << 16, F32)
    hi = pltpu.bitcast(p & jnp.uint32(0xFFFF0000), F32)
    return jnp.concatenate([lo, hi], axis=1)


def _merge_heads(hf_ref, hb_ref, o_ref, yf_ref, yc_ref, x_ref, mod_ref, hg_ref, g2_ref, wo_ref):
    hsum = hf_ref[0].astype(F32) + hb_ref[0].astype(F32)
    heads = [_rms(hsum[:, h * M_DH:(h + 1) * M_DH]) for h in range(M_H)]
    ym = jax.nn.sigmoid(o_ref[0].astype(F32)) * (jnp.concatenate(heads, axis=1) * hg_ref[...])
    proj = (_dot(ym.astype(BF16), wo_ref[0:M_W, :])
            + _dot(yf_ref[0], wo_ref[M_W:M_W + F_W, :])
            + _dot(yc_ref[0], wo_ref[M_W + F_W:, :]))
    xn = x_ref[0] + mod_ref[0, 2:3, :] * proj
    hx = (_rms(xn) * g2_ref[...]) * (1.0 + mod_ref[0, 4:5, :]) + mod_ref[0, 3:4, :]
    return xn, hx


def _out_ffn_kernel(hf_ref, hb_ref, o_ref, yf_ref, yc_ref, x_ref, mod_ref, hg_ref, g2_ref, wo_ref,
                    wgu_ref, wd_ref, xo_ref):
    xn, hx = _merge_heads(hf_ref, hb_ref, o_ref, yf_ref, yc_ref, x_ref, mod_ref, hg_ref, g2_ref, wo_ref)
    xo_ref[0] = xn + mod_ref[0, 5:6, :] * _swiglu_tile(hx.astype(BF16), wgu_ref, wd_ref)


def _out_kernel(hf_ref, hb_ref, o_ref, yf_ref, yc_ref, x_ref, mod_ref, hg_ref, g2_ref, wo_ref,
                wrt_ref, brt_ref, tri_ref, xo_ref, hx_ref, route_ref, cnt_ref, carry_ref):
    xn, hx = _merge_heads(hf_ref, hb_ref, o_ref, yf_ref, yc_ref, x_ref, mod_ref, hg_ref, g2_ref, wo_ref)
    hh, hl = _split2(hx)
    wh, wl = _split2(wrt_ref[...])
    logits = _dot_nt(wh, hh) + _dot_nt(wl, hh) + _dot_nt(wh, hl) + brt_ref[:, 0:1]
    row, i1, i2, w1, w2 = _top2_route(logits)

    @pl.when((pl.program_id(0) == 0) & (pl.program_id(1) == 0))
    def _():
        carry_ref[...] = jnp.zeros_like(carry_ref)

    hit = jnp.where((row == i1) | (row == i2), 1.0, 0.0)
    before = _dot(hit.astype(BF16), tri_ref[...]) + carry_ref[:, 0:1]
    rank1 = jnp.sum(jnp.where(row == i1, before, 0.0), axis=0, keepdims=True)
    rank2 = jnp.sum(jnp.where(row == i2, before, 0.0), axis=0, keepdims=True)
    total = carry_ref[...] + jnp.sum(hit, axis=1, keepdims=True)
    xo_ref[0] = xn
    hx_ref[0] = _pack_halves(hx)
    zero = jnp.zeros_like(w1)
    route_ref[0] = jnp.concatenate([i1, i2, w1, w2, rank1, rank2, zero, zero], axis=0)
    carry_ref[...] = total
    cnt_ref[...] = total.astype(jnp.int32)


def _merge_specs(hf, hb, qvo, yf, yc, x, modv, head_g, norm2_g, w_out, tile=ROW_TILE):
    b, n, d = x.shape
    tm = min(tile, n)
    row = lambda w: pl.BlockSpec((1, tm, w), lambda i, j: (i, j, 0))
    in_specs = [row(M_W), row(M_W),
                pl.BlockSpec((1, tm, M_W), lambda i, j: (i, j, 2)),
                row(F_W), row(C_W), row(d),
                pl.BlockSpec((1, 8, d), lambda i, j: (i, 0, 0)),
                _const_spec((1, M_W)), _const_spec((1, d)), _const_spec((d, d))]
    args = [hf, hb, qvo, yf, yc, x, modv, head_g.reshape(1, M_W), norm2_g.reshape(1, d), w_out]
    return tm, row, in_specs, args


def _out_ffn(hf, hb, qvo, yf, yc, x, modv, head_g, norm2_g, w_out, wgu, wd):
    b, n, d = x.shape
    tm, row, in_specs, args = _merge_specs(hf, hb, qvo, yf, yc, x, modv, head_g, norm2_g, w_out)
    return pl.pallas_call(
        _out_ffn_kernel,
        grid=(b, n // tm),
        in_specs=in_specs + [_const_spec(wgu.shape), _const_spec(wd.shape)],
        out_specs=row(d),
        out_shape=jax.ShapeDtypeStruct((b, n, d), F32),
        compiler_params=_params("parallel", "arbitrary"),
        name="out_ffn",
    )(*args, wgu, wd)


def _out_route(hf, hb, qvo, yf, yc, x, modv, head_g, norm2_g, w_out, wrt, brt):
    b, n, d = x.shape
    tm, row, in_specs, args = _merge_specs(hf, hb, qvo, yf, yc, x, modv, head_g, norm2_g, w_out,
                                           tile=2 * ROW_TILE)
    tri = jnp.asarray(np.triu(np.ones((tm, tm), np.float32), 1), BF16)
    return pl.pallas_call(
        _out_kernel,
        grid=(b, n // tm),
        in_specs=in_specs + [_const_spec((16, d)), _const_spec((16, 128)), _const_spec((tm, tm))],
        out_specs=(row(d), row(d // 2), pl.BlockSpec((1, 8, tm), lambda i, j: (i, 0, j)),
                   _const_spec((16, 128))),
        out_shape=(jax.ShapeDtypeStruct((b, n, d), F32), jax.ShapeDtypeStruct((b, n, d // 2), jnp.uint32),
                   jax.ShapeDtypeStruct((b, 8, n), F32), jax.ShapeDtypeStruct((16, 128), jnp.int32)),
        scratch_shapes=[pltpu.VMEM((16, 128), F32)],
        compiler_params=_params("arbitrary", "arbitrary"),
        name="out_route",
    )(*args, wrt, brt, tri)


def _swiglu_tile(hx, wgu_ref, wd_ref):
    acc = jnp.zeros((hx.shape[0], D_MODEL), F32)
    for j in range(FFN_HIDDEN // HID_TILE):
        cs = slice(j * HID_TILE, (j + 1) * HID_TILE)
        us = slice(FFN_HIDDEN + j * HID_TILE, FFN_HIDDEN + (j + 1) * HID_TILE)
        a = _silu(_dot(hx, wgu_ref[:, cs])) * _dot(hx, wgu_ref[:, us])
        acc = acc + _dot(a.astype(BF16), wd_ref[cs, :])
    return acc


def _sc_worker_rows(n_rows):
    assert n_rows % (SC_WORKERS * SC_CHUNK) == 0
    return n_rows // SC_WORKERS


def _sc_scatter_rows(x, dest, n_out):
    t, w = x.shape
    per_w = _sc_worker_rows(t)
    mesh = plsc.VectorSubcoreMesh(core_axis_name="c", subcore_axis_name="s")

    @functools.partial(
        pl.kernel, mesh=mesh,
        out_type=jax.ShapeDtypeStruct((n_out, w), x.dtype),
        scratch_types=[pltpu.VMEM((1, SC_CHUNK), jnp.int32),
                       pltpu.VMEM((SC_CHUNK, w), x.dtype),
                       pltpu.SemaphoreType.DMA],
    )
    def scatter(x_hbm, dest_hbm, out_hbm, idx_v, rows_v, sem):
        base = (lax.axis_index("s") * SC_CORES + lax.axis_index("c")) * per_w

        @pl.loop(0, per_w // SC_CHUNK)
        def _(g):
            off = base + g * SC_CHUNK
            pltpu.sync_copy(x_hbm.at[pl.ds(off, SC_CHUNK)], rows_v)
            for k in range(2):
                pltpu.sync_copy(dest_hbm.at[pl.ds(k, 1), pl.ds(off, SC_CHUNK)], idx_v)
                pltpu.async_copy(rows_v, out_hbm.at[idx_v.at[0]], sem).wait()

    return scatter(x, dest)


def _sc_gather_rows(table, idx):
    r = idx.shape[0]
    w = table.shape[1]
    per_w = _sc_worker_rows(r)
    mesh = plsc.VectorSubcoreMesh(core_axis_name="c", subcore_axis_name="s")

    @functools.partial(
        pl.kernel, mesh=mesh,
        out_type=jax.ShapeDtypeStruct((r, w), table.dtype),
        scratch_types=[pltpu.VMEM((1, SC_CHUNK), jnp.int32),
                       pltpu.VMEM((SC_CHUNK, w), table.dtype),
                       pltpu.SemaphoreType.DMA],
    )
    def gather(table_hbm, idx_hbm, out_hbm, idx_v, rows_v, sem):
        base = (lax.axis_index("s") * SC_CORES + lax.axis_index("c")) * per_w

        @pl.loop(0, per_w // SC_CHUNK)
        def _(g):
            off = base + g * SC_CHUNK
            pltpu.sync_copy(idx_hbm.at[pl.ds(0, 1), pl.ds(off, SC_CHUNK)], idx_v)
            pltpu.async_copy(table_hbm.at[idx_v.at[0]], rows_v, sem).wait()
            pltpu.sync_copy(rows_v, out_hbm.at[pl.ds(off, SC_CHUNK)])

    return gather(table, idx.reshape(1, r))


def _group_kernel(te_ref, nv_ref, xs_ref, wgu_ref, wd_ref, ys_ref):
    @pl.when(pl.program_id(0) < nv_ref[0])
    def _():
        hx = _unpack_halves(xs_ref[...]).astype(BF16)
        ys_ref[...] = _pack_halves(_swiglu_tile(hx, wgu_ref.at[0], wd_ref.at[0]))


def _grouped_swiglu(xs, tile_expert, n_valid, wgu, wd):
    r, wp = xs.shape
    grid_spec = pltpu.PrefetchScalarGridSpec(
        num_scalar_prefetch=2,
        grid=(r // GROUP_TILE,),
        in_specs=[pl.BlockSpec((GROUP_TILE, wp), lambda i, te, nv: (i, 0)),
                  pl.BlockSpec((1,) + wgu.shape[1:], lambda i, te, nv: (te[i], 0, 0)),
                  pl.BlockSpec((1,) + wd.shape[1:], lambda i, te, nv: (te[i], 0, 0))],
        out_specs=pl.BlockSpec((GROUP_TILE, wp), lambda i, te, nv: (i, 0)),
    )
    return pl.pallas_call(
        _group_kernel,
        grid_spec=grid_spec,
        out_shape=jax.ShapeDtypeStruct((r, wp), jnp.uint32),
        compiler_params=_params("arbitrary"),
        name="grouped_swiglu",
    )(tile_expert, n_valid, xs, wgu, wd)


def _combine_kernel(x_ref, y_ref, route_ref, mod_ref, fg_ref, *rest):
    o_ref = rest[-1]
    r = route_ref[0]
    moe = r[:, 2:3] * _unpack_halves(y_ref[0]) + r[:, 3:4] * _unpack_halves(y_ref[1])
    xn = x_ref[0] + mod_ref[0, 5:6, :] * moe
    o_ref[0] = _rms(xn) * fg_ref[...]


def _combine_final(x, y2, route, modv, final_g, bi, prev):
    b, n, d = x.shape
    tm = min(2 * ROW_TILE, n)
    row = lambda w: pl.BlockSpec((1, tm, w), lambda j: (bi, j, 0))
    in_specs = [row(d), pl.BlockSpec((2, tm, d // 2), lambda j: (0, j, 0)), row(8),
                pl.BlockSpec((1, 8, d), lambda j: (bi, 0, 0)), _const_spec((1, d))]
    args = [x, y2, route, modv, final_g.reshape(1, d)]
    aliases = {}
    if prev is not None:
        in_specs.append(pl.BlockSpec(memory_space=pl.ANY))
        args.append(prev)
        aliases = {len(args) - 1: 0}
    return pl.pallas_call(
        _combine_kernel,
        grid=(n // tm,),
        in_specs=in_specs,
        out_specs=row(d),
        out_shape=jax.ShapeDtypeStruct((b, n, d), F32),
        input_output_aliases=aliases,
        compiler_params=_params("arbitrary"),
        name="moe_combine",
    )(*args)


def _moe_final(hxp, x, route, counts, modv, final_g, wgu, wd):
    b, n, d = x.shape
    t = b * n
    counts = counts[:N_EXPERTS, 0]
    padded = ((counts + GROUP_TILE - 1) // GROUP_TILE) * GROUP_TILE
    ends = jnp.cumsum(padded)
    starts = ends - padded
    rows = jnp.transpose(route, (1, 0, 2)).reshape(8, t)
    experts = rows[0:2].astype(jnp.int32)
    dest = rows[4:6].astype(jnp.int32)
    for e in range(N_EXPERTS):
        dest = dest + jnp.where(experts == e, starts[e], 0)
    route = jnp.transpose(route, (0, 2, 1))
    n_rows = 2 * t + N_EXPERTS * GROUP_TILE
    tile_start = jnp.arange(n_rows // GROUP_TILE, dtype=jnp.int32) * GROUP_TILE
    tile_expert = jnp.minimum(jnp.sum(ends[None, :] <= tile_start[:, None], axis=1), N_EXPERTS - 1).astype(jnp.int32)
    n_valid = (ends[-1:] // GROUP_TILE).astype(jnp.int32)

    xs = _sc_scatter_rows(hxp.reshape(t, d // 2), dest, n_rows)
    ys = _grouped_swiglu(xs, tile_expert, n_valid, wgu, wd)
    out = None
    for bi in range(b):
        y2 = _sc_gather_rows(ys, dest[:, bi * n:(bi + 1) * n].reshape(2 * n))
        out = _combine_final(x, y2.reshape(2, n, d // 2), route, modv, final_g, bi, out)
    return out


def _in_weights(w):
    wm = jnp.concatenate([w[:, O_Q:O_K], w[:, O_V:O_G], w[:, O_F:P_IN]], axis=1).astype(BF16)
    return wm, w[:, O_K:O_V].T.astype(BF16), w[:, O_G:O_F].T.astype(BF16)


def _zero_state(b):
    return (jnp.zeros((b, M_H, M_DH, 2 * M_DH), F32), jnp.full((b, M_H, SUBLANES, LANES), -1e30, F32))


def _mod_rows(m):
    r = m.shape[0]
    m = m.reshape(r, 6, D_MODEL)
    return jnp.concatenate([m, jnp.zeros((r, 2, D_MODEL), F32)], axis=1)


def kernel(x, c, ctx, c_ctx, norm1_g, norm2_g, w_mod, b_mod, w_in, b_gate, mlstm_norm_g, conv_dw, conv_db, conv_norm_g, conv_norm_b, w_out, ffn_w_gate_up, ffn_w_down, moe_w_router, moe_b_router, moe_w_gate_up, moe_w_down, final_norm_g):
    b = x.shape[0]
    depth = w_in.shape[0]
    assert depth == 2 and b <= 7
    cond8 = jnp.concatenate([c, c_ctx[None, :], jnp.zeros((7 - b, D_MODEL), F32)], axis=0)
    xc = ctx
    mods_all = _modulation(cond8, w_mod, b_mod)
    for layer in range(depth):
        last = layer == depth - 1
        mods = mods_all[layer]
        mod_lat = _mod_rows(mods[:b])
        mod_ctx = jnp.broadcast_to(_mod_rows(mods[b:b + 1]), (b, 8, D_MODEL))
        wm, wkt, wgt = _in_weights(w_in[layer])
        bg = b_gate[layer].reshape(16, 1)
        w_o = w_out[layer].astype(BF16)

        qvo_c, kt_c, zf_c, cc_c, gt_c, gc_c = _in_proj(xc, mod_ctx, norm1_g[layer], wm, wkt, wgt, bg)
        qvo_l, kt_l, zf_l, cc_l, gt_l, gc_l = _in_proj(x, mod_lat, norm1_g[layer], wm, wkt, wgt, bg)

        hf_c, hb_c, cf, cb, mf, mb = _mlstm(qvo_c, kt_c, gt_c, gc_c, _zero_state(b), _zero_state(b))
        hf_l, hb_l, _, _, _, _ = _mlstm(qvo_l, kt_l, gt_l, gc_l, (cf, mf), (cb, mb))

        conv_args = (conv_dw[layer], conv_db[layer], conv_norm_g[layer], conv_norm_b[layer])
        yf_l = _fourier(zf_l, x.shape[1])
        yc_l = _conv(cc_l, *conv_args)
        if not last:
            wgu = ffn_w_gate_up[layer // 2].astype(BF16)
            wd = ffn_w_down[layer // 2].astype(BF16)
            x = _out_ffn(hf_l, hb_l, qvo_l, yf_l, yc_l, x, mod_lat, mlstm_norm_g[layer],
                         norm2_g[layer], w_o, wgu, wd)
            yf_c = _fourier(zf_c, xc.shape[1])
            yc_c = _conv(cc_c, *conv_args)
            xc = _out_ffn(hf_c, hb_c, qvo_c, yf_c, yc_c, xc, mod_ctx, mlstm_norm_g[layer],
                          norm2_g[layer], w_o, wgu, wd)
        else:
            idx = layer // 2
            wrt = jnp.concatenate([moe_w_router[idx].T, jnp.zeros((16 - N_EXPERTS, D_MODEL), F32)], axis=0)
            brt = jnp.broadcast_to(jnp.concatenate([moe_b_router[idx], jnp.zeros((16 - N_EXPERTS,), F32)])[:, None],
                                   (16, 128))
            x, hxp, route, counts = _out_route(hf_l, hb_l, qvo_l, yf_l, yc_l, x, mod_lat,
                                               mlstm_norm_g[layer], norm2_g[layer], w_o, wrt, brt)
            x = _moe_final(hxp, x, route, counts, mod_lat, final_norm_g,
                           moe_w_gate_up[idx].astype(BF16), moe_w_down[idx].astype(BF16))
    return x
```
